```python
import math
import jax, jax.numpy as jnp
from jax import lax
import numpy as np

D_MODEL = 1024
BATCH = 32
SEQ = 256
DEPTH = 2
DEC_BATCH = 2
DEC_SEQ = 1024
PAST_LEN = 256

GRID_W = 64
HEAD_DIM = 64
N_HEADS = 16
N_KV_HEADS = 4
Q_BLOCK = 128
ROPE_THETA = 10000.0
AXIS_DIM = HEAD_DIM // 2
HALF = HEAD_DIM // 2
SSM_GROUP = 16
SSM_WIDTH = 512
N_GROUPS = SSM_WIDTH // SSM_GROUP
STATE_N = 64
N_DIRS = 2
D_FF = 2816
N_EXPERTS = 8
TOP_K = 2
D_FF_EXPERT = 3584
N_DENSE = (DEPTH + 1) // 2
N_MOE = DEPTH // 2
EPS = 1e-6
Q_W = N_HEADS * HEAD_DIM
KV_W = N_KV_HEADS * HEAD_DIM
IN_W = SSM_WIDTH + Q_W + 2 * KV_W + 2 * D_MODEL
IN_SPLITS = [SSM_WIDTH, SSM_WIDTH + Q_W, SSM_WIDTH + Q_W + KV_W, SSM_WIDTH + Q_W + 2 * KV_W,
             SSM_WIDTH + Q_W + 2 * KV_W + D_MODEL]

kernel_name = 'hybrid_s5_gqa_prefix_diffusion_step'


def _rmsnorm(x, g):
    xf = x.astype(jnp.float32)
    y = xf * lax.rsqrt(jnp.mean(xf * xf, axis=-1, keepdims=True) + EPS)
    return (y * g.astype(jnp.float32)).astype(x.dtype)


def _axial_rope(n_tokens):
    rows = n_tokens // GRID_W
    row = jnp.repeat(jnp.arange(rows, dtype=jnp.float32), GRID_W)
    col = jnp.tile(jnp.arange(GRID_W, dtype=jnp.float32), rows)
    inv = ROPE_THETA ** (-jnp.arange(0, AXIS_DIM, 2, dtype=jnp.float32) / AXIS_DIM)
    ang = jnp.concatenate([row[:, None] * inv, col[:, None] * inv], axis=-1)
    return jnp.cos(ang), jnp.sin(ang)


def _apply_rope(x, cos, sin):
    xf = x.astype(jnp.float32)
    x1, x2 = xf[..., :HALF], xf[..., HALF:]
    c = cos[None, :, None, :]
    s = sin[None, :, None, :]
    return jnp.concatenate([x1 * c - x2 * s, x2 * c + x1 * s], axis=-1).astype(x.dtype)


def _block_attention(q, k, v):
    b, lq, h, dh = q.shape
    rep = h // N_KV_HEADS
    nblk = lq // Q_BLOCK
    qb = q.astype(jnp.float32).reshape(b, nblk, Q_BLOCK, N_KV_HEADS, rep, dh).transpose(1, 0, 2, 3, 4, 5)
    kf = k.astype(jnp.float32)
    vf = v.astype(jnp.float32)
    scale = 1.0 / math.sqrt(dh)

    def one_block(qblk):
        s = jnp.einsum('bqgrd,bkgd->bgrqk', qblk, kf) * scale
        p = jax.nn.softmax(s, axis=-1)
        return jnp.einsum('bgrqk,bkgd->bqgrd', p, vf)

    out = lax.map(one_block, qb)
    return out.transpose(1, 0, 2, 3, 4, 5).reshape(b, lq, h * dh).astype(q.dtype)


def _s5_discretise(a_re, a_im, log_dt, b_re, b_im):
    A = lax.complex(a_re.astype(jnp.float32), a_im.astype(jnp.float32))
    dt = jnp.exp(log_dt.astype(jnp.float32))[:, None]
    a_bar = jnp.exp(A * dt)
    B = lax.complex(b_re.astype(jnp.float32), b_im.astype(jnp.float32))
    b_bar = ((a_bar - 1.0) / A)[..., None] * B
    return a_bar, b_bar


def _s5_scan(ug, a_bar, b_bar, h0, reverse):
    bu = jnp.einsum('gnp,blgp->blgn', b_bar, ug.astype(jnp.complex64))
    if h0 is not None:
        idx = -1 if reverse else 0
        bu = bu.at[:, idx].add(a_bar[None] * h0)
    a = jnp.broadcast_to(a_bar, bu.shape)

    def combine(left, right):
        a_l, b_l = left
        a_r, b_r = right
        return a_r * a_l, a_r * b_l + b_r

    _, hs = lax.associative_scan(combine, (a, bu), reverse=reverse, axis=1)
    return hs


def _token_mixer(h, l, rope, ctx, P):
    b, L, _ = h.shape
    z = h @ P['w_in'][l]
    u, q, k, v, gs_logit, ga_logit = jnp.split(z, IN_SPLITS, axis=-1)
    q = _rmsnorm(q.reshape(b, L, N_HEADS, HEAD_DIM), P['q_norm_g'][l])
    k = _rmsnorm(k.reshape(b, L, N_KV_HEADS, HEAD_DIM), P['k_norm_g'][l])
    v = v.reshape(b, L, N_KV_HEADS, HEAD_DIM)
    if ctx is None:
        attn = _block_attention(q, k, v)
    else:
        ck, cv, cs = ctx
        cos, sin = rope
        qr = _apply_rope(q, cos, sin)
        kr = _apply_rope(k, cos, sin)
        k_all = jnp.concatenate([ck.astype(k.dtype), kr], axis=1)
        v_all = jnp.concatenate([cv.astype(v.dtype), v], axis=1)
        attn = _block_attention(qr, k_all, v_all)
    attn_br = attn @ P['w_attn_br'][l]
    uf = u.astype(jnp.float32)
    ug = uf.reshape(b, L, N_GROUPS, SSM_GROUP)
    y = uf * P['ssm_d'][l].astype(jnp.float32)
    finals = []
    for d in range(N_DIRS):
        a_bar, b_bar = _s5_discretise(P['ssm_a_re'][l, d], P['ssm_a_im'][l, d], P['ssm_log_dt'][l, d],
                                      P['ssm_b_re'][l, d], P['ssm_b_im'][l, d])
        h0 = None if ctx is None else lax.complex(cs[:, d, ..., 0].astype(jnp.float32),
                                                   cs[:, d, ..., 1].astype(jnp.float32))
        hs = _s5_scan(ug, a_bar, b_bar, h0, reverse=(d == 1))
        cmat = lax.complex(P['ssm_c_re'][l, d].astype(jnp.float32), P['ssm_c_im'][l, d].astype(jnp.float32))
        y = y + jnp.real(jnp.einsum('gpn,blgn->blgp', cmat, hs)).reshape(b, L, SSM_WIDTH)
        if ctx is None:
            hf = hs[:, 0] if d == 1 else hs[:, -1]
            finals.append(jnp.stack([jnp.real(hf), jnp.imag(hf)], axis=-1))
    g = jax.nn.gelu(y)
    glu_a, glu_b = jnp.split(g @ P['w_glu'][l].astype(jnp.float32), 2, axis=-1)
    ssm_br = (glu_a * jax.nn.sigmoid(glu_b)).astype(h.dtype)
    merged = jax.nn.sigmoid(gs_logit) * ssm_br + jax.nn.sigmoid(ga_logit) * attn_br
    out = merged @ P['w_out'][l]
    if ctx is None:
        return out, (k, v, jnp.stack(finals, axis=1).astype(h.dtype))
    return out, None


def _swiglu(h, wg, wu, wd):
    return (jax.nn.silu(h @ wg) * (h @ wu)) @ wd


def _moe(h, router, wg, wu, wd):
    logits = jnp.einsum('bld,de->ble', h.astype(jnp.float32), router.astype(jnp.float32))
    top_v, top_i = lax.top_k(logits, TOP_K)
    top_w = jax.nn.softmax(top_v, axis=-1)
    gates = jnp.sum(jax.nn.one_hot(top_i, N_EXPERTS, dtype=jnp.float32) * top_w[..., None], axis=-2)
    gates = gates.astype(h.dtype)
    out = jnp.zeros_like(h)
    for e in range(N_EXPERTS):
        out = out + gates[..., e:e + 1] * _swiglu(h, wg[e], wu[e], wd[e])
    return out


def _layer(x, c_act, l, rope, ctx, P):
    mod = (c_act @ P['w_ada'][l] + P['b_ada'][l])[:, None, :]
    sh1, sc1, g1, sh2, sc2, g2 = jnp.split(mod, 6, axis=-1)
    h = _rmsnorm(x, P['norm1_g'][l]) * (1.0 + sc1) + sh1
    mix, ctx_out = _token_mixer(h, l, rope, ctx, P)
    x = x + g1 * mix
    h = _rmsnorm(x, P['norm2_g'][l]) * (1.0 + sc2) + sh2
    if l % 2 == 0:
        i = l // 2
        f = _swiglu(h, P['ffn_w_gate'][i], P['ffn_w_up'][i], P['ffn_w_down'][i])
    else:
        i = l // 2
        f = _moe(h, P['moe_router'][i], P['moe_w_gate'][i], P['moe_w_up'][i], P['moe_w_down'][i])
    x = x + g2 * f
    return x, ctx_out


def setup_inputs(seed: int = 0) -> dict:
    key = jax.random.key(seed)
    keys = jax.random.split(key, 48)
    counter = [0]

    def nxt():
        k = keys[counter[0]]
        counter[0] += 1
        return k

    def nrm(shape, scale=1.0):
        return scale * jax.random.normal(nxt(), shape, jnp.float32)

    D = D_MODEL
    a_im_base = jnp.pi * jnp.arange(STATE_N, dtype=jnp.float32)
    inp = {}
    inp['x_prompt'] = nrm((BATCH, SEQ, D))
    inp['x_sample'] = nrm((DEC_BATCH, DEC_SEQ, D))
    inp['cache_k'] = nrm((DEC_BATCH, DEPTH, PAST_LEN, N_KV_HEADS, HEAD_DIM))
    inp['cache_v'] = nrm((DEC_BATCH, DEPTH, PAST_LEN, N_KV_HEADS, HEAD_DIM))
    inp['state_ssm'] = nrm((DEC_BATCH, DEPTH, N_DIRS, N_GROUPS, STATE_N, 2), 0.5)
    inp['c'] = nrm((DEC_BATCH, D))
    inp['c_ctx'] = nrm((D,))
    inp['w_ada'] = nrm((DEPTH, D, 6 * D), 0.5 * D ** -0.5)
    inp['b_ada'] = nrm((DEPTH, 6 * D), 0.02)
    inp['norm1_g'] = 1.0 + nrm((DEPTH, D), 0.02)
    inp['norm2_g'] = 1.0 + nrm((DEPTH, D), 0.02)
    inp['w_in'] = nrm((DEPTH, D, IN_W), D ** -0.5)
    inp['q_norm_g'] = 1.0 + nrm((DEPTH, HEAD_DIM), 0.02)
    inp['k_norm_g'] = 1.0 + nrm((DEPTH, HEAD_DIM), 0.02)
    inp['ssm_a_re'] = -0.5 + nrm((DEPTH, N_DIRS, N_GROUPS, STATE_N), 0.01)
    inp['ssm_a_im'] = a_im_base + nrm((DEPTH, N_DIRS, N_GROUPS, STATE_N), 0.01)
    inp['ssm_log_dt'] = jax.random.uniform(nxt(), (DEPTH, N_DIRS, N_GROUPS), jnp.float32,
                                           minval=math.log(1e-3), maxval=math.log(1e-1))
    b_scale = (2.0 * SSM_GROUP) ** -0.5
    c_scale = (2.0 * STATE_N) ** -0.5
    inp['ssm_b_re'] = nrm((DEPTH, N_DIRS, N_GROUPS, STATE_N, SSM_GROUP), b_scale)
    inp['ssm_b_im'] = nrm((DEPTH, N_DIRS, N_GROUPS, STATE_N, SSM_GROUP), b_scale)
    inp['ssm_c_re'] = nrm((DEPTH, N_DIRS, N_GROUPS, SSM_GROUP, STATE_N), c_scale)
    inp['ssm_c_im'] = nrm((DEPTH, N_DIRS, N_GROUPS, SSM_GROUP, STATE_N), c_scale)
    inp['ssm_d'] = nrm((DEPTH, SSM_WIDTH), 0.5)
    inp['w_glu'] = nrm((DEPTH, SSM_WIDTH, 2 * D), SSM_WIDTH ** -0.5)
    inp['w_attn_br'] = nrm((DEPTH, Q_W, D), Q_W ** -0.5)
    inp['w_out'] = nrm((DEPTH, D, D), D ** -0.5)
    inp['ffn_w_gate'] = nrm((N_DENSE, D, D_FF), D ** -0.5)
    inp['ffn_w_up'] = nrm((N_DENSE, D, D_FF), D ** -0.5)
    inp['ffn_w_down'] = nrm((N_DENSE, D_FF, D), D_FF ** -0.5)
    inp['moe_router'] = nrm((N_MOE, D, N_EXPERTS), D ** -0.5)
    inp['moe_w_gate'] = nrm((N_MOE, N_EXPERTS, D, D_FF_EXPERT), D ** -0.5)
    inp['moe_w_up'] = nrm((N_MOE, N_EXPERTS, D, D_FF_EXPERT), D ** -0.5)
    inp['moe_w_down'] = nrm((N_MOE, N_EXPERTS, D_FF_EXPERT, D), D_FF_EXPERT ** -0.5)
    inp['final_norm_g'] = 1.0 + nrm((D,), 0.02)
    return inp


def reference(x_prompt, x_sample, cache_k, cache_v, state_ssm, c, c_ctx, w_ada, b_ada, norm1_g, norm2_g,
              w_in, q_norm_g, k_norm_g, ssm_a_re, ssm_a_im, ssm_log_dt, ssm_b_re, ssm_b_im, ssm_c_re,
              ssm_c_im, ssm_d, w_glu, w_attn_br, w_out, ffn_w_gate, ffn_w_up, ffn_w_down, moe_router,
              moe_w_gate, moe_w_up, moe_w_down, final_norm_g):
    P = dict(w_ada=w_ada, b_ada=b_ada, norm1_g=norm1_g, norm2_g=norm2_g, w_in=w_in, q_norm_g=q_norm_g,
             k_norm_g=k_norm_g, ssm_a_re=ssm_a_re, ssm_a_im=ssm_a_im, ssm_log_dt=ssm_log_dt,
             ssm_b_re=ssm_b_re, ssm_b_im=ssm_b_im, ssm_c_re=ssm_c_re, ssm_c_im=ssm_c_im, ssm_d=ssm_d,
             w_glu=w_glu, w_attn_br=w_attn_br, w_out=w_out, ffn_w_gate=ffn_w_gate, ffn_w_up=ffn_w_up,
             ffn_w_down=ffn_w_down, moe_router=moe_router, moe_w_gate=moe_w_gate, moe_w_up=moe_w_up,
             moe_w_down=moe_w_down)

    xp = x_prompt
    c_ctx_act = jax.nn.silu(c_ctx)[None, :]
    ks, vs, ss = [], [], []
    for l in range(DEPTH):
        xp, (k_l, v_l, s_l) = _layer(xp, c_ctx_act, l, None, None, P)
        ks.append(k_l)
        vs.append(v_l)
        ss.append(s_l)
    y_prompt = _rmsnorm(xp, final_norm_g)
    new_cache_k = jnp.stack(ks, axis=1)
    new_cache_v = jnp.stack(vs, axis=1)
    new_state_ssm = jnp.stack(ss, axis=1)

    xs = x_sample
    rope = _axial_rope(xs.shape[1])
    c_act = jax.nn.silu(c)
    for l in range(DEPTH):
        xs, _ = _layer(xs, c_act, l, rope, (cache_k[:, l], cache_v[:, l], state_ssm[:, l]), P)
    y_sample = _rmsnorm(xs, final_norm_g)

    return (y_prompt, y_sample, new_cache_k, new_cache_v, new_state_ssm)
```

```python
import functools
import math

import jax
import jax.numpy as jnp
from jax import lax
from jax.experimental import pallas as pl
from jax.experimental.pallas import tpu as pltpu

F32 = jnp.float32
BF16 = jnp.bfloat16

D_MODEL = 1024
BATCH = 32
SEQ = 256
DEPTH = 2
DEC_BATCH = 2
DEC_SEQ = 1024
PAST_LEN = 256
GRID_W = 64
HEAD_DIM = 64
N_HEADS = 16
N_KV_HEADS = 4
ROPE_THETA = 10000.0
AXIS_DIM = HEAD_DIM // 2
HALF = HEAD_DIM // 2
SSM_GROUP = 16
SSM_WIDTH = 512
N_GROUPS = SSM_WIDTH // SSM_GROUP
STATE_N = 64
N_DIRS = 2
D_FF = 2816
N_EXPERTS = 8
D_FF_EXPERT = 3584
EPS = 1e-6
Q_W = N_HEADS * HEAD_DIM
KV_W = N_KV_HEADS * HEAD_DIM
IN_W = SSM_WIDTH + Q_W + 2 * KV_W + 2 * D_MODEL

CTX_TOKENS = BATCH * SEQ
DEC_TOKENS = DEC_BATCH * DEC_SEQ
ALL_TOKENS = CTX_TOKENS + DEC_TOKENS

LANES = 128
SUBLANES = 8
VMEM_LIMIT_BYTES = 56 * 1024 * 1024

ROW_TILE = 512
ATTN_Q_TILE = 256
S5_GROUP_SET = 8
S5_SET_CH = S5_GROUP_SET * SSM_GROUP
S5_SET_ST = S5_GROUP_SET * STATE_N
S5_N_SETS = N_GROUPS // S5_GROUP_SET
S5_TIME_CHUNK = 256
N_MOD_ROWS = 1 + DEC_BATCH


def _params(sem):
    return pltpu.CompilerParams(dimension_semantics=sem, vmem_limit_bytes=VMEM_LIMIT_BYTES)


def _resident(shape, index_map):
    return pl.BlockSpec(shape, index_map, pipeline_mode=pl.Buffered(1))


def _sigmoid(x):
    return 1.0 / (1.0 + jnp.exp(-x))


def _mod_row(i, tm):
    nctx = CTX_TOKENS // tm
    per_b = DEC_SEQ // tm
    return jnp.where(i < nctx, 0, 1 + (i - nctx) // per_b)


def _rope_block(i, tm):
    nctx = CTX_TOKENS // tm
    per_b = DEC_SEQ // tm
    return jnp.where(i < nctx, 0, 1 + (i - nctx) % per_b)


def _rms_modulate(x, g, scale, shift):
    ms = jnp.mean(x * x, axis=-1, keepdims=True)
    return (x * lax.rsqrt(ms + EPS) * g) * (1.0 + scale) + shift


def _ada_kernel(c_ref, w_ref, b_ref, o_ref):
    c = c_ref[...]
    act = c * _sigmoid(c)
    o_ref[0] = jnp.dot(act.astype(BF16), w_ref[0].astype(BF16), preferred_element_type=F32) + b_ref[0]


def _ada_modulation(c_rows, w_ada, b_ada):
    tn = 1536
    return pl.pallas_call(
        _ada_kernel,
        grid=(DEPTH, 6 * D_MODEL // tn),
        in_specs=[
            pl.BlockSpec((SUBLANES, D_MODEL), lambda l, n: (0, 0)),
            pl.BlockSpec((1, D_MODEL, tn), lambda l, n: (l, 0, n)),
            pl.BlockSpec((1, 1, tn), lambda l, n: (l, 0, n)),
        ],
        out_specs=pl.BlockSpec((1, SUBLANES, tn), lambda l, n: (l, 0, n)),
        out_shape=jax.ShapeDtypeStruct((DEPTH, SUBLANES, 6 * D_MODEL), F32),
        compiler_params=_params(("parallel", "parallel")),
        name="ada_modulation",
    )(c_rows, w_ada, b_ada.reshape(DEPTH, 1, 6 * D_MODEL))


def _rotate_pairs(x, cos, sin_signed):
    n = x.shape[-1]
    lane = lax.broadcasted_iota(jnp.int32, x.shape, 1)
    first_half = (lane % HEAD_DIM) < HALF
    partner = jnp.where(first_half, pltpu.roll(x, n - HALF, 1), pltpu.roll(x, HALF, 1))
    reps = n // LANES
    return x * jnp.tile(cos, (1, reps)) + partner * jnp.tile(sin_signed, (1, reps))


def _inproj_kernel(x_ref, mod_ref, g_ref, w_ref, ones_ref, qg_ref, kg_ref, cos_ref, sin_ref,
                   u_ref, q_ref, k_ref, kraw_ref, v_ref, sgs_ref, sga_ref):
    mod = mod_ref[0]
    h = _rms_modulate(x_ref[...], g_ref[...], mod[1:2], mod[0:1]).astype(BF16)

    def proj(lo, width):
        return jnp.dot(h, w_ref[:, lo:lo + width], preferred_element_type=F32)

    def head_norm(t, ones, gain):
        ss = jnp.dot((t * t).astype(BF16), ones, preferred_element_type=F32)
        return t * lax.rsqrt(ss * (1.0 / HEAD_DIM) + EPS) * gain

    cos = cos_ref[...]
    sin = sin_ref[...]
    o = 0
    u_ref[...] = proj(o, SSM_WIDTH)
    o += SSM_WIDTH
    q = head_norm(proj(o, Q_W), ones_ref[...], qg_ref[...])
    q_ref[...] = (_rotate_pairs(q, cos, sin) * (1.0 / math.sqrt(HEAD_DIM))).astype(BF16)
    o += Q_W
    k = head_norm(proj(o, KV_W), ones_ref[:KV_W, :KV_W], kg_ref[...])
    kraw_ref[...] = k
    k_ref[...] = _rotate_pairs(k, cos, sin)
    o += KV_W
    v_ref[...] = proj(o, KV_W)
    o += KV_W
    sgs_ref[...] = _sigmoid(proj(o, D_MODEL))
    o += D_MODEL
    sga_ref[...] = _sigmoid(proj(o, D_MODEL))


def _in_projection(x_all, mod, norm_g, w_in_bf16, ones_bd, q_gain, k_gain, rope_cos, rope_sin):
    tm = ROW_TILE
    n_tiles = ALL_TOKENS // tm
    row = lambda w: pl.BlockSpec((tm, w), lambda i: (i, 0))
    shapes = [(SSM_WIDTH, F32), (Q_W, BF16), (KV_W, F32), (KV_W, F32), (KV_W, F32), (D_MODEL, F32), (D_MODEL, F32)]
    return pl.pallas_call(
        _inproj_kernel,
        grid=(n_tiles,),
        in_specs=[
            row(D_MODEL),
            pl.BlockSpec((1, 6, D_MODEL), lambda i: (_mod_row(i, tm), 0, 0)),
            _resident((1, D_MODEL), lambda i: (0, 0)),
            _resident((D_MODEL, IN_W), lambda i: (0, 0)),
            _resident((Q_W, Q_W), lambda i: (0, 0)),
            _resident((1, Q_W), lambda i: (0, 0)),
            _resident((1, KV_W), lambda i: (0, 0)),
            pl.BlockSpec((tm, LANES), lambda i: (_rope_block(i, tm), 0)),
            pl.BlockSpec((tm, LANES), lambda i: (_rope_block(i, tm), 0)),
        ],
        out_specs=[row(w) for w, _ in shapes],
        out_shape=[jax.ShapeDtypeStruct((ALL_TOKENS, w), dt) for w, dt in shapes],
        compiler_params=_params(("parallel",)),
        name="in_projection",
    )(x_all, mod, norm_g, w_in_bf16, ones_bd, q_gain, k_gain, rope_cos, rope_sin)


def _attn_kernel(q_ref, k_ref, v_ref, o_ref):
    lane = lax.broadcasted_iota(jnp.int32, (1, LANES), 1)
    low = lane < HEAD_DIM
    groups_per_slab = LANES // HEAD_DIM
    rep = N_HEADS // N_KV_HEADS
    for g in range(N_KV_HEADS):
        slab = g // groups_per_slab
        own_low = (g % groups_per_slab) == 0
        keep = low if own_low else jnp.logical_not(low)

        def halves(ref):
            same = jnp.where(keep, ref[0, :, slab * LANES:(slab + 1) * LANES], 0.0)
            swapped = pltpu.roll(same, HEAD_DIM, 1)
            lo, hi = (same, swapped) if own_low else (swapped, same)
            return lo.astype(BF16), hi.astype(BF16)

        k_lo, k_hi = halves(k_ref)
        v_lo, v_hi = halves(v_ref)
        for j in range(g * rep // groups_per_slab, (g + 1) * rep // groups_per_slab):
            qs = q_ref[0, :, j * LANES:(j + 1) * LANES]
            acc = None
            for kk, vv in ((k_lo, v_lo), (k_hi, v_hi)):
                s = lax.dot_general(qs, kk, (((1,), (1,)), ((), ())), preferred_element_type=F32)
                p = jnp.exp(s - jnp.max(s, axis=-1, keepdims=True))
                denom = jnp.sum(p, axis=-1, keepdims=True)
                part = jnp.dot(p.astype(BF16), vv, preferred_element_type=F32) / denom
                acc = part if acc is None else acc + part
            o_ref[0, :, j * LANES:(j + 1) * LANES] = acc.astype(BF16)


def _attention(q_tiles, k, v, n_batch, q_tiles_per_batch, first_tile, name):
    tq = ATTN_Q_TILE
    lk = k.shape[1]
    kv_spec = pl.BlockSpec((1, lk, KV_W), lambda b, i: (b, 0, 0))
    return pl.pallas_call(
        _attn_kernel,
        grid=(n_batch, q_tiles_per_batch),
        in_specs=[
            pl.BlockSpec((1, tq, Q_W), lambda b, i: (first_tile + b * q_tiles_per_batch + i, 0, 0)),
            kv_spec,
            kv_spec,
        ],
        out_specs=pl.BlockSpec((1, tq, Q_W), lambda b, i: (b * q_tiles_per_batch + i, 0, 0)),
        out_shape=jax.ShapeDtypeStruct((n_batch * q_tiles_per_batch, tq, Q_W), BF16),
        compiler_params=_params(("parallel", "parallel")),
        name=name,
    )(q_tiles, k, v)


def _s5_kernel(u_ref, bd_ref, a_ref, cd_ref, h0_ref, dskip_ref, y_ref, hfin_ref, bu_scr, hs_scr, *, seq_len):
    tc = S5_TIME_CHUNK
    n_chunks = seq_len // tc
    direction = pl.program_id(2)
    forward = direction == 0
    a_re = jnp.broadcast_to(a_ref[0, 0, :, :S5_SET_ST], (SUBLANES, S5_SET_ST))
    a_im = jnp.broadcast_to(a_ref[0, 0, :, S5_SET_ST:], (SUBLANES, S5_SET_ST))

    @pl.when(forward)
    def _():
        y_ref[...] = u_ref[...] * dskip_ref[...]

    def chunk(ci, carry):
        c = jnp.where(forward, ci, n_chunks - 1 - ci)
        t0 = pl.multiple_of(c * tc, tc)
        u2 = u_ref[pl.ds(t0, tc)].reshape(tc * SUBLANES, S5_SET_CH).astype(BF16)
        bu_scr[...] = jnp.dot(u2, bd_ref[0, 0], preferred_element_type=F32)

        def step(t, hc):
            h_re, h_im = hc
            tt = jnp.where(forward, t, tc - 1 - t)
            r = pl.multiple_of(tt * SUBLANES, SUBLANES)
            n_re = a_re * h_re - a_im * h_im + bu_scr[pl.ds(r, SUBLANES), :S5_SET_ST]
            n_im = a_re * h_im + a_im * h_re + bu_scr[pl.ds(r, SUBLANES), S5_SET_ST:]
            hs_scr[pl.ds(r, SUBLANES), :S5_SET_ST] = n_re
            hs_scr[pl.ds(r, SUBLANES), S5_SET_ST:] = n_im
            return n_re, n_im

        carry = lax.fori_loop(0, tc, step, carry)
        yc = jnp.dot(hs_scr[...].astype(BF16), cd_ref[0, 0], preferred_element_type=F32)
        y_ref[pl.ds(t0, tc)] += yc.reshape(tc, SUBLANES, S5_SET_CH)
        return carry

    h0 = h0_ref[0, 0]
    h_re, h_im = lax.fori_loop(0, n_chunks, chunk, (h0[:, :S5_SET_ST], h0[:, S5_SET_ST:]))
    hfin_ref[0, 0, :, :S5_SET_ST] = h_re
    hfin_ref[0, 0, :, S5_SET_ST:] = h_im


def _s5_branch(u_tm, bd, a_bar, cd, h0, d_skip, name):
    seq_len, nb, _ = u_tm.shape
    st2 = 2 * S5_SET_ST
    par = lambda shape: pl.BlockSpec(shape, lambda b, s, d: (d, s, 0, 0))
    state = pl.BlockSpec((1, 1, SUBLANES, st2), lambda b, s, d: (d, s, b, 0))
    seq = pl.BlockSpec((seq_len, SUBLANES, S5_SET_CH), lambda b, s, d: (0, b, s))
    rows = S5_TIME_CHUNK * SUBLANES
    return pl.pallas_call(
        functools.partial(_s5_kernel, seq_len=seq_len),
        grid=(nb // SUBLANES, S5_N_SETS, N_DIRS),
        in_specs=[
            seq,
            par((1, 1, S5_SET_CH, st2)),
            par((1, 1, 1, st2)),
            par((1, 1, st2, S5_SET_CH)),
            state,
            pl.BlockSpec((1, S5_SET_CH), lambda b, s, d: (0, s)),
        ],
        out_specs=[seq, state],
        out_shape=[jax.ShapeDtypeStruct((seq_len, nb, SSM_WIDTH), F32),
                   jax.ShapeDtypeStruct((N_DIRS, S5_N_SETS, nb, st2), F32)],
        scratch_shapes=[pltpu.VMEM((rows, st2), F32), pltpu.VMEM((rows, st2), F32)],
        compiler_params=_params(("parallel", "parallel", "arbitrary")),
        name=name,
    )(u_tm, bd, a_bar, cd, h0, d_skip)


def _s5_operators(l, P):
    a_c = lax.complex(P['ssm_a_re'][l].astype(F32), P['ssm_a_im'][l].astype(F32))
    dt = jnp.exp(P['ssm_log_dt'][l].astype(F32))[..., None]
    a_bar = jnp.exp(a_c * dt)
    b_c = lax.complex(P['ssm_b_re'][l].astype(F32), P['ssm_b_im'][l].astype(F32))
    b_bar = ((a_bar - 1.0) / a_c)[..., None] * b_c
    eye = jnp.eye(S5_GROUP_SET, dtype=F32)

    def sets(t):
        return t.reshape(N_DIRS, S5_N_SETS, S5_GROUP_SET, *t.shape[2:])

    def b_operator(part):
        return jnp.einsum('dsgnp,gh->dsgphn', part, eye).reshape(N_DIRS, S5_N_SETS, S5_SET_CH, S5_SET_ST)

    def c_operator(part):
        return jnp.einsum('dsgpn,gh->dshngp', part, eye).reshape(N_DIRS, S5_N_SETS, S5_SET_ST, S5_SET_CH)

    bb = sets(b_bar)
    bd = jnp.concatenate([b_operator(jnp.real(bb)), b_operator(jnp.imag(bb))], axis=-1).astype(BF16)
    c_re = sets(P['ssm_c_re'][l].astype(F32))
    c_im = sets(P['ssm_c_im'][l].astype(F32))
    cd = jnp.concatenate([c_operator(c_re), -c_operator(c_im)], axis=-2).astype(BF16)
    ab = sets(a_bar).reshape(N_DIRS, S5_N_SETS, 1, S5_SET_ST)
    a_rows = jnp.concatenate([jnp.real(ab), jnp.imag(ab)], axis=-1)
    return bd, a_rows, cd


def _mixer_out_kernel(attn_ref, y_ref, sgs_ref, sga_ref, x_ref, mod_ref, wab_ref, wglu_ref, wout_ref, o_ref):
    attn_br = jnp.dot(attn_ref[...], wab_ref[...], preferred_element_type=F32)
    y = y_ref[...]
    gelu = 0.5 * y * (1.0 + jnp.tanh(math.sqrt(2.0 / math.pi) * (y + 0.044715 * (y * y * y))))
    glu = jnp.dot(gelu.astype(BF16), wglu_ref[...], preferred_element_type=F32)
    ssm_br = glu[:, :D_MODEL] * _sigmoid(glu[:, D_MODEL:])
    merged = sgs_ref[...] * ssm_br + sga_ref[...] * attn_br
    out = jnp.dot(merged.astype(BF16), wout_ref[...], preferred_element_type=F32)
    o_ref[...] = x_ref[...] + mod_ref[0][2:3] * out


def _mixer_out(attn, y, sgs, sga, x_all, mod, w_attn_br, w_glu, w_out):
    tm = ROW_TILE
    row = lambda w: pl.BlockSpec((tm, w), lambda i: (i, 0))
    return pl.pallas_call(
        _mixer_out_kernel,
        grid=(ALL_TOKENS // tm,),
        in_specs=[
            row(Q_W), row(SSM_WIDTH), row(D_MODEL), row(D_MODEL), row(D_MODEL),
            pl.BlockSpec((1, 6, D_MODEL), lambda i: (_mod_row(i, tm), 0, 0)),
            _resident((Q_W, D_MODEL), lambda i: (0, 0)),
            _resident((SSM_WIDTH, 2 * D_MODEL), lambda i: (0, 0)),
            _resident((D_MODEL, D_MODEL), lambda i: (0, 0)),
        ],
        out_specs=row(D_MODEL),
        out_shape=jax.ShapeDtypeStruct((ALL_TOKENS, D_MODEL), F32),
        compiler_params=_params(("parallel",)),
        name="mixer_out",
    )(attn, y, sgs, sga, x_all, mod, w_attn_br, w_glu, w_out)


def _prenorm_kernel(x_ref, mod_ref, g_ref, h_ref):
    mod = mod_ref[0]
    h_ref[...] = _rms_modulate(x_ref[...], g_ref[...], mod[4:5], mod[3:4]).astype(BF16)


def _router_kernel(x_ref, mod_ref, g_ref, r_ref, h_ref, gates_ref):
    mod = mod_ref[0]
    h = _rms_modulate(x_ref[...], g_ref[...], mod[4:5], mod[3:4])
    h_ref[...] = h.astype(BF16)
    logits = jnp.dot(h, r_ref[...], preferred_element_type=F32, precision=lax.Precision.HIGHEST)
    lane = lax.broadcasted_iota(jnp.int32, logits.shape, 1).astype(F32)
    logits = jnp.where(lane < N_EXPERTS, logits, -jnp.inf)
    top1 = jnp.max(logits, axis=-1, keepdims=True)
    idx1 = jnp.min(jnp.where(logits == top1, lane, float(LANES)), axis=-1, keepdims=True)
    rest = jnp.where(lane == idx1, -jnp.inf, logits)
    top2 = jnp.max(rest, axis=-1, keepdims=True)
    idx2 = jnp.min(jnp.where(rest == top2, lane, float(LANES)), axis=-1, keepdims=True)
    e2 = jnp.exp(top2 - top1)
    w1 = 1.0 / (1.0 + e2)
    w2 = e2 / (1.0 + e2)
    gates_ref[...] = jnp.where(lane == idx1, w1, 0.0) + jnp.where(lane == idx2, w2, 0.0)


def _prenorm(x_all, mod, norm_g, router_pad=None):
    tm = ROW_TILE
    row = lambda w: pl.BlockSpec((tm, w), lambda i: (i, 0))
    in_specs = [row(D_MODEL),
                pl.BlockSpec((1, 6, D_MODEL), lambda i: (_mod_row(i, tm), 0, 0)),
                _resident((1, D_MODEL), lambda i: (0, 0))]
    if router_pad is None:
        return pl.pallas_call(
            _prenorm_kernel, grid=(ALL_TOKENS // tm,), in_specs=in_specs, out_specs=row(D_MODEL),
            out_shape=jax.ShapeDtypeStruct((ALL_TOKENS, D_MODEL), BF16),
            compiler_params=_params(("parallel",)), name="prenorm",
        )(x_all, mod, norm_g)
    return pl.pallas_call(
        _router_kernel, grid=(ALL_TOKENS // tm,),
        in_specs=in_specs + [_resident((D_MODEL, LANES), lambda i: (0, 0))],
        out_specs=[row(D_MODEL), row(LANES)],
        out_shape=[jax.ShapeDtypeStruct((ALL_TOKENS, D_MODEL), BF16),
                   jax.ShapeDtypeStruct((ALL_TOKENS, LANES), F32)],
        compiler_params=_params(("parallel",)), name="prenorm_router",
    )(x_all, mod, norm_g, router_pad)


def _ffn_kernel(tile_ref, expert_ref, first_ref, h_ref, gates_ref, wg_ref, wu_ref, wd_ref, o_ref, *, gated):
    s = pl.program_id(0)
    j = pl.program_id(1)
    h = h_ref[...]
    a = jnp.dot(h, wg_ref[0], preferred_element_type=F32)
    b = jnp.dot(h, wu_ref[0], preferred_element_type=F32)
    inter = (a * _sigmoid(a)) * b
    if gated:
        lane = lax.broadcasted_iota(jnp.int32, gates_ref.shape, 1)
        gate = jnp.sum(jnp.where(lane == expert_ref[s], gates_ref[...], 0.0), axis=-1, keepdims=True)
        inter = inter * gate
    contrib = jnp.dot(inter.astype(BF16), wd_ref[0], preferred_element_type=F32)
    start = jnp.logical_and(j == 0, first_ref[s] == 1)

    @pl.when(start)
    def _():
        o_ref[...] = contrib

    @pl.when(jnp.logical_not(start))
    def _():
        o_ref[...] += contrib


def _expert_ffn(h, gates, w_gate, w_up, w_down, step_tile, step_expert, step_first, tm, tf, gated, name):
    n_rows = h.shape[0]
    d_ff = w_gate.shape[-1]
    n_steps = step_tile.shape[0]
    grid_spec = pltpu.PrefetchScalarGridSpec(
        num_scalar_prefetch=3,
        grid=(n_steps, d_ff // tf),
        in_specs=[
            pl.BlockSpec((tm, D_MODEL), lambda s, j, t, e, f: (t[s], 0)),
            pl.BlockSpec((tm, LANES), lambda s, j, t, e, f: (t[s], 0)),
            pl.BlockSpec((1, D_MODEL, tf), lambda s, j, t, e, f: (e[s], 0, j)),
            pl.BlockSpec((1, D_MODEL, tf), lambda s, j, t, e, f: (e[s], 0, j)),
            pl.BlockSpec((1, tf, D_MODEL), lambda s, j, t, e, f: (e[s], j, 0)),
        ],
        out_specs=pl.BlockSpec((tm, D_MODEL), lambda s, j, t, e, f: (t[s], 0)),
    )
    return pl.pallas_call(
        functools.partial(_ffn_kernel, gated=gated),
        grid_spec=grid_spec,
        out_shape=jax.ShapeDtypeStruct((n_rows, D_MODEL), F32),
        compiler_params=_params(("arbitrary", "arbitrary")),
        name=name,
    )(step_tile, step_expert, step_first, h, gates, w_gate, w_up, w_down)


def _residual_kernel(x_ref, f_ref, mod_ref, o_ref):
    o_ref[...] = x_ref[...] + mod_ref[0][5:6] * f_ref[...]


def _residual_norm_kernel(x_ref, f_ref, mod_ref, g_ref, o_ref):
    x = x_ref[...] + mod_ref[0][5:6] * f_ref[...]
    ms = jnp.mean(x * x, axis=-1, keepdims=True)
    o_ref[...] = x * lax.rsqrt(ms + EPS) * g_ref[...]


def _residual(x_all, f, mod, final_g=None):
    tm = ROW_TILE
    row = pl.BlockSpec((tm, D_MODEL), lambda i: (i, 0))
    in_specs = [row, row, pl.BlockSpec((1, 6, D_MODEL), lambda i: (_mod_row(i, tm), 0, 0))]
    args = [x_all, f, mod]
    body = _residual_kernel
    if final_g is not None:
        in_specs.append(_resident((1, D_MODEL), lambda i: (0, 0)))
        args.append(final_g)
        body = _residual_norm_kernel
    return pl.pallas_call(
        body, grid=(ALL_TOKENS // tm,), in_specs=in_specs, out_specs=row,
        out_shape=jax.ShapeDtypeStruct((ALL_TOKENS, D_MODEL), F32),
        compiler_params=_params(("parallel",)),
        name="residual" if final_g is None else "residual_final_norm",
    )(*args)


def _rope_tables():
    rows = DEC_SEQ // GRID_W
    row = jnp.repeat(jnp.arange(rows, dtype=F32), GRID_W)
    col = jnp.tile(jnp.arange(GRID_W, dtype=F32), rows)
    inv = ROPE_THETA ** (-jnp.arange(0, AXIS_DIM, 2, dtype=F32) / AXIS_DIM)
    ang = jnp.concatenate([row[:, None] * inv, col[:, None] * inv], axis=-1)
    cos, sin = jnp.cos(ang), jnp.sin(ang)
    reps = LANES // HEAD_DIM
    cos_l = jnp.tile(jnp.concatenate([cos, cos], axis=-1), (1, reps))
    sin_l = jnp.tile(jnp.concatenate([-sin, sin], axis=-1), (1, reps))
    ident = (jnp.ones((ROW_TILE, LANES), F32), jnp.zeros((ROW_TILE, LANES), F32))
    return jnp.concatenate([ident[0], cos_l], axis=0), jnp.concatenate([ident[1], sin_l], axis=0)


def _dense_steps(n_tiles, n_experts):
    tile = jnp.repeat(jnp.arange(n_tiles, dtype=jnp.int32), n_experts)
    expert = jnp.tile(jnp.arange(n_experts, dtype=jnp.int32), n_tiles)
    first = (expert == 0).astype(jnp.int32)
    return tile, expert, first


def kernel(x_prompt, x_sample, cache_k, cache_v, state_ssm, c, c_ctx, w_ada, b_ada, norm1_g, norm2_g, w_in, q_norm_g, k_norm_g, ssm_a_re, ssm_a_im, ssm_log_dt, ssm_b_re, ssm_b_im, ssm_c_re, ssm_c_im, ssm_d, w_glu, w_attn_br, w_out, ffn_w_gate, ffn_w_up, ffn_w_down, moe_router, moe_w_gate, moe_w_up, moe_w_down, final_norm_g):
    P = dict(ssm_a_re=ssm_a_re, ssm_a_im=ssm_a_im, ssm_log_dt=ssm_log_dt, ssm_b_re=ssm_b_re,
             ssm_b_im=ssm_b_im, ssm_c_re=ssm_c_re, ssm_c_im=ssm_c_im)

    c_rows = jnp.zeros((SUBLANES, D_MODEL), F32).at[0].set(c_ctx).at[1:1 + DEC_BATCH].set(c)
    mod_all = _ada_modulation(c_rows, w_ada, b_ada)
    mod_all = mod_all[:, :N_MOD_ROWS].reshape(DEPTH, N_MOD_ROWS, 6, D_MODEL)

    rope_cos, rope_sin = _rope_tables()
    head_id = jnp.arange(Q_W, dtype=jnp.int32) // HEAD_DIM
    ones_bd = (head_id[:, None] == head_id[None, :]).astype(BF16)

    x_all = jnp.concatenate([x_prompt.reshape(CTX_TOKENS, D_MODEL), x_sample.reshape(DEC_TOKENS, D_MODEL)], axis=0)
    ctx_q_tiles = SEQ // ATTN_Q_TILE
    dec_q_tiles = DEC_SEQ // ATTN_Q_TILE
    nb_pad = SUBLANES
    ks, vs, ss = [], [], []

    for l in range(DEPTH):
        mod = mod_all[l]
        u, q, k, k_raw, v, sgs, sga = _in_projection(
            x_all, mod, norm1_g[l][None, :], w_in[l].astype(BF16), ones_bd,
            jnp.tile(q_norm_g[l], N_HEADS)[None, :], jnp.tile(k_norm_g[l], N_KV_HEADS)[None, :],
            rope_cos, rope_sin)

        q_tiles = q.reshape(ALL_TOKENS // ATTN_Q_TILE, ATTN_Q_TILE, Q_W)
        k_ctx = k_raw[:CTX_TOKENS].reshape(BATCH, SEQ, KV_W)
        v_ctx = v[:CTX_TOKENS].reshape(BATCH, SEQ, KV_W)
        attn_ctx = _attention(q_tiles, k_ctx, v_ctx, BATCH, ctx_q_tiles, 0, "attention_context")
        k_dec = jnp.concatenate([cache_k[:, l].reshape(DEC_BATCH, PAST_LEN, KV_W),
                                 k[CTX_TOKENS:].reshape(DEC_BATCH, DEC_SEQ, KV_W)], axis=1)
        v_dec = jnp.concatenate([cache_v[:, l].reshape(DEC_BATCH, PAST_LEN, KV_W),
                                 v[CTX_TOKENS:].reshape(DEC_BATCH, DEC_SEQ, KV_W)], axis=1)
        attn_dec = _attention(q_tiles, k_dec, v_dec, DEC_BATCH, dec_q_tiles, CTX_TOKENS // ATTN_Q_TILE,
                              "attention_latent")
        attn = jnp.concatenate([attn_ctx.reshape(CTX_TOKENS, Q_W), attn_dec.reshape(DEC_TOKENS, Q_W)], axis=0)
        ks.append(k_ctx.reshape(BATCH, SEQ, N_KV_HEADS, HEAD_DIM))
        vs.append(v_ctx.reshape(BATCH, SEQ, N_KV_HEADS, HEAD_DIM))

        bd, a_rows, cd = _s5_operators(l, P)
        d_skip = ssm_d[l][None, :]
        u_ctx = u[:CTX_TOKENS].reshape(BATCH, SEQ, SSM_WIDTH).transpose(1, 0, 2)
        h0_ctx = jnp.zeros((N_DIRS, S5_N_SETS, BATCH, 2 * S5_SET_ST), F32)
        y_ctx, fin = _s5_branch(u_ctx, bd, a_rows, cd, h0_ctx, d_skip, "s5_context")
        u_dec = u[CTX_TOKENS:].reshape(DEC_BATCH, DEC_SEQ, SSM_WIDTH).transpose(1, 0, 2)
        u_dec = jnp.pad(u_dec, ((0, 0), (0, nb_pad - DEC_BATCH), (0, 0)))
        h0_dec = state_ssm[:, l].reshape(DEC_BATCH, N_DIRS, S5_N_SETS, S5_GROUP_SET * STATE_N, 2)
        h0_dec = h0_dec.transpose(1, 2, 0, 4, 3).reshape(N_DIRS, S5_N_SETS, DEC_BATCH, 2 * S5_SET_ST)
        h0_dec = jnp.pad(h0_dec, ((0, 0), (0, 0), (0, nb_pad - DEC_BATCH), (0, 0)))
        y_dec, _ = _s5_branch(u_dec, bd, a_rows, cd, h0_dec, d_skip, "s5_latent")
        y = jnp.concatenate([y_ctx.transpose(1, 0, 2).reshape(CTX_TOKENS, SSM_WIDTH),
                             y_dec[:, :DEC_BATCH].transpose(1, 0, 2).reshape(DEC_TOKENS, SSM_WIDTH)], axis=0)
        fin = fin.reshape(N_DIRS, S5_N_SETS, BATCH, 2, S5_GROUP_SET, STATE_N)
        ss.append(fin.transpose(2, 0, 1, 4, 5, 3).reshape(BATCH, N_DIRS, N_GROUPS, STATE_N, 2))

        x_all = _mixer_out(attn, y, sgs, sga, x_all, mod, w_attn_br[l].astype(BF16),
                           w_glu[l].astype(BF16), w_out[l].astype(BF16))

        i = l // 2
        if l % 2 == 0:
            h2 = _prenorm(x_all, mod, norm2_g[l][None, :])
            tm = 1024
            n_tiles = ALL_TOKENS // tm
            tile, expert, first = _dense_steps(n_tiles, 1)
            f = _expert_ffn(h2, jnp.zeros((ALL_TOKENS, LANES), F32), ffn_w_gate[i][None].astype(BF16),
                            ffn_w_up[i][None].astype(BF16), ffn_w_down[i][None].astype(BF16),
                            tile, expert, first, tm, 256, False, "ffn_dense")
        else:
            router_pad = jnp.pad(moe_router[i].astype(F32), ((0, 0), (0, LANES - N_EXPERTS)))
            h2, gates = _prenorm(x_all, mod, norm2_g[l][None, :], router_pad)
            tm = 1024
            n_tiles = ALL_TOKENS // tm
            tile, expert, first = _dense_steps(n_tiles, N_EXPERTS)
            f = _expert_ffn(h2, gates, moe_w_gate[i].astype(BF16), moe_w_up[i].astype(BF16),
                            moe_w_down[i].astype(BF16), tile, expert, first, tm, 512, True, "ffn_experts")
        x_all = _residual(x_all, f, mod, final_norm_g[None, :] if l == DEPTH - 1 else None)

    y_prompt = x_all[:CTX_TOKENS].reshape(BATCH, SEQ, D_MODEL)
    y_sample = x_all[CTX_TOKENS:].reshape(DEC_BATCH, DEC_SEQ, D_MODEL)
    return (y_prompt, y_sample, jnp.stack(ks, axis=1), jnp.stack(vs, axis=1), jnp.stack(ss, axis=1))
```

```python
import functools
import math

import jax
import jax.numpy as jnp
from jax import lax
from jax.experimental import pallas as pl
from jax.experimental.pallas import tpu as pltpu

F32 = jnp.float32
BF16 = jnp.bfloat16

D_MODEL = 1024
BATCH = 32
SEQ = 256
DEPTH = 2
DEC_BATCH = 2
DEC_SEQ = 1024
PAST_LEN = 256
GRID_W = 64
HEAD_DIM = 64
N_HEADS = 16
N_KV_HEADS = 4
ROPE_THETA = 10000.0
AXIS_DIM = HEAD_DIM // 2
HALF = HEAD_DIM // 2
SSM_GROUP = 16
SSM_WIDTH = 512
N_GROUPS = SSM_WIDTH // SSM_GROUP
STATE_N = 64
N_DIRS = 2
D_FF = 2816
N_EXPERTS = 8
D_FF_EXPERT = 3584
EPS = 1e-6
Q_W = N_HEADS * HEAD_DIM
KV_W = N_KV_HEADS * HEAD_DIM
IN_W = SSM_WIDTH + Q_W + 2 * KV_W + 2 * D_MODEL

CTX_TOKENS = BATCH * SEQ
DEC_TOKENS = DEC_BATCH * DEC_SEQ
ALL_TOKENS = CTX_TOKENS + DEC_TOKENS

LANES = 128
SUBLANES = 8
VMEM_LIMIT_BYTES = 56 * 1024 * 1024

ROW_TILE = 512
ATTN_Q_TILE = 256
S5_GROUP_SET = 8
S5_SET_CH = S5_GROUP_SET * SSM_GROUP
S5_SET_ST = S5_GROUP_SET * STATE_N
S5_N_SETS = N_GROUPS // S5_GROUP_SET
S5_TIME_CHUNK = 256
N_MOD_ROWS = 1 + DEC_BATCH


def _params(sem):
    return pltpu.CompilerParams(dimension_semantics=sem, vmem_limit_bytes=VMEM_LIMIT_BYTES)


def _resident(shape, index_map):
    return pl.BlockSpec(shape, index_map, pipeline_mode=pl.Buffered(1))


def _sigmoid(x):
    return 1.0 / (1.0 + jnp.exp(-x))


def _mod_row(i, tm):
    nctx = CTX_TOKENS // tm
    per_b = DEC_SEQ // tm
    return jnp.where(i < nctx, 0, 1 + (i - nctx) // per_b)


def _rope_block(i, tm):
    nctx = CTX_TOKENS // tm
    per_b = DEC_SEQ // tm
    return jnp.where(i < nctx, 0, 1 + (i - nctx) % per_b)


def _rms_modulate(x, g, scale, shift):
    ms = jnp.mean(x * x, axis=-1, keepdims=True)
    return (x * lax.rsqrt(ms + EPS) * g) * (1.0 + scale) + shift


def _ada_kernel(c_ref, w_ref, b_ref, o_ref):
    c = c_ref[...]
    act = c * _sigmoid(c)
    o_ref[0] = jnp.dot(act.astype(BF16), w_ref[0].astype(BF16), preferred_element_type=F32) + b_ref[0]


def _ada_modulation(c_rows, w_ada, b_ada):
    tn = 1536
    return pl.pallas_call(
        _ada_kernel,
        grid=(DEPTH, 6 * D_MODEL // tn),
        in_specs=[
            pl.BlockSpec((SUBLANES, D_MODEL), lambda l, n: (0, 0)),
            pl.BlockSpec((1, D_MODEL, tn), lambda l, n: (l, 0, n)),
            pl.BlockSpec((1, 1, tn), lambda l, n: (l, 0, n)),
        ],
        out_specs=pl.BlockSpec((1, SUBLANES, tn), lambda l, n: (l, 0, n)),
        out_shape=jax.ShapeDtypeStruct((DEPTH, SUBLANES, 6 * D_MODEL), F32),
        compiler_params=_params(("parallel", "parallel")),
        name="ada_modulation",
    )(c_rows, w_ada, b_ada.reshape(DEPTH, 1, 6 * D_MODEL))


def _rotate_pairs(x, cos, sin_signed):
    n = x.shape[-1]
    lane = lax.broadcasted_iota(jnp.int32, x.shape, 1)
    first_half = (lane % HEAD_DIM) < HALF
    partner = jnp.where(first_half, pltpu.roll(x, n - HALF, 1), pltpu.roll(x, HALF, 1))
    reps = n // LANES
    return x * jnp.tile(cos, (1, reps)) + partner * jnp.tile(sin_signed, (1, reps))


def _inproj_kernel(x_ref, mod_ref, g_ref, w_ref, ones_ref, qg_ref, kg_ref, cos_ref, sin_ref,
                   u_ref, q_ref, k_ref, kraw_ref, v_ref, sgs_ref, sga_ref):
    mod = mod_ref[0]
    h = _rms_modulate(x_ref[...], g_ref[...], mod[1:2], mod[0:1]).astype(BF16)

    def proj(lo, width):
        return jnp.dot(h, w_ref[:, lo:lo + width], preferred_element_type=F32)

    def head_norm(t, ones, gain):
        ss = jnp.dot((t * t).astype(BF16), ones, preferred_element_type=F32)
        return t * lax.rsqrt(ss * (1.0 / HEAD_DIM) + EPS) * gain

    cos = cos_ref[...]
    sin = sin_ref[...]
    o = 0
    u_ref[...] = proj(o, SSM_WIDTH)
    o += SSM_WIDTH
    q = head_norm(proj(o, Q_W), ones_ref[...], qg_ref[...])
    q_ref[...] = (_rotate_pairs(q, cos, sin) * (1.0 / math.sqrt(HEAD_DIM))).astype(BF16)
    o += Q_W
    k = head_norm(proj(o, KV_W), ones_ref[:KV_W, :KV_W], kg_ref[...])
    kraw_ref[...] = k
    k_ref[...] = _rotate_pairs(k, cos, sin)
    o += KV_W
    v_ref[...] = proj(o, KV_W)
    o += KV_W
    sgs_ref[...] = _sigmoid(proj(o, D_MODEL))
    o += D_MODEL
    sga_ref[...] = _sigmoid(proj(o, D_MODEL))


def _in_projection(x_all, mod, norm_g, w_in_bf16, ones_bd, q_gain, k_gain, rope_cos, rope_sin):
    tm = ROW_TILE
    n_tiles = ALL_TOKENS // tm
    row = lambda w: pl.BlockSpec((tm, w), lambda i: (i, 0))
    shapes = [(SSM_WIDTH, F32), (Q_W, BF16), (KV_W, F32), (KV_W, F32), (KV_W, F32), (D_MODEL, F32), (D_MODEL, F32)]
    return pl.pallas_call(
        _inproj_kernel,
        grid=(n_tiles,),
        in_specs=[
            row(D_MODEL),
            pl.BlockSpec((1, 6, D_MODEL), lambda i: (_mod_row(i, tm), 0, 0)),
            _resident((1, D_MODEL), lambda i: (0, 0)),
            _resident((D_MODEL, IN_W), lambda i: (0, 0)),
            _resident((Q_W, Q_W), lambda i: (0, 0)),
            _resident((1, Q_W), lambda i: (0, 0)),
            _resident((1, KV_W), lambda i: (0, 0)),
            pl.BlockSpec((tm, LANES), lambda i: (_rope_block(i, tm), 0)),
            pl.BlockSpec((tm, LANES), lambda i: (_rope_block(i, tm), 0)),
        ],
        out_specs=[row(w) for w, _ in shapes],
        out_shape=[jax.ShapeDtypeStruct((ALL_TOKENS, w), dt) for w, dt in shapes],
        compiler_params=_params(("parallel",)),
        name="in_projection",
    )(x_all, mod, norm_g, w_in_bf16, ones_bd, q_gain, k_gain, rope_cos, rope_sin)


def _attn_kernel(q_ref, k_ref, v_ref, o_ref):
    lane = lax.broadcasted_iota(jnp.int32, (1, LANES), 1)
    low = lane < HEAD_DIM
    groups_per_slab = LANES // HEAD_DIM
    rep = N_HEADS // N_KV_HEADS
    for g in range(N_KV_HEADS):
        slab = g // groups_per_slab
        own_low = (g % groups_per_slab) == 0
        keep = low if own_low else jnp.logical_not(low)

        def halves(ref):
            same = jnp.where(keep, ref[0, :, slab * LANES:(slab + 1) * LANES], 0.0)
            swapped = pltpu.roll(same, HEAD_DIM, 1)
            lo, hi = (same, swapped) if own_low else (swapped, same)
            return lo.astype(BF16), hi.astype(BF16)

        k_lo, k_hi = halves(k_ref)
        v_lo, v_hi = halves(v_ref)
        for j in range(g * rep // groups_per_slab, (g + 1) * rep // groups_per_slab):
            qs = q_ref[0, :, j * LANES:(j + 1) * LANES]
            acc = None
            for kk, vv in ((k_lo, v_lo), (k_hi, v_hi)):
                s = lax.dot_general(qs, kk, (((1,), (1,)), ((), ())), preferred_element_type=F32)
                p = jnp.exp(s - jnp.max(s, axis=-1, keepdims=True))
                denom = jnp.sum(p, axis=-1, keepdims=True)
                part = jnp.dot(p.astype(BF16), vv, preferred_element_type=F32) / denom
                acc = part if acc is None else acc + part
            o_ref[0, :, j * LANES:(j + 1) * LANES] = acc.astype(BF16)


def _attention(q_tiles, k, v, n_batch, q_tiles_per_batch, first_tile, name):
    tq = ATTN_Q_TILE
    lk = k.shape[1]
    kv_spec = pl.BlockSpec((1, lk, KV_W), lambda b, i: (b, 0, 0))
    return pl.pallas_call(
        _attn_kernel,
        grid=(n_batch, q_tiles_per_batch),
        in_specs=[
            pl.BlockSpec((1, tq, Q_W), lambda b, i: (first_tile + b * q_tiles_per_batch + i, 0, 0)),
            kv_spec,
            kv_spec,
        ],
        out_specs=pl.BlockSpec((1, tq, Q_W), lambda b, i: (b * q_tiles_per_batch + i, 0, 0)),
        out_shape=jax.ShapeDtypeStruct((n_batch * q_tiles_per_batch, tq, Q_W), BF16),
        compiler_params=_params(("parallel", "parallel")),
        name=name,
    )(q_tiles, k, v)


def _s5_kernel(u_ref, bd_ref, a_ref, cd_ref, h0_ref, dskip_ref, y_ref, hfin_ref, bu_scr, hs_scr, *, seq_len):
    tc = S5_TIME_CHUNK
    n_chunks = seq_len // tc
    direction = pl.program_id(2)
    forward = direction == 0
    a_re = jnp.broadcast_to(a_ref[0, 0, :, :S5_SET_ST], (SUBLANES, S5_SET_ST))
    a_im = jnp.broadcast_to(a_ref[0, 0, :, S5_SET_ST:], (SUBLANES, S5_SET_ST))

    @pl.when(forward)
    def _():
        y_ref[...] = u_ref[...] * dskip_ref[...]

    def chunk(ci, carry):
        c = jnp.where(forward, ci, n_chunks - 1 - ci)
        t0 = pl.multiple_of(c * tc, tc)
        u2 = u_ref[pl.ds(t0, tc)].reshape(tc * SUBLANES, S5_SET_CH).astype(BF16)
        bu_scr[...] = jnp.dot(u2, bd_ref[0, 0], preferred_element_type=F32)

        def step(t, hc):
            h_re, h_im = hc
            tt = jnp.where(forward, t, tc - 1 - t)
            r = pl.multiple_of(tt * SUBLANES, SUBLANES)
            n_re = a_re * h_re - a_im * h_im + bu_scr[pl.ds(r, SUBLANES), :S5_SET_ST]
            n_im = a_re * h_im + a_im * h_re + bu_scr[pl.ds(r, SUBLANES), S5_SET_ST:]
            hs_scr[pl.ds(r, SUBLANES), :S5_SET_ST] = n_re
            hs_scr[pl.ds(r, SUBLANES), S5_SET_ST:] = n_im
            return n_re, n_im

        carry = lax.fori_loop(0, tc, step, carry)
        yc = jnp.dot(hs_scr[...].astype(BF16), cd_ref[0, 0], preferred_element_type=F32)
        y_ref[pl.ds(t0, tc)] += yc.reshape(tc, SUBLANES, S5_SET_CH)
        return carry

    h0 = h0_ref[0, 0]
    h_re, h_im = lax.fori_loop(0, n_chunks, chunk, (h0[:, :S5_SET_ST], h0[:, S5_SET_ST:]))
    hfin_ref[0, 0, :, :S5_SET_ST] = h_re
    hfin_ref[0, 0, :, S5_SET_ST:] = h_im


def _s5_branch(u_tm, bd, a_bar, cd, h0, d_skip, name):
    seq_len, nb, _ = u_tm.shape
    st2 = 2 * S5_SET_ST
    par = lambda shape: pl.BlockSpec(shape, lambda b, s, d: (d, s, 0, 0))
    state = pl.BlockSpec((1, 1, SUBLANES, st2), lambda b, s, d: (d, s, b, 0))
    seq = pl.BlockSpec((seq_len, SUBLANES, S5_SET_CH), lambda b, s, d: (0, b, s))
    rows = S5_TIME_CHUNK * SUBLANES
    return pl.pallas_call(
        functools.partial(_s5_kernel, seq_len=seq_len),
        grid=(nb // SUBLANES, S5_N_SETS, N_DIRS),
        in_specs=[
            seq,
            par((1, 1, S5_SET_CH, st2)),
            par((1, 1, 1, st2)),
            par((1, 1, st2, S5_SET_CH)),
            state,
            pl.BlockSpec((1, S5_SET_CH), lambda b, s, d: (0, s)),
        ],
        out_specs=[seq, state],
        out_shape=[jax.ShapeDtypeStruct((seq_len, nb, SSM_WIDTH), F32),
                   jax.ShapeDtypeStruct((N_DIRS, S5_N_SETS, nb, st2), F32)],
        scratch_shapes=[pltpu.VMEM((rows, st2), F32), pltpu.VMEM((rows, st2), F32)],
        compiler_params=_params(("parallel", "parallel", "arbitrary")),
        name=name,
    )(u_tm, bd, a_bar, cd, h0, d_skip)


def _s5_operators(l, P):
    a_re, a_im = P['ssm_a_re'][l].astype(F32), P['ssm_a_im'][l].astype(F32)
    dt = jnp.exp(P['ssm_log_dt'][l].astype(F32))[..., None]
    mag = jnp.exp(a_re * dt)
    ab_re, ab_im = mag * jnp.cos(a_im * dt), mag * jnp.sin(a_im * dt)
    den = a_re * a_re + a_im * a_im
    k_re = ((ab_re - 1.0) * a_re + ab_im * a_im) / den
    k_im = (ab_im * a_re - (ab_re - 1.0) * a_im) / den
    b_re, b_im = P['ssm_b_re'][l].astype(F32), P['ssm_b_im'][l].astype(F32)
    bb_re = k_re[..., None] * b_re - k_im[..., None] * b_im
    bb_im = k_re[..., None] * b_im + k_im[..., None] * b_re
    eye = jnp.eye(S5_GROUP_SET, dtype=F32)

    def sets(t):
        return t.reshape(N_DIRS, S5_N_SETS, S5_GROUP_SET, *t.shape[2:])

    def b_operator(part):
        return jnp.einsum('dsgnp,gh->dsgphn', part, eye).reshape(N_DIRS, S5_N_SETS, S5_SET_CH, S5_SET_ST)

    def c_operator(part):
        return jnp.einsum('dsgpn,gh->dshngp', part, eye).reshape(N_DIRS, S5_N_SETS, S5_SET_ST, S5_SET_CH)

    bd = jnp.concatenate([b_operator(sets(bb_re)), b_operator(sets(bb_im))], axis=-1).astype(BF16)
    c_re = sets(P['ssm_c_re'][l].astype(F32))
    c_im = sets(P['ssm_c_im'][l].astype(F32))
    cd = jnp.concatenate([c_operator(c_re), -c_operator(c_im)], axis=-2).astype(BF16)
    row = lambda t: sets(t).reshape(N_DIRS, S5_N_SETS, 1, S5_SET_ST)
    a_rows = jnp.concatenate([row(ab_re), row(ab_im)], axis=-1)
    return bd, a_rows, cd


def _mixer_out_kernel(attn_ref, y_ref, sgs_ref, sga_ref, x_ref, mod_ref, wab_ref, wglu_ref, wout_ref, o_ref):
    attn_br = jnp.dot(attn_ref[...], wab_ref[...], preferred_element_type=F32)
    y = y_ref[...]
    gelu = 0.5 * y * (1.0 + jnp.tanh(math.sqrt(2.0 / math.pi) * (y + 0.044715 * (y * y * y))))
    glu = jnp.dot(gelu.astype(BF16), wglu_ref[...], preferred_element_type=F32)
    ssm_br = glu[:, :D_MODEL] * _sigmoid(glu[:, D_MODEL:])
    merged = sgs_ref[...] * ssm_br + sga_ref[...] * attn_br
    out = jnp.dot(merged.astype(BF16), wout_ref[...], preferred_element_type=F32)
    o_ref[...] = x_ref[...] + mod_ref[0][2:3] * out


def _mixer_out(attn, y, sgs, sga, x_all, mod, w_attn_br, w_glu, w_out):
    tm = ROW_TILE
    row = lambda w: pl.BlockSpec((tm, w), lambda i: (i, 0))
    return pl.pallas_call(
        _mixer_out_kernel,
        grid=(ALL_TOKENS // tm,),
        in_specs=[
            row(Q_W), row(SSM_WIDTH), row(D_MODEL), row(D_MODEL), row(D_MODEL),
            pl.BlockSpec((1, 6, D_MODEL), lambda i: (_mod_row(i, tm), 0, 0)),
            _resident((Q_W, D_MODEL), lambda i: (0, 0)),
            _resident((SSM_WIDTH, 2 * D_MODEL), lambda i: (0, 0)),
            _resident((D_MODEL, D_MODEL), lambda i: (0, 0)),
        ],
        out_specs=row(D_MODEL),
        out_shape=jax.ShapeDtypeStruct((ALL_TOKENS, D_MODEL), F32),
        compiler_params=_params(("parallel",)),
        name="mixer_out",
    )(attn, y, sgs, sga, x_all, mod, w_attn_br, w_glu, w_out)


def _prenorm_kernel(x_ref, mod_ref, g_ref, h_ref):
    mod = mod_ref[0]
    h_ref[...] = _rms_modulate(x_ref[...], g_ref[...], mod[4:5], mod[3:4]).astype(BF16)


ROUTE_IDX_LANE = 0
ROUTE_W_LANE = 2


def _router_kernel(x_ref, mod_ref, g_ref, r_ref, h_ref, route_ref):
    mod = mod_ref[0]
    h = _rms_modulate(x_ref[...], g_ref[...], mod[4:5], mod[3:4])
    h_ref[...] = h
    logits = jnp.dot(h, r_ref[...], preferred_element_type=F32, precision=lax.Precision.HIGHEST)
    lane = lax.broadcasted_iota(jnp.int32, logits.shape, 1).astype(F32)
    logits = jnp.where(lane < N_EXPERTS, logits, -jnp.inf)
    top1 = jnp.max(logits, axis=-1, keepdims=True)
    idx1 = jnp.min(jnp.where(logits == top1, lane, float(LANES)), axis=-1, keepdims=True)
    rest = jnp.where(lane == idx1, -jnp.inf, logits)
    top2 = jnp.max(rest, axis=-1, keepdims=True)
    idx2 = jnp.min(jnp.where(rest == top2, lane, float(LANES)), axis=-1, keepdims=True)
    e2 = jnp.exp(top2 - top1)
    w1 = 1.0 / (1.0 + e2)
    w2 = e2 / (1.0 + e2)
    route = jnp.where(lane == ROUTE_IDX_LANE, idx1, 0.0)
    route = jnp.where(lane == ROUTE_IDX_LANE + 1, idx2, route)
    route = jnp.where(lane == ROUTE_W_LANE, w1, route)
    route_ref[...] = jnp.where(lane == ROUTE_W_LANE + 1, w2, route)


def _prenorm(x_all, mod, norm_g, router_pad=None):
    tm = ROW_TILE
    row = lambda w: pl.BlockSpec((tm, w), lambda i: (i, 0))
    in_specs = [row(D_MODEL),
                pl.BlockSpec((1, 6, D_MODEL), lambda i: (_mod_row(i, tm), 0, 0)),
                _resident((1, D_MODEL), lambda i: (0, 0))]
    if router_pad is None:
        return pl.pallas_call(
            _prenorm_kernel, grid=(ALL_TOKENS // tm,), in_specs=in_specs, out_specs=row(D_MODEL),
            out_shape=jax.ShapeDtypeStruct((ALL_TOKENS, D_MODEL), BF16),
            compiler_params=_params(("parallel",)), name="prenorm",
        )(x_all, mod, norm_g)
    return pl.pallas_call(
        _router_kernel, grid=(ALL_TOKENS // tm,),
        in_specs=in_specs + [_resident((D_MODEL, LANES), lambda i: (0, 0))],
        out_specs=[row(D_MODEL), row(LANES)],
        out_shape=[jax.ShapeDtypeStruct((ALL_TOKENS, D_MODEL), F32),
                   jax.ShapeDtypeStruct((ALL_TOKENS, LANES), F32)],
        compiler_params=_params(("parallel",)), name="prenorm_router",
    )(x_all, mod, norm_g, router_pad)


def _ffn_kernel(tile_ref, expert_ref, valid_ref, h_ref, wg_ref, wu_ref, wd_ref, o_ref):
    s = pl.program_id(0)
    j = pl.program_id(1)

    @pl.when(valid_ref[s] == 1)
    def _():
        h = h_ref[...].astype(BF16)
        a = jnp.dot(h, wg_ref[0], preferred_element_type=F32)
        b = jnp.dot(h, wu_ref[0], preferred_element_type=F32)
        inter = (a * _sigmoid(a)) * b
        contrib = jnp.dot(inter.astype(BF16), wd_ref[0], preferred_element_type=F32)

        @pl.when(j == 0)
        def _():
            o_ref[...] = contrib

        @pl.when(j != 0)
        def _():
            o_ref[...] += contrib

    @pl.when(jnp.logical_and(valid_ref[s] == 0, j == 0))
    def _():
        o_ref[...] = jnp.zeros_like(o_ref)


def _expert_ffn(h, w_gate, w_up, w_down, step_tile, step_expert, step_valid, tm, tf, name):
    n_rows = h.shape[0]
    d_ff = w_gate.shape[-1]
    n_steps = step_tile.shape[0]
    n_ff = d_ff // tf
    ff = lambda s, j, f: jnp.where(f[s] == 1, j, n_ff - 1)
    grid_spec = pltpu.PrefetchScalarGridSpec(
        num_scalar_prefetch=3,
        grid=(n_steps, n_ff),
        in_specs=[
            pl.BlockSpec((tm, D_MODEL), lambda s, j, t, e, f: (t[s], 0)),
            pl.BlockSpec((1, D_MODEL, tf), lambda s, j, t, e, f: (e[s], 0, ff(s, j, f))),
            pl.BlockSpec((1, D_MODEL, tf), lambda s, j, t, e, f: (e[s], 0, ff(s, j, f))),
            pl.BlockSpec((1, tf, D_MODEL), lambda s, j, t, e, f: (e[s], ff(s, j, f), 0)),
        ],
        out_specs=pl.BlockSpec((tm, D_MODEL), lambda s, j, t, e, f: (t[s], 0)),
    )
    return pl.pallas_call(
        _ffn_kernel,
        grid_spec=grid_spec,
        out_shape=jax.ShapeDtypeStruct((n_rows, D_MODEL), F32),
        compiler_params=_params(("arbitrary", "arbitrary")),
        name=name,
    )(step_tile, step_expert, step_valid, h, w_gate, w_up, w_down)


TOP_K = 2
MOE_ROW_TILE = 512
MOE_ROWS = TOP_K * ALL_TOKENS + N_EXPERTS * MOE_ROW_TILE


def _route_plan(route):
    tm = MOE_ROW_TILE
    n_steps = MOE_ROWS // tm
    expert = route[:, ROUTE_IDX_LANE:ROUTE_IDX_LANE + TOP_K].astype(jnp.int32).reshape(-1)
    onehot = (expert[:, None] == jnp.arange(N_EXPERTS, dtype=jnp.int32)[None, :]).astype(jnp.int32)
    running = jnp.cumsum(onehot, axis=0)
    rank = jnp.sum(onehot * (running - 1), axis=1)
    counts = running[-1]
    tiles = (counts + tm - 1) // tm
    tile_end = jnp.cumsum(tiles)
    starts = (tile_end - tiles) * tm
    pos = jnp.sum(onehot * starts[None, :], axis=1) + rank
    last = tile_end[-1] - 1
    step = jnp.arange(n_steps, dtype=jnp.int32)
    step_expert = jnp.sum((jnp.minimum(step, last)[:, None] >= tile_end[None, :]).astype(jnp.int32), axis=1)
    step_valid = (step <= last).astype(jnp.int32)
    return pos.astype(jnp.int32), step, step_expert.astype(jnp.int32), step_valid


def _dispatch_kernel(pos_ref, h_ref, init_ref, xs_ref, sem):
    del init_ref
    tm = h_ref.shape[0]
    base = pl.program_id(0) * tm

    def issue(r, carry):
        for k in range(TOP_K):
            dst = pos_ref[TOP_K * (base + r) + k]
            pltpu.make_async_copy(h_ref.at[pl.ds(r, 1)], xs_ref.at[pl.ds(dst, 1)], sem).start()
        return carry

    lax.fori_loop(0, tm, issue, 0)
    for k in range(TOP_K):
        pltpu.make_async_copy(h_ref, xs_ref.at[pl.ds(0, tm)], sem).wait()


def _dispatch(h, pos):
    tm = ROW_TILE
    grid_spec = pltpu.PrefetchScalarGridSpec(
        num_scalar_prefetch=1,
        grid=(ALL_TOKENS // tm,),
        in_specs=[pl.BlockSpec((tm, D_MODEL), lambda i, p: (i, 0)),
                  pl.BlockSpec(memory_space=pl.ANY)],
        out_specs=pl.BlockSpec(memory_space=pl.ANY),
        scratch_shapes=[pltpu.SemaphoreType.DMA(())],
    )
    return pl.pallas_call(
        _dispatch_kernel,
        grid_spec=grid_spec,
        out_shape=jax.ShapeDtypeStruct((MOE_ROWS, D_MODEL), F32),
        input_output_aliases={2: 0},
        compiler_params=_params(("arbitrary",)),
        name="moe_dispatch",
    )(pos, h, jnp.zeros((MOE_ROWS, D_MODEL), F32))


def _combine_kernel(pos_ref, x_ref, route_ref, mod_ref, g_ref, ys_ref, o_ref, y_scr, sems, *, final):
    tm = x_ref.shape[0]
    base = pl.program_id(0) * tm

    def issue(r, carry):
        for k in range(TOP_K):
            src = pos_ref[TOP_K * (base + r) + k]
            pltpu.make_async_copy(ys_ref.at[pl.ds(src, 1)], y_scr.at[k, pl.ds(r, 1)], sems.at[k]).start()
        return carry

    lax.fori_loop(0, tm, issue, 0)
    for k in range(TOP_K):
        pltpu.make_async_copy(ys_ref.at[pl.ds(0, tm)], y_scr.at[k], sems.at[k]).wait()
    route = route_ref[...]
    f = (route[:, ROUTE_W_LANE:ROUTE_W_LANE + 1] * y_scr[0]
         + route[:, ROUTE_W_LANE + 1:ROUTE_W_LANE + 2] * y_scr[1])
    x = x_ref[...] + mod_ref[0][5:6] * f
    if final:
        ms = jnp.mean(x * x, axis=-1, keepdims=True)
        x = x * lax.rsqrt(ms + EPS) * g_ref[...]
    o_ref[...] = x


def _combine(x_all, y_sorted, pos, route, mod, final_g, final):
    tm = ROW_TILE
    row = lambda w: pl.BlockSpec((tm, w), lambda i, p: (i, 0))
    grid_spec = pltpu.PrefetchScalarGridSpec(
        num_scalar_prefetch=1,
        grid=(ALL_TOKENS // tm,),
        in_specs=[row(D_MODEL), row(LANES),
                  pl.BlockSpec((1, 6, D_MODEL), lambda i, p: (_mod_row(i, tm), 0, 0)),
                  pl.BlockSpec((1, D_MODEL), lambda i, p: (0, 0)),
                  pl.BlockSpec(memory_space=pl.ANY)],
        out_specs=row(D_MODEL),
        scratch_shapes=[pltpu.VMEM((TOP_K, tm, D_MODEL), F32), pltpu.SemaphoreType.DMA((TOP_K,))],
    )
    return pl.pallas_call(
        functools.partial(_combine_kernel, final=final),
        grid_spec=grid_spec,
        out_shape=jax.ShapeDtypeStruct((ALL_TOKENS, D_MODEL), F32),
        compiler_params=_params(("arbitrary",)),
        name="moe_combine",
    )(pos, x_all, route, mod, final_g, y_sorted)


def _residual_kernel(x_ref, f_ref, mod_ref, o_ref):
    o_ref[...] = x_ref[...] + mod_ref[0][5:6] * f_ref[...]


def _residual_norm_kernel(x_ref, f_ref, mod_ref, g_ref, o_ref):
    x = x_ref[...] + mod_ref[0][5:6] * f_ref[...]
    ms = jnp.mean(x * x, axis=-1, keepdims=True)
    o_ref[...] = x * lax.rsqrt(ms + EPS) * g_ref[...]


def _residual(x_all, f, mod, final_g=None):
    tm = ROW_TILE
    row = pl.BlockSpec((tm, D_MODEL), lambda i: (i, 0))
    in_specs = [row, row, pl.BlockSpec((1, 6, D_MODEL), lambda i: (_mod_row(i, tm), 0, 0))]
    args = [x_all, f, mod]
    body = _residual_kernel
    if final_g is not None:
        in_specs.append(_resident((1, D_MODEL), lambda i: (0, 0)))
        args.append(final_g)
        body = _residual_norm_kernel
    return pl.pallas_call(
        body, grid=(ALL_TOKENS // tm,), in_specs=in_specs, out_specs=row,
        out_shape=jax.ShapeDtypeStruct((ALL_TOKENS, D_MODEL), F32),
        compiler_params=_params(("parallel",)),
        name="residual" if final_g is None else "residual_final_norm",
    )(*args)


def _rope_tables():
    rows = DEC_SEQ // GRID_W
    row = jnp.repeat(jnp.arange(rows, dtype=F32), GRID_W)
    col = jnp.tile(jnp.arange(GRID_W, dtype=F32), rows)
    inv = ROPE_THETA ** (-jnp.arange(0, AXIS_DIM, 2, dtype=F32) / AXIS_DIM)
    ang = jnp.concatenate([row[:, None] * inv, col[:, None] * inv], axis=-1)
    cos, sin = jnp.cos(ang), jnp.sin(ang)
    reps = LANES // HEAD_DIM
    cos_l = jnp.tile(jnp.concatenate([cos, cos], axis=-1), (1, reps))
    sin_l = jnp.tile(jnp.concatenate([-sin, sin], axis=-1), (1, reps))
    ident = (jnp.ones((ROW_TILE, LANES), F32), jnp.zeros((ROW_TILE, LANES), F32))
    return jnp.concatenate([ident[0], cos_l], axis=0), jnp.concatenate([ident[1], sin_l], axis=0)


def _dense_steps(n_tiles):
    tile = jnp.arange(n_tiles, dtype=jnp.int32)
    return tile, jnp.zeros((n_tiles,), jnp.int32), jnp.ones((n_tiles,), jnp.int32)


def kernel(x_prompt, x_sample, cache_k, cache_v, state_ssm, c, c_ctx, w_ada, b_ada, norm1_g, norm2_g, w_in, q_norm_g, k_norm_g, ssm_a_re, ssm_a_im, ssm_log_dt, ssm_b_re, ssm_b_im, ssm_c_re, ssm_c_im, ssm_d, w_glu, w_attn_br, w_out, ffn_w_gate, ffn_w_up, ffn_w_down, moe_router, moe_w_gate, moe_w_up, moe_w_down, final_norm_g):
    P = dict(ssm_a_re=ssm_a_re, ssm_a_im=ssm_a_im, ssm_log_dt=ssm_log_dt, ssm_b_re=ssm_b_re,
             ssm_b_im=ssm_b_im, ssm_c_re=ssm_c_re, ssm_c_im=ssm_c_im)

    c_rows = jnp.zeros((SUBLANES, D_MODEL), F32).at[0].set(c_ctx).at[1:1 + DEC_BATCH].set(c)
    mod_all = _ada_modulation(c_rows, w_ada, b_ada)
    mod_all = mod_all[:, :N_MOD_ROWS].reshape(DEPTH, N_MOD_ROWS, 6, D_MODEL)

    rope_cos, rope_sin = _rope_tables()
    head_id = jnp.arange(Q_W, dtype=jnp.int32) // HEAD_DIM
    ones_bd = (head_id[:, None] == head_id[None, :]).astype(BF16)

    x_all = jnp.concatenate([x_prompt.reshape(CTX_TOKENS, D_MODEL), x_sample.reshape(DEC_TOKENS, D_MODEL)], axis=0)
    ctx_q_tiles = SEQ // ATTN_Q_TILE
    dec_q_tiles = DEC_SEQ // ATTN_Q_TILE
    nb_pad = SUBLANES
    ks, vs, ss = [], [], []

    for l in range(DEPTH):
        mod = mod_all[l]
        u, q, k, k_raw, v, sgs, sga = _in_projection(
            x_all, mod, norm1_g[l][None, :], w_in[l].astype(BF16), ones_bd,
            jnp.tile(q_norm_g[l], N_HEADS)[None, :], jnp.tile(k_norm_g[l], N_KV_HEADS)[None, :],
            rope_cos, rope_sin)

        q_tiles = q.reshape(ALL_TOKENS // ATTN_Q_TILE, ATTN_Q_TILE, Q_W)
        k_ctx = k_raw[:CTX_TOKENS].reshape(BATCH, SEQ, KV_W)
        v_ctx = v[:CTX_TOKENS].reshape(BATCH, SEQ, KV_W)
        attn_ctx = _attention(q_tiles, k_ctx, v_ctx, BATCH, ctx_q_tiles, 0, "attention_context")
        k_dec = jnp.concatenate([cache_k[:, l].reshape(DEC_BATCH, PAST_LEN, KV_W),
                                 k[CTX_TOKENS:].reshape(DEC_BATCH, DEC_SEQ, KV_W)], axis=1)
        v_dec = jnp.concatenate([cache_v[:, l].reshape(DEC_BATCH, PAST_LEN, KV_W),
                                 v[CTX_TOKENS:].reshape(DEC_BATCH, DEC_SEQ, KV_W)], axis=1)
        attn_dec = _attention(q_tiles, k_dec, v_dec, DEC_BATCH, dec_q_tiles, CTX_TOKENS // ATTN_Q_TILE,
                              "attention_latent")
        attn = jnp.concatenate([attn_ctx.reshape(CTX_TOKENS, Q_W), attn_dec.reshape(DEC_TOKENS, Q_W)], axis=0)
        ks.append(k_ctx.reshape(BATCH, SEQ, N_KV_HEADS, HEAD_DIM))
        vs.append(v_ctx.reshape(BATCH, SEQ, N_KV_HEADS, HEAD_DIM))

        bd, a_rows, cd = _s5_operators(l, P)
        d_skip = ssm_d[l][None, :]
        u_ctx = u[:CTX_TOKENS].reshape(BATCH, SEQ, SSM_WIDTH).transpose(1, 0, 2)
        h0_ctx = jnp.zeros((N_DIRS, S5_N_SETS, BATCH, 2 * S5_SET_ST), F32)
        y_ctx, fin = _s5_branch(u_ctx, bd, a_rows, cd, h0_ctx, d_skip, "s5_context")
        u_dec = u[CTX_TOKENS:].reshape(DEC_BATCH, DEC_SEQ, SSM_WIDTH).transpose(1, 0, 2)
        u_dec = jnp.pad(u_dec, ((0, 0), (0, nb_pad - DEC_BATCH), (0, 0)))
        h0_dec = state_ssm[:, l].reshape(DEC_BATCH, N_DIRS, S5_N_SETS, S5_GROUP_SET * STATE_N, 2)
        h0_dec = h0_dec.transpose(1, 2, 0, 4, 3).reshape(N_DIRS, S5_N_SETS, DEC_BATCH, 2 * S5_SET_ST)
        h0_dec = jnp.pad(h0_dec, ((0, 0), (0, 0), (0, nb_pad - DEC_BATCH), (0, 0)))
        y_dec, _ = _s5_branch(u_dec, bd, a_rows, cd, h0_dec, d_skip, "s5_latent")
        y = jnp.concatenate([y_ctx.transpose(1, 0, 2).reshape(CTX_TOKENS, SSM_WIDTH),
                             y_dec[:, :DEC_BATCH].transpose(1, 0, 2).reshape(DEC_TOKENS, SSM_WIDTH)], axis=0)
        fin = fin.reshape(N_DIRS, S5_N_SETS, BATCH, 2, S5_GROUP_SET, STATE_N)
        ss.append(fin.transpose(2, 0, 1, 4, 5, 3).reshape(BATCH, N_DIRS, N_GROUPS, STATE_N, 2))

        x_all = _mixer_out(attn, y, sgs, sga, x_all, mod, w_attn_br[l].astype(BF16),
                           w_glu[l].astype(BF16), w_out[l].astype(BF16))

        i = l // 2
        last_layer = l == DEPTH - 1
        if l % 2 == 0:
            h2 = _prenorm(x_all, mod, norm2_g[l][None, :])
            tm = 1024
            n_tiles = ALL_TOKENS // tm
            tile, expert, valid = _dense_steps(n_tiles)
            f = _expert_ffn(h2, ffn_w_gate[i][None].astype(BF16), ffn_w_up[i][None].astype(BF16),
                            ffn_w_down[i][None].astype(BF16), tile, expert, valid, tm, 256, "ffn_dense")
            x_all = _residual(x_all, f, mod, final_norm_g[None, :] if last_layer else None)
        else:
            router_pad = jnp.pad(moe_router[i].astype(F32), ((0, 0), (0, LANES - N_EXPERTS)))
            h2, route = _prenorm(x_all, mod, norm2_g[l][None, :], router_pad)
            pos, tile, expert, valid = _route_plan(route)
            h_sorted = _dispatch(h2, pos)
            y_sorted = _expert_ffn(h_sorted, moe_w_gate[i].astype(BF16), moe_w_up[i].astype(BF16),
                                   moe_w_down[i].astype(BF16), tile, expert, valid, MOE_ROW_TILE, 512,
                                   "ffn_experts")
            x_all = _combine(x_all, y_sorted, pos, route, mod, final_norm_g[None, :], last_layer)

    y_prompt = x_all[:CTX_TOKENS].reshape(BATCH, SEQ, D_MODEL)
    y_sample = x_all[CTX_TOKENS:].reshape(DEC_BATCH, DEC_SEQ, D_MODEL)
    return (y_prompt, y_sample, jnp.stack(ks, axis=1), jnp.stack(vs, axis=1), jnp.stack(ss, axis=1))
```

```python
import functools
import math

import jax
import jax.numpy as jnp
from jax import lax
from jax.experimental import pallas as pl
from jax.experimental.pallas import tpu as pltpu

F32 = jnp.float32
BF16 = jnp.bfloat16

D_MODEL = 1024
BATCH = 32
SEQ = 256
DEPTH = 2
DEC_BATCH = 2
DEC_SEQ = 1024
PAST_LEN = 256
GRID_W = 64
HEAD_DIM = 64
N_HEADS = 16
N_KV_HEADS = 4
ROPE_THETA = 10000.0
AXIS_DIM = HEAD_DIM // 2
HALF = HEAD_DIM // 2
SSM_GROUP = 16
SSM_WIDTH = 512
N_GROUPS = SSM_WIDTH // SSM_GROUP
STATE_N = 64
N_DIRS = 2
D_FF = 2816
N_EXPERTS = 8
D_FF_EXPERT = 3584
EPS = 1e-6
Q_W = N_HEADS * HEAD_DIM
KV_W = N_KV_HEADS * HEAD_DIM
IN_W = SSM_WIDTH + Q_W + 2 * KV_W + 2 * D_MODEL

CTX_TOKENS = BATCH * SEQ
DEC_TOKENS = DEC_BATCH * DEC_SEQ
ALL_TOKENS = CTX_TOKENS + DEC_TOKENS

LANES = 128
MXU_WIDTH = 256
FF_CHUNK = MXU_WIDTH
SUBLANES = 8
VMEM_LIMIT_BYTES = 56 * 1024 * 1024

ROW_TILE = 512
ATTN_Q_TILE = 256
S5_GROUP_SET = 8
S5_SET_CH = S5_GROUP_SET * SSM_GROUP
S5_SET_ST = S5_GROUP_SET * STATE_N
S5_N_SETS = N_GROUPS // S5_GROUP_SET
S5_SCAN_UNROLL = 4
S5_PIECE = SEQ
N_MOD_ROWS = 1 + DEC_BATCH


def _params(sem):
    return pltpu.CompilerParams(dimension_semantics=sem, vmem_limit_bytes=VMEM_LIMIT_BYTES)


def _resident(shape, index_map):
    return pl.BlockSpec(shape, index_map, pipeline_mode=pl.Buffered(1))


def _sigmoid(x):
    return 1.0 / (1.0 + jnp.exp(-x))


def _mod_row(i, tm):
    nctx = CTX_TOKENS // tm
    per_b = DEC_SEQ // tm
    return jnp.where(i < nctx, 0, 1 + (i - nctx) // per_b)


def _rope_block(i, tm):
    nctx = CTX_TOKENS // tm
    per_b = DEC_SEQ // tm
    return jnp.where(i < nctx, 0, 1 + (i - nctx) % per_b)


def _rms_modulate(x, g, scale, shift):
    ms = jnp.mean(x * x, axis=-1, keepdims=True)
    return (x * lax.rsqrt(ms + EPS) * g) * (1.0 + scale) + shift


def _ada_kernel(c_ref, w_ref, b_ref, o_ref):
    c = c_ref[...]
    act = c * _sigmoid(c)
    o_ref[0] = jnp.dot(act.astype(BF16), w_ref[0].astype(BF16), preferred_element_type=F32) + b_ref[0]


def _ada_modulation(c_rows, w_ada, b_ada):
    tn = 1536
    return pl.pallas_call(
        _ada_kernel,
        grid=(DEPTH, 6 * D_MODEL // tn),
        in_specs=[
            pl.BlockSpec((SUBLANES, D_MODEL), lambda l, n: (0, 0)),
            pl.BlockSpec((1, D_MODEL, tn), lambda l, n: (l, 0, n)),
            pl.BlockSpec((1, 1, tn), lambda l, n: (l, 0, n)),
        ],
        out_specs=pl.BlockSpec((1, SUBLANES, tn), lambda l, n: (l, 0, n)),
        out_shape=jax.ShapeDtypeStruct((DEPTH, SUBLANES, 6 * D_MODEL), F32),
        compiler_params=_params(("parallel", "parallel")),
        name="ada_modulation",
    )(c_rows, w_ada, b_ada.reshape(DEPTH, 1, 6 * D_MODEL))


def _rotate_pairs(x, cos, sin_signed):
    n = x.shape[-1]
    lane = lax.broadcasted_iota(jnp.int32, x.shape, 1)
    first_half = (lane % HEAD_DIM) < HALF
    partner = jnp.where(first_half, pltpu.roll(x, n - HALF, 1), pltpu.roll(x, HALF, 1))
    reps = n // LANES
    return x * jnp.tile(cos, (1, reps)) + partner * jnp.tile(sin_signed, (1, reps))


def _inproj_kernel(x_ref, mod_ref, g_ref, w_ref, ones_ref, qg_ref, kg_ref, cos_ref, sin_ref,
                   u_ref, q_ref, k_ref, v_ref, sgs_ref, sga_ref):
    mod = mod_ref[0]
    h = _rms_modulate(x_ref[...], g_ref[...], mod[1:2], mod[0:1]).astype(BF16)

    def proj(lo, width):
        return jnp.dot(h, w_ref[:, lo:lo + width], preferred_element_type=F32)

    def head_norm(t, gain):
        sq = (t * t).astype(BF16)
        ss = jnp.concatenate(
            [jnp.dot(sq[:, c:c + MXU_WIDTH], ones_ref[...], preferred_element_type=F32)
             for c in range(0, t.shape[-1], MXU_WIDTH)], axis=-1)
        return t * lax.rsqrt(ss * (1.0 / HEAD_DIM) + EPS) * gain

    cos = cos_ref[...]
    sin = sin_ref[...]
    o = 0
    u_ref[...] = proj(o, SSM_WIDTH)
    o += SSM_WIDTH
    q = head_norm(proj(o, Q_W), qg_ref[...])
    q_ref[...] = (_rotate_pairs(q, cos, sin) * (1.0 / math.sqrt(HEAD_DIM))).astype(BF16)
    o += Q_W
    k = head_norm(proj(o, KV_W), kg_ref[...])
    k_ref[...] = _rotate_pairs(k, cos, sin)
    o += KV_W
    v_ref[...] = proj(o, KV_W)
    o += KV_W
    sgs_ref[...] = _sigmoid(proj(o, D_MODEL))
    o += D_MODEL
    sga_ref[...] = _sigmoid(proj(o, D_MODEL))


def _in_projection(x_all, mod, norm_g, w_in_bf16, ones_bd, q_gain, k_gain, rope_cos, rope_sin):
    tm = ROW_TILE
    n_tiles = ALL_TOKENS // tm
    row = lambda w: pl.BlockSpec((tm, w), lambda i: (i, 0))
    shapes = [(SSM_WIDTH, F32), (Q_W, BF16), (KV_W, F32), (KV_W, F32), (D_MODEL, F32), (D_MODEL, F32)]
    return pl.pallas_call(
        _inproj_kernel,
        grid=(n_tiles,),
        in_specs=[
            row(D_MODEL),
            pl.BlockSpec((1, 6, D_MODEL), lambda i: (_mod_row(i, tm), 0, 0)),
            _resident((1, D_MODEL), lambda i: (0, 0)),
            _resident((D_MODEL, IN_W), lambda i: (0, 0)),
            _resident((MXU_WIDTH, MXU_WIDTH), lambda i: (0, 0)),
            _resident((1, Q_W), lambda i: (0, 0)),
            _resident((1, KV_W), lambda i: (0, 0)),
            pl.BlockSpec((tm, LANES), lambda i: (_rope_block(i, tm), 0)),
            pl.BlockSpec((tm, LANES), lambda i: (_rope_block(i, tm), 0)),
        ],
        out_specs=[row(w) for w, _ in shapes],
        out_shape=[jax.ShapeDtypeStruct((ALL_TOKENS, w), dt) for w, dt in shapes],
        compiler_params=_params(("parallel",)),
        name="in_projection",
    )(x_all, mod, norm_g, w_in_bf16, ones_bd, q_gain, k_gain, rope_cos, rope_sin)


def _attn_kernel(*refs, with_cache):
    if with_cache:
        q_ref, k_ref, v_ref, ck_ref, cv_ref, o_ref = refs
        k_parts = (lambda sl: ck_ref[0, 0, :, sl], lambda sl: k_ref[0, :, sl])
        v_parts = (lambda sl: cv_ref[0, 0, :, sl], lambda sl: v_ref[0, :, sl])
    else:
        q_ref, k_ref, v_ref, o_ref = refs
        k_parts = (lambda sl: k_ref[0, :, sl],)
        v_parts = (lambda sl: v_ref[0, :, sl],)
    lane = lax.broadcasted_iota(jnp.int32, (1, LANES), 1)
    low = lane < HEAD_DIM
    groups_per_slab = LANES // HEAD_DIM
    rep = N_HEADS // N_KV_HEADS
    for g in range(N_KV_HEADS):
        slab = g // groups_per_slab
        own_low = (g % groups_per_slab) == 0
        keep = low if own_low else jnp.logical_not(low)

        def halves(parts):
            lanes = slice(slab * LANES, (slab + 1) * LANES)
            rows = jnp.concatenate([part(lanes) for part in parts], axis=0)
            same = jnp.where(keep, rows, 0.0)
            swapped = pltpu.roll(same, HEAD_DIM, 1)
            lo, hi = (same, swapped) if own_low else (swapped, same)
            return lo.astype(BF16), hi.astype(BF16)

        k_lo, k_hi = halves(k_parts)
        v_lo, v_hi = halves(v_parts)
        for j in range(g * rep // groups_per_slab, (g + 1) * rep // groups_per_slab):
            qs = q_ref[0, :, j * LANES:(j + 1) * LANES]
            acc = None
            for kk, vv in ((k_lo, v_lo), (k_hi, v_hi)):
                s = lax.dot_general(qs, kk, (((1,), (1,)), ((), ())), preferred_element_type=F32)
                p = jnp.exp(s - jnp.max(s, axis=-1, keepdims=True))
                denom = jnp.sum(p, axis=-1, keepdims=True)
                part = jnp.dot(p.astype(BF16), vv, preferred_element_type=F32) / denom
                acc = part if acc is None else acc + part
            o_ref[:, j * LANES:(j + 1) * LANES] = acc.astype(BF16)


def _attention(q, k, v, n_batch, seq_len, first_token, name, cache=None):
    tq = ATTN_Q_TILE
    per_b = seq_len // tq
    q_tiles = q.reshape(ALL_TOKENS // tq, tq, Q_W)
    first_tile = first_token // tq
    first_seq = first_token // seq_len
    kv_spec = pl.BlockSpec((1, seq_len, KV_W), lambda b, i: (first_seq + b, 0, 0))
    in_specs = [pl.BlockSpec((1, tq, Q_W), lambda b, i: (first_tile + b * per_b + i, 0, 0)), kv_spec, kv_spec]
    args = [q_tiles, k.reshape(ALL_TOKENS // seq_len, seq_len, KV_W), v.reshape(ALL_TOKENS // seq_len, seq_len, KV_W)]
    if cache is not None:
        cache_k, cache_v, layer = cache
        cache_spec = pl.BlockSpec((1, 1, PAST_LEN, KV_W), lambda b, i: (b, layer, 0, 0))
        in_specs += [cache_spec, cache_spec]
        args += [cache_k, cache_v]
    return pl.pallas_call(
        functools.partial(_attn_kernel, with_cache=cache is not None),
        grid=(n_batch, per_b),
        in_specs=in_specs,
        out_specs=pl.BlockSpec((tq, Q_W), lambda b, i: (b * per_b + i, 0)),
        out_shape=jax.ShapeDtypeStruct((n_batch * seq_len, Q_W), BF16),
        compiler_params=_params(("parallel", "parallel")),
        name=name,
    )(*args)


def _s5_kernel(u_ref, bd_ref, a_ref, cd_ref, h0_ref, dskip_ref, y_ref, hfin_ref, bu_scr, hs_scr, *, chain):
    seq_len = u_ref.shape[0]
    forward = pl.program_id(2) == 0
    a_re = jnp.broadcast_to(a_ref[0, 0, :, :S5_SET_ST], (SUBLANES, S5_SET_ST))
    a_im = jnp.broadcast_to(a_ref[0, 0, :, S5_SET_ST:], (SUBLANES, S5_SET_ST))
    u = u_ref[...]

    @pl.when(forward)
    def _():
        y_ref[...] = u * dskip_ref[...]

    u2 = u.reshape(seq_len * SUBLANES, S5_SET_CH).astype(BF16)
    bu_scr[...] = jnp.dot(u2, bd_ref[0, 0], preferred_element_type=F32)

    def scan(init, store):
        def step(t, hc):
            h_re, h_im = hc
            tt = jnp.where(forward, t, seq_len - 1 - t)
            r = pl.multiple_of(tt * SUBLANES, SUBLANES)
            n_re = a_re * h_re - a_im * h_im + bu_scr[pl.ds(r, SUBLANES), :S5_SET_ST]
            n_im = a_re * h_im + a_im * h_re + bu_scr[pl.ds(r, SUBLANES), S5_SET_ST:]
            if store:
                hs_scr[pl.ds(r, SUBLANES), :S5_SET_ST] = n_re
                hs_scr[pl.ds(r, SUBLANES), S5_SET_ST:] = n_im
            return n_re, n_im

        return lax.fori_loop(0, seq_len, step, init, unroll=S5_SCAN_UNROLL)

    h0 = h0_ref[0, 0]
    init = (h0[:, :S5_SET_ST], h0[:, S5_SET_ST:])
    if chain > 1:
        zero = jnp.zeros((SUBLANES, S5_SET_ST), F32)
        end_re, end_im = scan((zero, zero), store=False)
        p_re, p_im = a_re, a_im
        for _ in range(seq_len.bit_length() - 1):
            p_re, p_im = p_re * p_re - p_im * p_im, 2.0 * (p_re * p_im)
        piece = lax.broadcasted_iota(jnp.int32, (SUBLANES, S5_SET_ST), 0) % chain
        entry = piece == jnp.where(forward, 0, chain - 1)

        def from_neighbour(x):
            return jnp.where(forward, pltpu.roll(x, 1, 0), pltpu.roll(x, SUBLANES - 1, 0))

        i_re = jnp.where(entry, init[0], 0.0)
        i_im = jnp.where(entry, init[1], 0.0)
        for _ in range(chain - 1):
            o_re = p_re * i_re - p_im * i_im + end_re
            o_im = p_re * i_im + p_im * i_re + end_im
            i_re = jnp.where(entry, init[0], from_neighbour(o_re))
            i_im = jnp.where(entry, init[1], from_neighbour(o_im))
        init = (i_re, i_im)
    h_re, h_im = scan(init, store=True)
    hfin_ref[0, 0, :, :S5_SET_ST] = h_re
    hfin_ref[0, 0, :, S5_SET_ST:] = h_im
    yc = jnp.dot(hs_scr[...].astype(BF16), cd_ref[0, 0], preferred_element_type=F32)
    y_ref[...] += yc.reshape(seq_len, SUBLANES, S5_SET_CH)


def _s5_branch(u_tm, bd, a_bar, cd, h0, d_skip, chain, name):
    seq_len, nb, _ = u_tm.shape
    assert seq_len & (seq_len - 1) == 0 and SUBLANES % chain == 0
    st2 = 2 * S5_SET_ST
    par = lambda shape: pl.BlockSpec(shape, lambda b, s, d: (d, s, 0, 0))
    state = pl.BlockSpec((1, 1, SUBLANES, st2), lambda b, s, d: (d, s, b, 0))
    seq = pl.BlockSpec((seq_len, SUBLANES, S5_SET_CH), lambda b, s, d: (0, b, s))
    rows = seq_len * SUBLANES
    return pl.pallas_call(
        functools.partial(_s5_kernel, chain=chain),
        grid=(nb // SUBLANES, S5_N_SETS, N_DIRS),
        in_specs=[
            seq,
            par((1, 1, S5_SET_CH, st2)),
            par((1, 1, 1, st2)),
            par((1, 1, st2, S5_SET_CH)),
            state,
            pl.BlockSpec((1, S5_SET_CH), lambda b, s, d: (0, s)),
        ],
        out_specs=[seq, state],
        out_shape=[jax.ShapeDtypeStruct((seq_len, nb, SSM_WIDTH), F32),
                   jax.ShapeDtypeStruct((N_DIRS, S5_N_SETS, nb, st2), F32)],
        scratch_shapes=[pltpu.VMEM((rows, st2), F32), pltpu.VMEM((rows, st2), F32)],
        compiler_params=_params(("parallel", "parallel", "arbitrary")),
        name=name,
    )(u_tm, bd, a_bar, cd, h0, d_skip)


def _s5_operators(l, P):
    a_re, a_im = P['ssm_a_re'][l].astype(F32), P['ssm_a_im'][l].astype(F32)
    dt = jnp.exp(P['ssm_log_dt'][l].astype(F32))[..., None]
    mag = jnp.exp(a_re * dt)
    ab_re, ab_im = mag * jnp.cos(a_im * dt), mag * jnp.sin(a_im * dt)
    den = a_re * a_re + a_im * a_im
    k_re = ((ab_re - 1.0) * a_re + ab_im * a_im) / den
    k_im = (ab_im * a_re - (ab_re - 1.0) * a_im) / den
    b_re, b_im = P['ssm_b_re'][l].astype(F32), P['ssm_b_im'][l].astype(F32)
    bb_re = k_re[..., None] * b_re - k_im[..., None] * b_im
    bb_im = k_re[..., None] * b_im + k_im[..., None] * b_re
    eye = jnp.eye(S5_GROUP_SET, dtype=F32)

    def sets(t):
        return t.reshape(N_DIRS, S5_N_SETS, S5_GROUP_SET, *t.shape[2:])

    def b_operator(part):
        return jnp.einsum('dsgnp,gh->dsgphn', part, eye).reshape(N_DIRS, S5_N_SETS, S5_SET_CH, S5_SET_ST)

    def c_operator(part):
        return jnp.einsum('dsgpn,gh->dshngp', part, eye).reshape(N_DIRS, S5_N_SETS, S5_SET_ST, S5_SET_CH)

    bd = jnp.concatenate([b_operator(sets(bb_re)), b_operator(sets(bb_im))], axis=-1).astype(BF16)
    c_re = sets(P['ssm_c_re'][l].astype(F32))
    c_im = sets(P['ssm_c_im'][l].astype(F32))
    cd = jnp.concatenate([c_operator(c_re), -c_operator(c_im)], axis=-2).astype(BF16)
    row = lambda t: sets(t).reshape(N_DIRS, S5_N_SETS, 1, S5_SET_ST)
    a_rows = jnp.concatenate([row(ab_re), row(ab_im)], axis=-1)
    return bd, a_rows, cd


def _mixer_out_kernel(attn_ctx_ref, attn_dec_ref, y_ref, sgs_ref, sga_ref, x_ref, mod_ref,
                      wab_ref, wglu_ref, wout_ref, o_ref):
    is_ctx = pl.program_id(0) < CTX_TOKENS // ROW_TILE
    attn = jnp.where(is_ctx, attn_ctx_ref[...], attn_dec_ref[...])
    attn_br = jnp.dot(attn, wab_ref[...], preferred_element_type=F32)
    y = y_ref[...]
    gelu = 0.5 * y * (1.0 + jnp.tanh(math.sqrt(2.0 / math.pi) * (y + 0.044715 * (y * y * y))))
    glu = jnp.dot(gelu.astype(BF16), wglu_ref[...], preferred_element_type=F32)
    ssm_br = glu[:, :D_MODEL] * _sigmoid(glu[:, D_MODEL:])
    merged = sgs_ref[...] * ssm_br + sga_ref[...] * attn_br
    out = jnp.dot(merged.astype(BF16), wout_ref[...], preferred_element_type=F32)
    o_ref[...] = x_ref[...] + mod_ref[0][2:3] * out


def _mixer_out(attn_ctx, attn_dec, y, sgs, sga, x_all, mod, w_attn_br, w_glu, w_out):
    tm = ROW_TILE
    n_ctx = CTX_TOKENS // tm
    row = lambda w: pl.BlockSpec((tm, w), lambda i: (i, 0))
    return pl.pallas_call(
        _mixer_out_kernel,
        grid=(ALL_TOKENS // tm,),
        in_specs=[
            pl.BlockSpec((tm, Q_W), lambda i: (jnp.minimum(i, n_ctx - 1), 0)),
            pl.BlockSpec((tm, Q_W), lambda i: (jnp.maximum(i - n_ctx, 0), 0)),
            row(SSM_WIDTH), row(D_MODEL), row(D_MODEL), row(D_MODEL),
            pl.BlockSpec((1, 6, D_MODEL), lambda i: (_mod_row(i, tm), 0, 0)),
            _resident((Q_W, D_MODEL), lambda i: (0, 0)),
            _resident((SSM_WIDTH, 2 * D_MODEL), lambda i: (0, 0)),
            _resident((D_MODEL, D_MODEL), lambda i: (0, 0)),
        ],
        out_specs=row(D_MODEL),
        out_shape=jax.ShapeDtypeStruct((ALL_TOKENS, D_MODEL), F32),
        compiler_params=_params(("parallel",)),
        name="mixer_out",
    )(attn_ctx, attn_dec, y, sgs, sga, x_all, mod, w_attn_br, w_glu, w_out)


def _prenorm_kernel(x_ref, mod_ref, g_ref, h_ref):
    mod = mod_ref[0]
    h_ref[...] = _rms_modulate(x_ref[...], g_ref[...], mod[4:5], mod[3:4]).astype(BF16)


ROUTE_IDX_LANE = 0
ROUTE_W_LANE = 2


def _router_kernel(x_ref, mod_ref, g_ref, r_ref, h_ref, route_ref):
    mod = mod_ref[0]
    h = _rms_modulate(x_ref[...], g_ref[...], mod[4:5], mod[3:4])
    h_ref[...] = h
    logits = jnp.dot(h, r_ref[...], preferred_element_type=F32, precision=lax.Precision.HIGHEST)
    lane = lax.broadcasted_iota(jnp.int32, logits.shape, 1).astype(F32)
    logits = jnp.where(lane < N_EXPERTS, logits, -jnp.inf)
    top1 = jnp.max(logits, axis=-1, keepdims=True)
    idx1 = jnp.min(jnp.where(logits == top1, lane, float(LANES)), axis=-1, keepdims=True)
    rest = jnp.where(lane == idx1, -jnp.inf, logits)
    top2 = jnp.max(rest, axis=-1, keepdims=True)
    idx2 = jnp.min(jnp.where(rest == top2, lane, float(LANES)), axis=-1, keepdims=True)
    e2 = jnp.exp(top2 - top1)
    w1 = 1.0 / (1.0 + e2)
    w2 = e2 / (1.0 + e2)
    route = jnp.where(lane == ROUTE_IDX_LANE, idx1, 0.0)
    route = jnp.where(lane == ROUTE_IDX_LANE + 1, idx2, route)
    route = jnp.where(lane == ROUTE_W_LANE, w1, route)
    route_ref[...] = jnp.where(lane == ROUTE_W_LANE + 1, w2, route)


def _prenorm(x_all, mod, norm_g, router_pad=None):
    tm = ROW_TILE
    row = lambda w: pl.BlockSpec((tm, w), lambda i: (i, 0))
    in_specs = [row(D_MODEL),
                pl.BlockSpec((1, 6, D_MODEL), lambda i: (_mod_row(i, tm), 0, 0)),
                _resident((1, D_MODEL), lambda i: (0, 0))]
    if router_pad is None:
        return pl.pallas_call(
            _prenorm_kernel, grid=(ALL_TOKENS // tm,), in_specs=in_specs, out_specs=row(D_MODEL),
            out_shape=jax.ShapeDtypeStruct((ALL_TOKENS, D_MODEL), BF16),
            compiler_params=_params(("parallel",)), name="prenorm",
        )(x_all, mod, norm_g)
    return pl.pallas_call(
        _router_kernel, grid=(ALL_TOKENS // tm,),
        in_specs=in_specs + [_resident((D_MODEL, LANES), lambda i: (0, 0))],
        out_specs=[row(D_MODEL), row(LANES)],
        out_shape=[jax.ShapeDtypeStruct((ALL_TOKENS, D_MODEL), F32),
                   jax.ShapeDtypeStruct((ALL_TOKENS, LANES), F32)],
        compiler_params=_params(("parallel",)), name="prenorm_router",
    )(x_all, mod, norm_g, router_pad)


def _ffn_kernel(tile_ref, expert_ref, valid_ref, h_ref, wg_ref, wu_ref, wd_ref, o_ref):
    s = pl.program_id(0)
    j = pl.program_id(1)

    @pl.when(valid_ref[s] == 1)
    def _():
        h = h_ref[...].astype(BF16)
        contrib = None
        for c in range(wg_ref.shape[-1] // FF_CHUNK):
            cols = slice(c * FF_CHUNK, (c + 1) * FF_CHUNK)
            a = jnp.dot(h, wg_ref[0, :, cols], preferred_element_type=F32)
            b = jnp.dot(h, wu_ref[0, :, cols], preferred_element_type=F32)
            inter = ((a * _sigmoid(a)) * b).astype(BF16)
            part = jnp.dot(inter, wd_ref[0, cols, :], preferred_element_type=F32)
            contrib = part if contrib is None else contrib + part

        @pl.when(j == 0)
        def _():
            o_ref[...] = contrib

        @pl.when(j != 0)
        def _():
            o_ref[...] += contrib

    @pl.when(jnp.logical_and(valid_ref[s] == 0, j == 0))
    def _():
        o_ref[...] = jnp.zeros_like(o_ref)


def _expert_ffn(h, w_gate, w_up, w_down, step_tile, step_expert, step_valid, tm, tf, name):
    n_rows = h.shape[0]
    d_ff = w_gate.shape[-1]
    n_steps = step_tile.shape[0]
    n_ff = d_ff // tf
    mode = pl.Buffered(1) if (n_ff == 1 and w_gate.shape[0] == 1) else None
    ff = lambda s, j, f: jnp.where(f[s] == 1, j, n_ff - 1)
    grid_spec = pltpu.PrefetchScalarGridSpec(
        num_scalar_prefetch=3,
        grid=(n_steps, n_ff),
        in_specs=[
            pl.BlockSpec((tm, D_MODEL), lambda s, j, t, e, f: (t[s], 0)),
            pl.BlockSpec((1, D_MODEL, tf), lambda s, j, t, e, f: (e[s], 0, ff(s, j, f)), pipeline_mode=mode),
            pl.BlockSpec((1, D_MODEL, tf), lambda s, j, t, e, f: (e[s], 0, ff(s, j, f)), pipeline_mode=mode),
            pl.BlockSpec((1, tf, D_MODEL), lambda s, j, t, e, f: (e[s], ff(s, j, f), 0), pipeline_mode=mode),
        ],
        out_specs=pl.BlockSpec((tm, D_MODEL), lambda s, j, t, e, f: (t[s], 0)),
    )
    return pl.pallas_call(
        _ffn_kernel,
        grid_spec=grid_spec,
        out_shape=jax.ShapeDtypeStruct((n_rows, D_MODEL), F32),
        compiler_params=_params(("arbitrary", "arbitrary")),
        name=name,
    )(step_tile, step_expert, step_valid, h, w_gate, w_up, w_down)


TOP_K = 2
MOE_ROW_TILE = 512
MOE_ROWS = TOP_K * ALL_TOKENS + N_EXPERTS * MOE_ROW_TILE


def _route_plan(route):
    tm = MOE_ROW_TILE
    n_steps = MOE_ROWS // tm
    expert = route[:, ROUTE_IDX_LANE:ROUTE_IDX_LANE + TOP_K].astype(jnp.int32).reshape(-1)
    onehot = (expert[:, None] == jnp.arange(N_EXPERTS, dtype=jnp.int32)[None, :]).astype(jnp.int32)
    running = jnp.cumsum(onehot, axis=0)
    rank = jnp.sum(onehot * (running - 1), axis=1)
    counts = running[-1]
    tiles = (counts + tm - 1) // tm
    tile_end = jnp.cumsum(tiles)
    starts = (tile_end - tiles) * tm
    pos = jnp.sum(onehot * starts[None, :], axis=1) + rank
    last = tile_end[-1] - 1
    step = jnp.arange(n_steps, dtype=jnp.int32)
    step_expert = jnp.sum((jnp.minimum(step, last)[:, None] >= tile_end[None, :]).astype(jnp.int32), axis=1)
    step_valid = (step <= last).astype(jnp.int32)
    return pos.astype(jnp.int32), step, step_expert.astype(jnp.int32), step_valid


def _dispatch_kernel(pos_ref, h_ref, init_ref, xs_ref, sem):
    del init_ref
    tm = h_ref.shape[0]
    base = pl.program_id(0) * tm

    def issue(r, carry):
        for k in range(TOP_K):
            dst = pos_ref[TOP_K * (base + r) + k]
            pltpu.make_async_copy(h_ref.at[pl.ds(r, 1)], xs_ref.at[pl.ds(dst, 1)], sem).start()
        return carry

    lax.fori_loop(0, tm, issue, 0)
    for k in range(TOP_K):
        pltpu.make_async_copy(h_ref, xs_ref.at[pl.ds(0, tm)], sem).wait()


def _dispatch(h, pos):
    tm = ROW_TILE
    grid_spec = pltpu.PrefetchScalarGridSpec(
        num_scalar_prefetch=1,
        grid=(ALL_TOKENS // tm,),
        in_specs=[pl.BlockSpec((tm, D_MODEL), lambda i, p: (i, 0)),
                  pl.BlockSpec(memory_space=pl.ANY)],
        out_specs=pl.BlockSpec(memory_space=pl.ANY),
        scratch_shapes=[pltpu.SemaphoreType.DMA(())],
    )
    return pl.pallas_call(
        _dispatch_kernel,
        grid_spec=grid_spec,
        out_shape=jax.ShapeDtypeStruct((MOE_ROWS, D_MODEL), F32),
        input_output_aliases={2: 0},
        compiler_params=_params(("arbitrary",)),
        name="moe_dispatch",
    )(pos, h, jnp.zeros((MOE_ROWS, D_MODEL), F32))


def _combine_kernel(pos_ref, x_ref, route_ref, mod_ref, g_ref, ys_ref, o_ref, y_scr, sems, *, final):
    tm = x_ref.shape[0]
    base = pl.program_id(0) * tm

    def issue(r, carry):
        for k in range(TOP_K):
            src = pos_ref[TOP_K * (base + r) + k]
            pltpu.make_async_copy(ys_ref.at[pl.ds(src, 1)], y_scr.at[k, pl.ds(r, 1)], sems.at[k]).start()
        return carry

    lax.fori_loop(0, tm, issue, 0)
    for k in range(TOP_K):
        pltpu.make_async_copy(ys_ref.at[pl.ds(0, tm)], y_scr.at[k], sems.at[k]).wait()
    route = route_ref[...]
    f = (route[:, ROUTE_W_LANE:ROUTE_W_LANE + 1] * y_scr[0]
         + route[:, ROUTE_W_LANE + 1:ROUTE_W_LANE + 2] * y_scr[1])
    x = x_ref[...] + mod_ref[0][5:6] * f
    if final:
        ms = jnp.mean(x * x, axis=-1, keepdims=True)
        x = x * lax.rsqrt(ms + EPS) * g_ref[...]
    o_ref[...] = x


def _combine(x_all, y_sorted, pos, route, mod, final_g, final):
    tm = ROW_TILE
    row = lambda w: pl.BlockSpec((tm, w), lambda i, p: (i, 0))
    grid_spec = pltpu.PrefetchScalarGridSpec(
        num_scalar_prefetch=1,
        grid=(ALL_TOKENS // tm,),
        in_specs=[row(D_MODEL), row(LANES),
                  pl.BlockSpec((1, 6, D_MODEL), lambda i, p: (_mod_row(i, tm), 0, 0)),
                  pl.BlockSpec((1, D_MODEL), lambda i, p: (0, 0)),
                  pl.BlockSpec(memory_space=pl.ANY)],
        out_specs=row(D_MODEL),
        scratch_shapes=[pltpu.VMEM((TOP_K, tm, D_MODEL), F32), pltpu.SemaphoreType.DMA((TOP_K,))],
    )
    return pl.pallas_call(
        functools.partial(_combine_kernel, final=final),
        grid_spec=grid_spec,
        out_shape=jax.ShapeDtypeStruct((ALL_TOKENS, D_MODEL), F32),
        compiler_params=_params(("arbitrary",)),
        name="moe_combine",
    )(pos, x_all, route, mod, final_g, y_sorted)


def _residual_kernel(x_ref, f_ref, mod_ref, o_ref):
    o_ref[...] = x_ref[...] + mod_ref[0][5:6] * f_ref[...]


def _residual_norm_kernel(x_ref, f_ref, mod_ref, g_ref, o_ref):
    x = x_ref[...] + mod_ref[0][5:6] * f_ref[...]
    ms = jnp.mean(x * x, axis=-1, keepdims=True)
    o_ref[...] = x * lax.rsqrt(ms + EPS) * g_ref[...]


def _residual(x_all, f, mod, final_g=None):
    tm = ROW_TILE
    row = pl.BlockSpec((tm, D_MODEL), lambda i: (i, 0))
    in_specs = [row, row, pl.BlockSpec((1, 6, D_MODEL), lambda i: (_mod_row(i, tm), 0, 0))]
    args = [x_all, f, mod]
    body = _residual_kernel
    if final_g is not None:
        in_specs.append(_resident((1, D_MODEL), lambda i: (0, 0)))
        args.append(final_g)
        body = _residual_norm_kernel
    return pl.pallas_call(
        body, grid=(ALL_TOKENS // tm,), in_specs=in_specs, out_specs=row,
        out_shape=jax.ShapeDtypeStruct((ALL_TOKENS, D_MODEL), F32),
        compiler_params=_params(("parallel",)),
        name="residual" if final_g is None else "residual_final_norm",
    )(*args)


def _rope_tables():
    rows = DEC_SEQ // GRID_W
    row = jnp.repeat(jnp.arange(rows, dtype=F32), GRID_W)
    col = jnp.tile(jnp.arange(GRID_W, dtype=F32), rows)
    inv = ROPE_THETA ** (-jnp.arange(0, AXIS_DIM, 2, dtype=F32) / AXIS_DIM)
    ang = jnp.concatenate([row[:, None] * inv, col[:, None] * inv], axis=-1)
    cos, sin = jnp.cos(ang), jnp.sin(ang)
    reps = LANES // HEAD_DIM
    cos_l = jnp.tile(jnp.concatenate([cos, cos], axis=-1), (1, reps))
    sin_l = jnp.tile(jnp.concatenate([-sin, sin], axis=-1), (1, reps))
    ident = (jnp.ones((ROW_TILE, LANES), F32), jnp.zeros((ROW_TILE, LANES), F32))
    return jnp.concatenate([ident[0], cos_l], axis=0), jnp.concatenate([ident[1], sin_l], axis=0)


def _dense_steps(n_tiles):
    tile = jnp.arange(n_tiles, dtype=jnp.int32)
    return tile, jnp.zeros((n_tiles,), jnp.int32), jnp.ones((n_tiles,), jnp.int32)


def kernel(x_prompt, x_sample, cache_k, cache_v, state_ssm, c, c_ctx, w_ada, b_ada, norm1_g, norm2_g, w_in, q_norm_g, k_norm_g, ssm_a_re, ssm_a_im, ssm_log_dt, ssm_b_re, ssm_b_im, ssm_c_re, ssm_c_im, ssm_d, w_glu, w_attn_br, w_out, ffn_w_gate, ffn_w_up, ffn_w_down, moe_router, moe_w_gate, moe_w_up, moe_w_down, final_norm_g):
    P = dict(ssm_a_re=ssm_a_re, ssm_a_im=ssm_a_im, ssm_log_dt=ssm_log_dt, ssm_b_re=ssm_b_re,
             ssm_b_im=ssm_b_im, ssm_c_re=ssm_c_re, ssm_c_im=ssm_c_im)

    c_rows = jnp.zeros((SUBLANES, D_MODEL), F32).at[0].set(c_ctx).at[1:1 + DEC_BATCH].set(c)
    mod_all = _ada_modulation(c_rows, w_ada, b_ada)
    mod_all = mod_all[:, :N_MOD_ROWS].reshape(DEPTH, N_MOD_ROWS, 6, D_MODEL)

    rope_cos, rope_sin = _rope_tables()
    head_id = jnp.arange(MXU_WIDTH, dtype=jnp.int32) // HEAD_DIM
    ones_bd = (head_id[:, None] == head_id[None, :]).astype(BF16)

    x_all = jnp.concatenate([x_prompt.reshape(CTX_TOKENS, D_MODEL), x_sample.reshape(DEC_TOKENS, D_MODEL)], axis=0)
    cache_k4 = cache_k.reshape(DEC_BATCH, DEPTH, PAST_LEN, KV_W)
    cache_v4 = cache_v.reshape(DEC_BATCH, DEPTH, PAST_LEN, KV_W)
    chain = DEC_SEQ // S5_PIECE
    dec_pieces = DEC_BATCH * chain
    ks, vs, ss = [], [], []

    for l in range(DEPTH):
        mod = mod_all[l]
        u, q, k, v, sgs, sga = _in_projection(
            x_all, mod, norm1_g[l][None, :], w_in[l].astype(BF16), ones_bd,
            jnp.tile(q_norm_g[l], N_HEADS)[None, :], jnp.tile(k_norm_g[l], N_KV_HEADS)[None, :],
            rope_cos, rope_sin)

        attn_ctx = _attention(q, k, v, BATCH, SEQ, 0, "attention_context")
        attn_dec = _attention(q, k, v, DEC_BATCH, DEC_SEQ, CTX_TOKENS, "attention_latent",
                              cache=(cache_k4, cache_v4, l))
        ks.append(k[:CTX_TOKENS].reshape(BATCH, SEQ, N_KV_HEADS, HEAD_DIM))
        vs.append(v[:CTX_TOKENS].reshape(BATCH, SEQ, N_KV_HEADS, HEAD_DIM))

        bd, a_rows, cd = _s5_operators(l, P)
        d_skip = ssm_d[l][None, :]
        u_ctx = u[:CTX_TOKENS].reshape(BATCH, SEQ, SSM_WIDTH).transpose(1, 0, 2)
        h0_ctx = jnp.zeros((N_DIRS, S5_N_SETS, BATCH, 2 * S5_SET_ST), F32)
        y_ctx, fin = _s5_branch(u_ctx, bd, a_rows, cd, h0_ctx, d_skip, 1, "s5_context")
        u_dec = u[CTX_TOKENS:].reshape(dec_pieces, S5_PIECE, SSM_WIDTH).transpose(1, 0, 2)
        h0_dec = state_ssm[:, l].reshape(DEC_BATCH, N_DIRS, S5_N_SETS, S5_GROUP_SET * STATE_N, 2)
        h0_dec = h0_dec.transpose(1, 2, 0, 4, 3).reshape(N_DIRS, S5_N_SETS, DEC_BATCH, 2 * S5_SET_ST)
        h0_dec = jnp.repeat(h0_dec, chain, axis=2)
        y_dec, _ = _s5_branch(u_dec, bd, a_rows, cd, h0_dec, d_skip, chain, "s5_latent")
        y = jnp.concatenate([y_ctx.transpose(1, 0, 2).reshape(CTX_TOKENS, SSM_WIDTH),
                             y_dec.transpose(1, 0, 2).reshape(DEC_TOKENS, SSM_WIDTH)], axis=0)
        fin = fin.reshape(N_DIRS, S5_N_SETS, BATCH, 2, S5_GROUP_SET, STATE_N)
        ss.append(fin.transpose(2, 0, 1, 4, 5, 3).reshape(BATCH, N_DIRS, N_GROUPS, STATE_N, 2))

        x_all = _mixer_out(attn_ctx, attn_dec, y, sgs, sga, x_all, mod, w_attn_br[l].astype(BF16),
                           w_glu[l].astype(BF16), w_out[l].astype(BF16))

        i = l // 2
        last_layer = l == DEPTH - 1
        if l % 2 == 0:
            h2 = _prenorm(x_all, mod, norm2_g[l][None, :])
            tm = ROW_TILE
            n_tiles = ALL_TOKENS // tm
            tile, expert, valid = _dense_steps(n_tiles)
            f = _expert_ffn(h2, ffn_w_gate[i][None].astype(BF16), ffn_w_up[i][None].astype(BF16),
                            ffn_w_down[i][None].astype(BF16), tile, expert, valid, tm, D_FF, "ffn_dense")
            x_all = _residual(x_all, f, mod, final_norm_g[None, :] if last_layer else None)
        else:
            router_pad = jnp.pad(moe_router[i].astype(F32), ((0, 0), (0, LANES - N_EXPERTS)))
            h2, route = _prenorm(x_all, mod, norm2_g[l][None, :], router_pad)
            pos, tile, expert, valid = _route_plan(route)
            h_sorted = _dispatch(h2, pos)
            y_sorted = _expert_ffn(h_sorted, moe_w_gate[i].astype(BF16), moe_w_up[i].astype(BF16),
                                   moe_w_down[i].astype(BF16), tile, expert, valid, MOE_ROW_TILE, D_FF_EXPERT // 2,
                                   "ffn_experts")
            x_all = _combine(x_all, y_sorted, pos, route, mod, final_norm_g[None, :], last_layer)

    y_prompt = x_all[:CTX_TOKENS].reshape(BATCH, SEQ, D_MODEL)
    y_sample = x_all[CTX_TOKENS:].reshape(DEC_BATCH, DEC_SEQ, D_MODEL)
    return (y_prompt, y_sample, jnp.stack(ks, axis=1), jnp.stack(vs, axis=1), jnp.stack(ss, axis=1))
```

```python
import functools
import math

import jax
import jax.numpy as jnp
from jax import lax
from jax.experimental import pallas as pl
from jax.experimental.pallas import tpu as pltpu

F32 = jnp.float32
BF16 = jnp.bfloat16

D_MODEL = 1024
BATCH = 32
SEQ = 256
DEPTH = 2
DEC_BATCH = 2
DEC_SEQ = 1024
PAST_LEN = 256
GRID_W = 64
HEAD_DIM = 64
N_HEADS = 16
N_KV_HEADS = 4
ROPE_THETA = 10000.0
AXIS_DIM = HEAD_DIM // 2
HALF = HEAD_DIM // 2
SSM_GROUP = 16
SSM_WIDTH = 512
N_GROUPS = SSM_WIDTH // SSM_GROUP
STATE_N = 64
N_DIRS = 2
D_FF = 2816
N_EXPERTS = 8
D_FF_EXPERT = 3584
EPS = 1e-6
Q_W = N_HEADS * HEAD_DIM
KV_W = N_KV_HEADS * HEAD_DIM
IN_W = SSM_WIDTH + Q_W + 2 * KV_W + 2 * D_MODEL

CTX_TOKENS = BATCH * SEQ
DEC_TOKENS = DEC_BATCH * DEC_SEQ
ALL_TOKENS = CTX_TOKENS + DEC_TOKENS

LANES = 128
MXU_WIDTH = 256
FF_CHUNK = MXU_WIDTH
SUBLANES = 8
VMEM_LIMIT_BYTES = 56 * 1024 * 1024

ROW_TILE = 512
ATTN_Q_TILE = 256
S5_GROUP_SET = 8
S5_SET_CH = S5_GROUP_SET * SSM_GROUP
S5_SET_ST = S5_GROUP_SET * STATE_N
S5_N_SETS = N_GROUPS // S5_GROUP_SET
S5_SCAN_UNROLL = 4
S5_PIECE = SEQ
N_MOD_ROWS = 1 + DEC_BATCH


def _params(sem):
    return pltpu.CompilerParams(dimension_semantics=sem, vmem_limit_bytes=VMEM_LIMIT_BYTES)


def _resident(shape, index_map):
    return pl.BlockSpec(shape, index_map, pipeline_mode=pl.Buffered(1))


def _sigmoid(x):
    return 1.0 / (1.0 + jnp.exp(-x))


def _mod_row(i, tm):
    nctx = CTX_TOKENS // tm
    per_b = DEC_SEQ // tm
    return jnp.where(i < nctx, 0, 1 + (i - nctx) // per_b)


def _rope_block(i, tm):
    nctx = CTX_TOKENS // tm
    per_b = DEC_SEQ // tm
    return jnp.where(i < nctx, 0, 1 + (i - nctx) % per_b)


def _rms_modulate(x, g, scale, shift):
    ms = jnp.mean(x * x, axis=-1, keepdims=True)
    return (x * lax.rsqrt(ms + EPS) * g) * (1.0 + scale) + shift


def _ada_kernel(c_ref, w_ref, b_ref, o_ref):
    c = c_ref[...]
    act = c * _sigmoid(c)
    o_ref[0] = jnp.dot(act.astype(BF16), w_ref[0].astype(BF16), preferred_element_type=F32) + b_ref[0]


def _ada_modulation(c_rows, w_ada, b_ada):
    tn = 1536
    return pl.pallas_call(
        _ada_kernel,
        grid=(DEPTH, 6 * D_MODEL // tn),
        in_specs=[
            pl.BlockSpec((SUBLANES, D_MODEL), lambda l, n: (0, 0)),
            pl.BlockSpec((1, D_MODEL, tn), lambda l, n: (l, 0, n)),
            pl.BlockSpec((1, 1, tn), lambda l, n: (l, 0, n)),
        ],
        out_specs=pl.BlockSpec((1, SUBLANES, tn), lambda l, n: (l, 0, n)),
        out_shape=jax.ShapeDtypeStruct((DEPTH, SUBLANES, 6 * D_MODEL), F32),
        compiler_params=_params(("parallel", "parallel")),
        name="ada_modulation",
    )(c_rows, w_ada, b_ada.reshape(DEPTH, 1, 6 * D_MODEL))


def _rotate_pairs(x, cos, sin_signed):
    n = x.shape[-1]
    lane = lax.broadcasted_iota(jnp.int32, x.shape, 1)
    first_half = (lane % HEAD_DIM) < HALF
    partner = jnp.where(first_half, pltpu.roll(x, n - HALF, 1), pltpu.roll(x, HALF, 1))
    reps = n // LANES
    return x * jnp.tile(cos, (1, reps)) + partner * jnp.tile(sin_signed, (1, reps))


def _inproj_kernel(x_ref, mod_ref, g_ref, w_ref, ones_ref, qg_ref, kg_ref, cos_ref, sin_ref,
                   u_ref, q_ref, k_ref, v_ref, sgs_ref, sga_ref):
    mod = mod_ref[0]
    h = _rms_modulate(x_ref[...], g_ref[...], mod[1:2], mod[0:1]).astype(BF16)

    def proj(lo, width):
        return jnp.dot(h, w_ref[:, lo:lo + width], preferred_element_type=F32)

    def head_norm(t, gain):
        sq = (t * t).astype(BF16)
        ss = jnp.concatenate(
            [jnp.dot(sq[:, c:c + MXU_WIDTH], ones_ref[...], preferred_element_type=F32)
             for c in range(0, t.shape[-1], MXU_WIDTH)], axis=-1)
        return t * lax.rsqrt(ss * (1.0 / HEAD_DIM) + EPS) * gain

    cos = cos_ref[...]
    sin = sin_ref[...]
    o = 0
    u_ref[...] = proj(o, SSM_WIDTH)
    o += SSM_WIDTH
    q = head_norm(proj(o, Q_W), qg_ref[...])
    q_ref[...] = (_rotate_pairs(q, cos, sin) * (1.0 / math.sqrt(HEAD_DIM))).astype(BF16)
    o += Q_W
    k = head_norm(proj(o, KV_W), kg_ref[...])
    k_ref[...] = _rotate_pairs(k, cos, sin)
    o += KV_W
    v_ref[...] = proj(o, KV_W)
    o += KV_W
    sgs_ref[...] = _sigmoid(proj(o, D_MODEL))
    o += D_MODEL
    sga_ref[...] = _sigmoid(proj(o, D_MODEL))


def _in_projection(x_all, mod, norm_g, w_in_bf16, ones_bd, q_gain, k_gain, rope_cos, rope_sin):
    tm = ROW_TILE
    n_tiles = ALL_TOKENS // tm
    row = lambda w: pl.BlockSpec((tm, w), lambda i: (i, 0))
    shapes = [(SSM_WIDTH, F32), (Q_W, BF16), (KV_W, F32), (KV_W, F32), (D_MODEL, F32), (D_MODEL, F32)]
    return pl.pallas_call(
        _inproj_kernel,
        grid=(n_tiles,),
        in_specs=[
            row(D_MODEL),
            pl.BlockSpec((1, 6, D_MODEL), lambda i: (_mod_row(i, tm), 0, 0)),
            _resident((1, D_MODEL), lambda i: (0, 0)),
            _resident((D_MODEL, IN_W), lambda i: (0, 0)),
            _resident((MXU_WIDTH, MXU_WIDTH), lambda i: (0, 0)),
            _resident((1, Q_W), lambda i: (0, 0)),
            _resident((1, KV_W), lambda i: (0, 0)),
            pl.BlockSpec((tm, LANES), lambda i: (_rope_block(i, tm), 0)),
            pl.BlockSpec((tm, LANES), lambda i: (_rope_block(i, tm), 0)),
        ],
        out_specs=[row(w) for w, _ in shapes],
        out_shape=[jax.ShapeDtypeStruct((ALL_TOKENS, w), dt) for w, dt in shapes],
        compiler_params=_params(("parallel",)),
        name="in_projection",
    )(x_all, mod, norm_g, w_in_bf16, ones_bd, q_gain, k_gain, rope_cos, rope_sin)


def _attn_kernel(*refs, with_cache):
    if with_cache:
        q_ref, k_ref, v_ref, ck_ref, cv_ref, o_ref = refs
        k_parts = (lambda sl: ck_ref[0, 0, :, sl], lambda sl: k_ref[0, :, sl])
        v_parts = (lambda sl: cv_ref[0, 0, :, sl], lambda sl: v_ref[0, :, sl])
    else:
        q_ref, k_ref, v_ref, o_ref = refs
        k_parts = (lambda sl: k_ref[0, :, sl],)
        v_parts = (lambda sl: v_ref[0, :, sl],)
    lane = lax.broadcasted_iota(jnp.int32, (1, LANES), 1)
    low = lane < HEAD_DIM
    groups_per_slab = LANES // HEAD_DIM
    rep = N_HEADS // N_KV_HEADS
    for g in range(N_KV_HEADS):
        slab = g // groups_per_slab
        own_low = (g % groups_per_slab) == 0
        keep = low if own_low else jnp.logical_not(low)

        def halves(parts):
            lanes = slice(slab * LANES, (slab + 1) * LANES)
            rows = jnp.concatenate([part(lanes) for part in parts], axis=0)
            same = jnp.where(keep, rows, 0.0)
            swapped = pltpu.roll(same, HEAD_DIM, 1)
            lo, hi = (same, swapped) if own_low else (swapped, same)
            return lo.astype(BF16), hi.astype(BF16)

        k_lo, k_hi = halves(k_parts)
        v_lo, v_hi = halves(v_parts)
        for j in range(g * rep // groups_per_slab, (g + 1) * rep // groups_per_slab):
            qs = q_ref[0, :, j * LANES:(j + 1) * LANES]
            acc = None
            for kk, vv in ((k_lo, v_lo), (k_hi, v_hi)):
                s = lax.dot_general(qs, kk, (((1,), (1,)), ((), ())), preferred_element_type=F32)
                p = jnp.exp(s - jnp.max(s, axis=-1, keepdims=True))
                denom = jnp.sum(p, axis=-1, keepdims=True)
                part = jnp.dot(p.astype(BF16), vv, preferred_element_type=F32) / denom
                acc = part if acc is None else acc + part
            o_ref[:, j * LANES:(j + 1) * LANES] = acc.astype(BF16)


def _attention(q, k, v, n_batch, seq_len, first_token, name, cache=None):
    tq = ATTN_Q_TILE
    per_b = seq_len // tq
    q_tiles = q.reshape(ALL_TOKENS // tq, tq, Q_W)
    first_tile = first_token // tq
    first_seq = first_token // seq_len
    kv_spec = pl.BlockSpec((1, seq_len, KV_W), lambda b, i: (first_seq + b, 0, 0))
    in_specs = [pl.BlockSpec((1, tq, Q_W), lambda b, i: (first_tile + b * per_b + i, 0, 0)), kv_spec, kv_spec]
    args = [q_tiles, k.reshape(ALL_TOKENS // seq_len, seq_len, KV_W), v.reshape(ALL_TOKENS // seq_len, seq_len, KV_W)]
    if cache is not None:
        cache_k, cache_v, layer = cache
        cache_spec = pl.BlockSpec((1, 1, PAST_LEN, KV_W), lambda b, i: (b, layer, 0, 0))
        in_specs += [cache_spec, cache_spec]
        args += [cache_k, cache_v]
    return pl.pallas_call(
        functools.partial(_attn_kernel, with_cache=cache is not None),
        grid=(n_batch, per_b),
        in_specs=in_specs,
        out_specs=pl.BlockSpec((tq, Q_W), lambda b, i: (b * per_b + i, 0)),
        out_shape=jax.ShapeDtypeStruct((n_batch * seq_len, Q_W), BF16),
        compiler_params=_params(("parallel", "parallel")),
        name=name,
    )(*args)


def _s5_kernel(u_ref, bd_ref, a_ref, cd_ref, h0_ref, dskip_ref, y_ref, hfin_ref, bu_scr, hs_scr, *, chain):
    seq_len = u_ref.shape[0]
    forward = pl.program_id(2) == 0
    a_re = jnp.broadcast_to(a_ref[0, 0, :, :S5_SET_ST], (SUBLANES, S5_SET_ST))
    a_im = jnp.broadcast_to(a_ref[0, 0, :, S5_SET_ST:], (SUBLANES, S5_SET_ST))
    u = u_ref[...]

    @pl.when(forward)
    def _():
        y_ref[...] = u * dskip_ref[...]

    u2 = u.reshape(seq_len * SUBLANES, S5_SET_CH).astype(BF16)
    bu_scr[...] = jnp.dot(u2, bd_ref[0, 0], preferred_element_type=F32)

    def scan(init, store):
        def step(t, hc):
            h_re, h_im = hc
            tt = jnp.where(forward, t, seq_len - 1 - t)
            r = pl.multiple_of(tt * SUBLANES, SUBLANES)
            n_re = a_re * h_re - a_im * h_im + bu_scr[pl.ds(r, SUBLANES), :S5_SET_ST]
            n_im = a_re * h_im + a_im * h_re + bu_scr[pl.ds(r, SUBLANES), S5_SET_ST:]
            if store:
                hs_scr[pl.ds(r, SUBLANES), :S5_SET_ST] = n_re
                hs_scr[pl.ds(r, SUBLANES), S5_SET_ST:] = n_im
            return n_re, n_im

        return lax.fori_loop(0, seq_len, step, init, unroll=S5_SCAN_UNROLL)

    h0 = h0_ref[0, 0]
    init = (h0[:, :S5_SET_ST], h0[:, S5_SET_ST:])
    if chain > 1:
        zero = jnp.zeros((SUBLANES, S5_SET_ST), F32)
        end_re, end_im = scan((zero, zero), store=False)
        p_re, p_im = a_re, a_im
        for _ in range(seq_len.bit_length() - 1):
            p_re, p_im = p_re * p_re - p_im * p_im, 2.0 * (p_re * p_im)
        piece = lax.broadcasted_iota(jnp.int32, (SUBLANES, S5_SET_ST), 0) % chain
        entry = piece == jnp.where(forward, 0, chain - 1)

        def from_neighbour(x):
            return jnp.where(forward, pltpu.roll(x, 1, 0), pltpu.roll(x, SUBLANES - 1, 0))

        i_re = jnp.where(entry, init[0], 0.0)
        i_im = jnp.where(entry, init[1], 0.0)
        for _ in range(chain - 1):
            o_re = p_re * i_re - p_im * i_im + end_re
            o_im = p_re * i_im + p_im * i_re + end_im
            i_re = jnp.where(entry, init[0], from_neighbour(o_re))
            i_im = jnp.where(entry, init[1], from_neighbour(o_im))
        init = (i_re, i_im)
    h_re, h_im = scan(init, store=True)
    hfin_ref[0, 0, :, :S5_SET_ST] = h_re
    hfin_ref[0, 0, :, S5_SET_ST:] = h_im
    yc = jnp.dot(hs_scr[...].astype(BF16), cd_ref[0, 0], preferred_element_type=F32)
    y_ref[...] += yc.reshape(seq_len, SUBLANES, S5_SET_CH)


def _s5_branch(u_tm, bd, a_bar, cd, h0, d_skip, chain, name):
    seq_len, nb, _ = u_tm.shape
    assert seq_len & (seq_len - 1) == 0 and SUBLANES % chain == 0
    st2 = 2 * S5_SET_ST
    par = lambda shape: pl.BlockSpec(shape, lambda b, s, d: (d, s, 0, 0))
    state = pl.BlockSpec((1, 1, SUBLANES, st2), lambda b, s, d: (d, s, b, 0))
    seq = pl.BlockSpec((seq_len, SUBLANES, S5_SET_CH), lambda b, s, d: (0, b, s))
    rows = seq_len * SUBLANES
    return pl.pallas_call(
        functools.partial(_s5_kernel, chain=chain),
        grid=(nb // SUBLANES, S5_N_SETS, N_DIRS),
        in_specs=[
            seq,
            par((1, 1, S5_SET_CH, st2)),
            par((1, 1, 1, st2)),
            par((1, 1, st2, S5_SET_CH)),
            state,
            pl.BlockSpec((1, S5_SET_CH), lambda b, s, d: (0, s)),
        ],
        out_specs=[seq, state],
        out_shape=[jax.ShapeDtypeStruct((seq_len, nb, SSM_WIDTH), F32),
                   jax.ShapeDtypeStruct((N_DIRS, S5_N_SETS, nb, st2), F32)],
        scratch_shapes=[pltpu.VMEM((rows, st2), F32), pltpu.VMEM((rows, st2), F32)],
        compiler_params=_params(("parallel", "parallel", "arbitrary")),
        name=name,
    )(u_tm, bd, a_bar, cd, h0, d_skip)


def _s5_operators(l, P):
    a_re, a_im = P['ssm_a_re'][l].astype(F32), P['ssm_a_im'][l].astype(F32)
    dt = jnp.exp(P['ssm_log_dt'][l].astype(F32))[..., None]
    mag = jnp.exp(a_re * dt)
    ab_re, ab_im = mag * jnp.cos(a_im * dt), mag * jnp.sin(a_im * dt)
    den = a_re * a_re + a_im * a_im
    k_re = ((ab_re - 1.0) * a_re + ab_im * a_im) / den
    k_im = (ab_im * a_re - (ab_re - 1.0) * a_im) / den
    b_re, b_im = P['ssm_b_re'][l].astype(F32), P['ssm_b_im'][l].astype(F32)
    bb_re = k_re[..., None] * b_re - k_im[..., None] * b_im
    bb_im = k_re[..., None] * b_im + k_im[..., None] * b_re
    eye = jnp.eye(S5_GROUP_SET, dtype=F32)

    def sets(t):
        return t.reshape(N_DIRS, S5_N_SETS, S5_GROUP_SET, *t.shape[2:])

    def b_operator(part):
        return jnp.einsum('dsgnp,gh->dsgphn', part, eye).reshape(N_DIRS, S5_N_SETS, S5_SET_CH, S5_SET_ST)

    def c_operator(part):
        return jnp.einsum('dsgpn,gh->dshngp', part, eye).reshape(N_DIRS, S5_N_SETS, S5_SET_ST, S5_SET_CH)

    bd = jnp.concatenate([b_operator(sets(bb_re)), b_operator(sets(bb_im))], axis=-1).astype(BF16)
    c_re = sets(P['ssm_c_re'][l].astype(F32))
    c_im = sets(P['ssm_c_im'][l].astype(F32))
    cd = jnp.concatenate([c_operator(c_re), -c_operator(c_im)], axis=-2).astype(BF16)
    row = lambda t: sets(t).reshape(N_DIRS, S5_N_SETS, 1, S5_SET_ST)
    a_rows = jnp.concatenate([row(ab_re), row(ab_im)], axis=-1)
    return bd, a_rows, cd


def _mixer_out_kernel(attn_ctx_ref, attn_dec_ref, y_ref, sgs_ref, sga_ref, x_ref, mod_ref,
                      wab_ref, wglu_ref, wout_ref, o_ref):
    is_ctx = pl.program_id(0) < CTX_TOKENS // ROW_TILE
    attn = jnp.where(is_ctx, attn_ctx_ref[...], attn_dec_ref[...])
    attn_br = jnp.dot(attn, wab_ref[...], preferred_element_type=F32)
    y = y_ref[...]
    gelu = 0.5 * y * (1.0 + jnp.tanh(math.sqrt(2.0 / math.pi) * (y + 0.044715 * (y * y * y))))
    glu = jnp.dot(gelu.astype(BF16), wglu_ref[...], preferred_element_type=F32)
    ssm_br = glu[:, :D_MODEL] * _sigmoid(glu[:, D_MODEL:])
    merged = sgs_ref[...] * ssm_br + sga_ref[...] * attn_br
    out = jnp.dot(merged.astype(BF16), wout_ref[...], preferred_element_type=F32)
    o_ref[...] = x_ref[...] + mod_ref[0][2:3] * out


def _mixer_out(attn_ctx, attn_dec, y, sgs, sga, x_all, mod, w_attn_br, w_glu, w_out):
    tm = ROW_TILE
    n_ctx = CTX_TOKENS // tm
    row = lambda w: pl.BlockSpec((tm, w), lambda i: (i, 0))
    return pl.pallas_call(
        _mixer_out_kernel,
        grid=(ALL_TOKENS // tm,),
        in_specs=[
            pl.BlockSpec((tm, Q_W), lambda i: (jnp.minimum(i, n_ctx - 1), 0)),
            pl.BlockSpec((tm, Q_W), lambda i: (jnp.maximum(i - n_ctx, 0), 0)),
            row(SSM_WIDTH), row(D_MODEL), row(D_MODEL), row(D_MODEL),
            pl.BlockSpec((1, 6, D_MODEL), lambda i: (_mod_row(i, tm), 0, 0)),
            _resident((Q_W, D_MODEL), lambda i: (0, 0)),
            _resident((SSM_WIDTH, 2 * D_MODEL), lambda i: (0, 0)),
            _resident((D_MODEL, D_MODEL), lambda i: (0, 0)),
        ],
        out_specs=row(D_MODEL),
        out_shape=jax.ShapeDtypeStruct((ALL_TOKENS, D_MODEL), F32),
        compiler_params=_params(("parallel",)),
        name="mixer_out",
    )(attn_ctx, attn_dec, y, sgs, sga, x_all, mod, w_attn_br, w_glu, w_out)


ROUTE_IDX_LANE = 0
ROUTE_W_LANE = 2


def _router_kernel(x_ref, mod_ref, g_ref, r_ref, h_ref, route_ref):
    mod = mod_ref[0]
    h = _rms_modulate(x_ref[...], g_ref[...], mod[4:5], mod[3:4])
    h_ref[...] = h
    r = r_ref[...]
    h_hi, r_hi = h.astype(BF16), r.astype(BF16)
    h_lo = (h - h_hi.astype(F32)).astype(BF16)
    r_lo = (r - r_hi.astype(F32)).astype(BF16)
    logits = (jnp.dot(h_hi, r_hi, preferred_element_type=F32) + jnp.dot(h_lo, r_hi, preferred_element_type=F32)
              + jnp.dot(h_hi, r_lo, preferred_element_type=F32))
    lane = lax.broadcasted_iota(jnp.int32, logits.shape, 1).astype(F32)
    logits = jnp.where(lane < N_EXPERTS, logits, -jnp.inf)
    top1 = jnp.max(logits, axis=-1, keepdims=True)
    idx1 = jnp.min(jnp.where(logits == top1, lane, float(LANES)), axis=-1, keepdims=True)
    rest = jnp.where(lane == idx1, -jnp.inf, logits)
    top2 = jnp.max(rest, axis=-1, keepdims=True)
    idx2 = jnp.min(jnp.where(rest == top2, lane, float(LANES)), axis=-1, keepdims=True)
    e2 = jnp.exp(top2 - top1)
    w1 = 1.0 / (1.0 + e2)
    w2 = e2 / (1.0 + e2)
    route = jnp.where(lane == ROUTE_IDX_LANE, idx1, 0.0)
    route = jnp.where(lane == ROUTE_IDX_LANE + 1, idx2, route)
    route = jnp.where(lane == ROUTE_W_LANE, w1, route)
    route_ref[...] = jnp.where(lane == ROUTE_W_LANE + 1, w2, route)


def _prenorm_router(x_all, mod, norm_g, router_pad):
    tm = ROW_TILE
    row = lambda w: pl.BlockSpec((tm, w), lambda i: (i, 0))
    in_specs = [row(D_MODEL),
                pl.BlockSpec((1, 6, D_MODEL), lambda i: (_mod_row(i, tm), 0, 0)),
                _resident((1, D_MODEL), lambda i: (0, 0))]
    return pl.pallas_call(
        _router_kernel, grid=(ALL_TOKENS // tm,),
        in_specs=in_specs + [_resident((D_MODEL, LANES), lambda i: (0, 0))],
        out_specs=[row(D_MODEL), row(LANES)],
        out_shape=[jax.ShapeDtypeStruct((ALL_TOKENS, D_MODEL), F32),
                   jax.ShapeDtypeStruct((ALL_TOKENS, LANES), F32)],
        compiler_params=_params(("parallel",)), name="prenorm_router",
    )(x_all, mod, norm_g, router_pad)


def _swiglu_chains(h, gate_cols, up_cols, down_rows, d_ff):
    out = None
    for c in range(d_ff // FF_CHUNK):
        cols = slice(c * FF_CHUNK, (c + 1) * FF_CHUNK)
        a = jnp.dot(h, gate_cols(cols), preferred_element_type=F32)
        b = jnp.dot(h, up_cols(cols), preferred_element_type=F32)
        inter = ((a * _sigmoid(a)) * b).astype(BF16)
        part = jnp.dot(inter, down_rows(cols), preferred_element_type=F32)
        out = part if out is None else out + part
    return out


def _dense_ffn_kernel(x_ref, mod_ref, g_ref, wg_ref, wu_ref, wd_ref, fg_ref, o_ref, *, final):
    mod = mod_ref[0]
    x = x_ref[...]
    h = _rms_modulate(x, g_ref[...], mod[4:5], mod[3:4]).astype(BF16)
    f = _swiglu_chains(h, lambda c: wg_ref[:, c], lambda c: wu_ref[:, c], lambda c: wd_ref[c, :],
                       wg_ref.shape[-1])
    x = x + mod[5:6] * f
    if final:
        ms = jnp.mean(x * x, axis=-1, keepdims=True)
        x = x * lax.rsqrt(ms + EPS) * fg_ref[...]
    o_ref[...] = x


def _dense_ffn(x_all, mod, norm_g, w_gate, w_up, w_down, final_g, final):
    tm = ROW_TILE
    d_ff = w_gate.shape[-1]
    row = pl.BlockSpec((tm, D_MODEL), lambda i: (i, 0))
    return pl.pallas_call(
        functools.partial(_dense_ffn_kernel, final=final),
        grid=(ALL_TOKENS // tm,),
        in_specs=[row,
                  pl.BlockSpec((1, 6, D_MODEL), lambda i: (_mod_row(i, tm), 0, 0)),
                  _resident((1, D_MODEL), lambda i: (0, 0)),
                  _resident((D_MODEL, d_ff), lambda i: (0, 0)),
                  _resident((D_MODEL, d_ff), lambda i: (0, 0)),
                  _resident((d_ff, D_MODEL), lambda i: (0, 0)),
                  _resident((1, D_MODEL), lambda i: (0, 0))],
        out_specs=row,
        out_shape=jax.ShapeDtypeStruct((ALL_TOKENS, D_MODEL), F32),
        compiler_params=_params(("parallel",)),
        name="ffn_dense",
    )(x_all, mod, norm_g, w_gate, w_up, w_down, final_g)


def _ffn_kernel(tile_ref, expert_ref, valid_ref, h_ref, wg_ref, wu_ref, wd_ref, o_ref):
    s = pl.program_id(0)
    j = pl.program_id(1)

    @pl.when(valid_ref[s] == 1)
    def _():
        contrib = _swiglu_chains(h_ref[...].astype(BF16), lambda c: wg_ref[0, :, c], lambda c: wu_ref[0, :, c],
                                 lambda c: wd_ref[0, c, :], wg_ref.shape[-1])

        @pl.when(j == 0)
        def _():
            o_ref[...] = contrib

        @pl.when(j != 0)
        def _():
            o_ref[...] += contrib

    @pl.when(jnp.logical_and(valid_ref[s] == 0, j == 0))
    def _():
        o_ref[...] = jnp.zeros_like(o_ref)


def _expert_ffn(h, w_gate, w_up, w_down, step_tile, step_expert, step_valid, tm, tf, name):
    n_rows = h.shape[0]
    d_ff = w_gate.shape[-1]
    n_steps = step_tile.shape[0]
    n_ff = d_ff // tf
    ff = lambda s, j, f: jnp.where(f[s] == 1, j, n_ff - 1)
    grid_spec = pltpu.PrefetchScalarGridSpec(
        num_scalar_prefetch=3,
        grid=(n_steps, n_ff),
        in_specs=[
            pl.BlockSpec((tm, D_MODEL), lambda s, j, t, e, f: (t[s], 0)),
            pl.BlockSpec((1, D_MODEL, tf), lambda s, j, t, e, f: (e[s], 0, ff(s, j, f))),
            pl.BlockSpec((1, D_MODEL, tf), lambda s, j, t, e, f: (e[s], 0, ff(s, j, f))),
            pl.BlockSpec((1, tf, D_MODEL), lambda s, j, t, e, f: (e[s], ff(s, j, f), 0)),
        ],
        out_specs=pl.BlockSpec((tm, D_MODEL), lambda s, j, t, e, f: (t[s], 0)),
    )
    return pl.pallas_call(
        _ffn_kernel,
        grid_spec=grid_spec,
        out_shape=jax.ShapeDtypeStruct((n_rows, D_MODEL), F32),
        compiler_params=_params(("arbitrary", "arbitrary")),
        name=name,
    )(step_tile, step_expert, step_valid, h, w_gate, w_up, w_down)


TOP_K = 2
MOE_ROW_TILE = 512
DISPATCH_STEPS = 32
MOE_ROWS = TOP_K * ALL_TOKENS + N_EXPERTS * MOE_ROW_TILE


def _route_plan(route):
    tm = MOE_ROW_TILE
    n_steps = MOE_ROWS // tm
    expert = route[:, ROUTE_IDX_LANE:ROUTE_IDX_LANE + TOP_K].astype(jnp.int32).reshape(-1)
    onehot = (expert[:, None] == jnp.arange(N_EXPERTS, dtype=jnp.int32)[None, :]).astype(jnp.int32)
    running = jnp.cumsum(onehot, axis=0)
    rank = jnp.sum(onehot * (running - 1), axis=1)
    counts = running[-1]
    tiles = (counts + tm - 1) // tm
    tile_end = jnp.cumsum(tiles)
    starts = (tile_end - tiles) * tm
    pos = jnp.sum(onehot * starts[None, :], axis=1) + rank
    last = tile_end[-1] - 1
    step = jnp.arange(n_steps, dtype=jnp.int32)
    step_expert = jnp.sum((jnp.minimum(step, last)[:, None] >= tile_end[None, :]).astype(jnp.int32), axis=1)
    step_valid = (step <= last).astype(jnp.int32)
    return pos.astype(jnp.int32), step, step_expert.astype(jnp.int32), step_valid


def _dispatch_kernel(pos_ref, h_ref, init_ref, wg_ref, wu_ref, wd_ref, xs_ref, wgb_ref, wub_ref, wdb_ref, sem):
    del init_ref
    tm = h_ref.shape[0]
    base = pl.program_id(0) * tm

    def issue(r, carry):
        for k in range(TOP_K):
            dst = pos_ref[TOP_K * (base + r) + k]
            pltpu.make_async_copy(h_ref.at[pl.ds(r, 1)], xs_ref.at[pl.ds(dst, 1)], sem).start()
        return carry

    lax.fori_loop(0, tm, issue, 0)
    wgb_ref[...] = wg_ref[...].astype(BF16)
    wub_ref[...] = wu_ref[...].astype(BF16)
    wdb_ref[...] = wd_ref[...].astype(BF16)
    for k in range(TOP_K):
        pltpu.make_async_copy(h_ref, xs_ref.at[pl.ds(0, tm)], sem).wait()


def _dispatch(h, pos, w_gate, w_up, w_down):
    n_steps = DISPATCH_STEPS
    tm = ALL_TOKENS // n_steps
    wg2 = w_gate.reshape(-1, w_gate.shape[-1])
    wu2 = w_up.reshape(-1, w_up.shape[-1])
    wd2 = w_down.reshape(-1, w_down.shape[-1])
    slab = lambda w: pl.BlockSpec((w.shape[0] // n_steps, w.shape[1]), lambda i, p: (i, 0))
    grid_spec = pltpu.PrefetchScalarGridSpec(
        num_scalar_prefetch=1,
        grid=(n_steps,),
        in_specs=[pl.BlockSpec((tm, D_MODEL), lambda i, p: (i, 0)),
                  pl.BlockSpec(memory_space=pl.ANY),
                  slab(wg2), slab(wu2), slab(wd2)],
        out_specs=[pl.BlockSpec(memory_space=pl.ANY), slab(wg2), slab(wu2), slab(wd2)],
        scratch_shapes=[pltpu.SemaphoreType.DMA(())],
    )
    xs, wgb, wub, wdb = pl.pallas_call(
        _dispatch_kernel,
        grid_spec=grid_spec,
        out_shape=[jax.ShapeDtypeStruct((MOE_ROWS, D_MODEL), F32)]
        + [jax.ShapeDtypeStruct(w.shape, BF16) for w in (wg2, wu2, wd2)],
        input_output_aliases={2: 0},
        compiler_params=_params(("arbitrary",)),
        name="moe_dispatch",
    )(pos, h, jnp.zeros((MOE_ROWS, D_MODEL), F32), wg2, wu2, wd2)
    return xs, wgb.reshape(w_gate.shape), wub.reshape(w_up.shape), wdb.reshape(w_down.shape)


def _combine_kernel(pos_ref, x_ref, route_ref, mod_ref, g_ref, ys_ref, *rest, final):
    out_refs, (y_scr, sems) = rest[:-2], rest[-2:]
    tm = x_ref.shape[0]
    i = pl.program_id(0)
    n = pl.num_programs(0)

    def gather(tile, slot):
        base = tile * tm

        def issue(r, carry):
            for k in range(TOP_K):
                src = pos_ref[TOP_K * (base + r) + k]
                pltpu.make_async_copy(ys_ref.at[pl.ds(src, 1)], y_scr.at[slot, k, pl.ds(r, 1)],
                                      sems.at[slot, k]).start()
            return carry

        lax.fori_loop(0, tm, issue, 0)

    @pl.when(i == 0)
    def _():
        gather(0, 0)

    @pl.when(i + 1 < n)
    def _():
        gather(i + 1, (i + 1) % 2)

    slot = i % 2
    for k in range(TOP_K):
        pltpu.make_async_copy(ys_ref.at[pl.ds(0, tm)], y_scr.at[slot, k], sems.at[slot, k]).wait()
    route = route_ref[...]
    f = (route[:, ROUTE_W_LANE:ROUTE_W_LANE + 1] * y_scr[slot, 0]
         + route[:, ROUTE_W_LANE + 1:ROUTE_W_LANE + 2] * y_scr[slot, 1])
    x = x_ref[...] + mod_ref[0][5:6] * f
    if not final:
        out_refs[0][...] = x
        return
    ms = jnp.mean(x * x, axis=-1, keepdims=True)
    x = x * lax.rsqrt(ms + EPS) * g_ref[...]
    is_ctx = i < CTX_TOKENS // tm
    @pl.when(is_ctx)
    def _():
        out_refs[0][...] = x

    @pl.when(jnp.logical_not(is_ctx))
    def _():
        out_refs[1][...] = x


def _combine(x_all, y_sorted, pos, route, mod, final_g, final):
    tm = ROW_TILE
    n_ctx = CTX_TOKENS // tm
    row = lambda w: pl.BlockSpec((tm, w), lambda i, p: (i, 0))
    if final:
        out_specs = [pl.BlockSpec((tm, D_MODEL), lambda i, p: (jnp.minimum(i, n_ctx - 1), 0)),
                     pl.BlockSpec((tm, D_MODEL), lambda i, p: (jnp.maximum(i - n_ctx, 0), 0))]
        out_shape = [jax.ShapeDtypeStruct((CTX_TOKENS, D_MODEL), F32),
                     jax.ShapeDtypeStruct((DEC_TOKENS, D_MODEL), F32)]
    else:
        out_specs = [row(D_MODEL)]
        out_shape = [jax.ShapeDtypeStruct((ALL_TOKENS, D_MODEL), F32)]
    grid_spec = pltpu.PrefetchScalarGridSpec(
        num_scalar_prefetch=1,
        grid=(ALL_TOKENS // tm,),
        in_specs=[row(D_MODEL), row(LANES),
                  pl.BlockSpec((1, 6, D_MODEL), lambda i, p: (_mod_row(i, tm), 0, 0)),
                  pl.BlockSpec((1, D_MODEL), lambda i, p: (0, 0)),
                  pl.BlockSpec(memory_space=pl.ANY)],
        out_specs=out_specs,
        scratch_shapes=[pltpu.VMEM((2, TOP_K, tm, D_MODEL), F32), pltpu.SemaphoreType.DMA((2, TOP_K))],
    )
    return pl.pallas_call(
        functools.partial(_combine_kernel, final=final),
        grid_spec=grid_spec,
        out_shape=out_shape,
        compiler_params=_params(("arbitrary",)),
        name="moe_combine",
    )(pos, x_all, route, mod, final_g, y_sorted)


def _rope_tables():
    rows = DEC_SEQ // GRID_W
    row = jnp.repeat(jnp.arange(rows, dtype=F32), GRID_W)
    col = jnp.tile(jnp.arange(GRID_W, dtype=F32), rows)
    inv = ROPE_THETA ** (-jnp.arange(0, AXIS_DIM, 2, dtype=F32) / AXIS_DIM)
    ang = jnp.concatenate([row[:, None] * inv, col[:, None] * inv], axis=-1)
    cos, sin = jnp.cos(ang), jnp.sin(ang)
    reps = LANES // HEAD_DIM
    cos_l = jnp.tile(jnp.concatenate([cos, cos], axis=-1), (1, reps))
    sin_l = jnp.tile(jnp.concatenate([-sin, sin], axis=-1), (1, reps))
    ident = (jnp.ones((ROW_TILE, LANES), F32), jnp.zeros((ROW_TILE, LANES), F32))
    return jnp.concatenate([ident[0], cos_l], axis=0), jnp.concatenate([ident[1], sin_l], axis=0)


def kernel(x_prompt, x_sample, cache_k, cache_v, state_ssm, c, c_ctx, w_ada, b_ada, norm1_g, norm2_g, w_in, q_norm_g, k_norm_g, ssm_a_re, ssm_a_im, ssm_log_dt, ssm_b_re, ssm_b_im, ssm_c_re, ssm_c_im, ssm_d, w_glu, w_attn_br, w_out, ffn_w_gate, ffn_w_up, ffn_w_down, moe_router, moe_w_gate, moe_w_up, moe_w_down, final_norm_g):
    P = dict(ssm_a_re=ssm_a_re, ssm_a_im=ssm_a_im, ssm_log_dt=ssm_log_dt, ssm_b_re=ssm_b_re,
             ssm_b_im=ssm_b_im, ssm_c_re=ssm_c_re, ssm_c_im=ssm_c_im)

    c_rows = jnp.zeros((SUBLANES, D_MODEL), F32).at[0].set(c_ctx).at[1:1 + DEC_BATCH].set(c)
    mod_all = _ada_modulation(c_rows, w_ada, b_ada)
    mod_all = mod_all[:, :N_MOD_ROWS].reshape(DEPTH, N_MOD_ROWS, 6, D_MODEL)

    rope_cos, rope_sin = _rope_tables()
    head_id = jnp.arange(MXU_WIDTH, dtype=jnp.int32) // HEAD_DIM
    ones_bd = (head_id[:, None] == head_id[None, :]).astype(BF16)

    x_all = jnp.concatenate([x_prompt.reshape(CTX_TOKENS, D_MODEL), x_sample.reshape(DEC_TOKENS, D_MODEL)], axis=0)
    cache_k4 = cache_k.reshape(DEC_BATCH, DEPTH, PAST_LEN, KV_W)
    cache_v4 = cache_v.reshape(DEC_BATCH, DEPTH, PAST_LEN, KV_W)
    chain = DEC_SEQ // S5_PIECE
    dec_pieces = DEC_BATCH * chain
    ks, vs, ss = [], [], []

    for l in range(DEPTH):
        mod = mod_all[l]
        u, q, k, v, sgs, sga = _in_projection(
            x_all, mod, norm1_g[l][None, :], w_in[l].astype(BF16), ones_bd,
            jnp.tile(q_norm_g[l], N_HEADS)[None, :], jnp.tile(k_norm_g[l], N_KV_HEADS)[None, :],
            rope_cos, rope_sin)

        attn_ctx = _attention(q, k, v, BATCH, SEQ, 0, "attention_context")
        attn_dec = _attention(q, k, v, DEC_BATCH, DEC_SEQ, CTX_TOKENS, "attention_latent",
                              cache=(cache_k4, cache_v4, l))
        ks.append(k[:CTX_TOKENS].reshape(BATCH, SEQ, N_KV_HEADS, HEAD_DIM))
        vs.append(v[:CTX_TOKENS].reshape(BATCH, SEQ, N_KV_HEADS, HEAD_DIM))

        bd, a_rows, cd = _s5_operators(l, P)
        d_skip = ssm_d[l][None, :]
        u_ctx = u[:CTX_TOKENS].reshape(BATCH, SEQ, SSM_WIDTH).transpose(1, 0, 2)
        h0_ctx = jnp.zeros((N_DIRS, S5_N_SETS, BATCH, 2 * S5_SET_ST), F32)
        y_ctx, fin = _s5_branch(u_ctx, bd, a_rows, cd, h0_ctx, d_skip, 1, "s5_context")
        u_dec = u[CTX_TOKENS:].reshape(dec_pieces, S5_PIECE, SSM_WIDTH).transpose(1, 0, 2)
        h0_dec = state_ssm[:, l].reshape(DEC_BATCH, N_DIRS, S5_N_SETS, S5_GROUP_SET * STATE_N, 2)
        h0_dec = h0_dec.transpose(1, 2, 0, 4, 3).reshape(N_DIRS, S5_N_SETS, DEC_BATCH, 2 * S5_SET_ST)
        h0_dec = jnp.repeat(h0_dec, chain, axis=2)
        y_dec, _ = _s5_branch(u_dec, bd, a_rows, cd, h0_dec, d_skip, chain, "s5_latent")
        y = jnp.concatenate([y_ctx.transpose(1, 0, 2).reshape(CTX_TOKENS, SSM_WIDTH),
                             y_dec.transpose(1, 0, 2).reshape(DEC_TOKENS, SSM_WIDTH)], axis=0)
        fin = fin.reshape(N_DIRS, S5_N_SETS, BATCH, 2, S5_GROUP_SET, STATE_N)
        ss.append(fin.transpose(2, 0, 1, 4, 5, 3).reshape(BATCH, N_DIRS, N_GROUPS, STATE_N, 2))

        x_all = _mixer_out(attn_ctx, attn_dec, y, sgs, sga, x_all, mod, w_attn_br[l].astype(BF16),
                           w_glu[l].astype(BF16), w_out[l].astype(BF16))

        i = l // 2
        last_layer = l == DEPTH - 1
        if l % 2 == 0:
            out = _dense_ffn(x_all, mod, norm2_g[l][None, :], ffn_w_gate[i].astype(BF16), ffn_w_up[i].astype(BF16),
                             ffn_w_down[i].astype(BF16), final_norm_g[None, :], last_layer)
            x_all = out
            streams = (out[:CTX_TOKENS], out[CTX_TOKENS:])
        else:
            router_pad = jnp.pad(moe_router[i].astype(F32), ((0, 0), (0, LANES - N_EXPERTS)))
            h2, route = _prenorm_router(x_all, mod, norm2_g[l][None, :], router_pad)
            pos, tile, expert, valid = _route_plan(route)
            h_sorted, wg, wu, wd = _dispatch(h2, pos, moe_w_gate[i], moe_w_up[i], moe_w_down[i])
            y_sorted = _expert_ffn(h_sorted, wg, wu, wd, tile, expert, valid, MOE_ROW_TILE, D_FF_EXPERT // 2,
                                   "ffn_experts")
            out = _combine(x_all, y_sorted, pos, route, mod, final_norm_g[None, :], last_layer)
            if last_layer:
                streams = tuple(out)
            else:
                x_all = out[0]
                streams = (x_all[:CTX_TOKENS], x_all[CTX_TOKENS:])

    y_prompt = streams[0].reshape(BATCH, SEQ, D_MODEL)
    y_sample = streams[1].reshape(DEC_BATCH, DEC_SEQ, D_MODEL)
    return (y_prompt, y_sample, jnp.stack(ks, axis=1), jnp.stack(vs, axis=1), jnp.stack(ss, axis=1))
```

```python
import functools
import math

import jax
import jax.numpy as jnp
from jax import lax
from jax.experimental import pallas as pl
from jax.experimental.pallas import tpu as pltpu

F32 = jnp.float32
BF16 = jnp.bfloat16

D_MODEL = 1024
BATCH = 32
SEQ = 256
DEPTH = 2
DEC_BATCH = 2
DEC_SEQ = 1024
PAST_LEN = 256
GRID_W = 64
HEAD_DIM = 64
N_HEADS = 16
N_KV_HEADS = 4
ROPE_THETA = 10000.0
AXIS_DIM = HEAD_DIM // 2
HALF = HEAD_DIM // 2
SSM_GROUP = 16
SSM_WIDTH = 512
N_GROUPS = SSM_WIDTH // SSM_GROUP
STATE_N = 64
N_DIRS = 2
D_FF = 2816
N_EXPERTS = 8
D_FF_EXPERT = 3584
EPS = 1e-6
Q_W = N_HEADS * HEAD_DIM
KV_W = N_KV_HEADS * HEAD_DIM
IN_W = SSM_WIDTH + Q_W + 2 * KV_W + 2 * D_MODEL

CTX_TOKENS = BATCH * SEQ
DEC_TOKENS = DEC_BATCH * DEC_SEQ
ALL_TOKENS = CTX_TOKENS + DEC_TOKENS

LANES = 128
MXU_WIDTH = 256
FF_CHUNK = MXU_WIDTH
SUBLANES = 8
VMEM_LIMIT_BYTES = 56 * 1024 * 1024

ROW_TILE = 512
ATTN_Q_TILE = 256
S5_GROUP_SET = 8
S5_SET_CH = S5_GROUP_SET * SSM_GROUP
S5_SET_ST = S5_GROUP_SET * STATE_N
S5_N_SETS = N_GROUPS // S5_GROUP_SET
S5_SCAN_UNROLL = 4
S5_PIECE = SEQ
N_MOD_ROWS = 1 + DEC_BATCH


def _params(sem):
    return pltpu.CompilerParams(dimension_semantics=sem, vmem_limit_bytes=VMEM_LIMIT_BYTES)


def _resident(shape, index_map):
    return pl.BlockSpec(shape, index_map, pipeline_mode=pl.Buffered(1))


def _sigmoid(x):
    return 1.0 / (1.0 + jnp.exp(-x))


def _mod_row(i, tm):
    nctx = CTX_TOKENS // tm
    per_b = DEC_SEQ // tm
    return jnp.where(i < nctx, 0, 1 + (i - nctx) // per_b)


def _rope_block(i, tm):
    nctx = CTX_TOKENS // tm
    per_b = DEC_SEQ // tm
    return jnp.maximum(i - nctx, 0) % per_b


N_CTX_TILES = CTX_TOKENS // ROW_TILE


def _stream_specs(width, dec_first_block=0):
    return (pl.BlockSpec((ROW_TILE, width), lambda i, *_: (jnp.minimum(i, N_CTX_TILES - 1), 0)),
            pl.BlockSpec((ROW_TILE, width), lambda i, *_: (dec_first_block + jnp.maximum(i - N_CTX_TILES, 0), 0)))


def _is_ctx_tile():
    return pl.program_id(0) < N_CTX_TILES


def _pick_stream(ctx_ref, dec_ref):
    return jnp.where(_is_ctx_tile(), ctx_ref[...], dec_ref[...])


def _rms_modulate(x, g, scale, shift):
    ms = jnp.mean(x * x, axis=-1, keepdims=True)
    return (x * lax.rsqrt(ms + EPS) * g) * (1.0 + scale) + shift


def _ada_kernel(c_ref, w_ref, b_ref, o_ref):
    c = c_ref[...]
    act = c * _sigmoid(c)
    o_ref[0] = jnp.dot(act.astype(BF16), w_ref[0].astype(BF16), preferred_element_type=F32) + b_ref[0]


def _ada_modulation(c_rows, w_ada, b_ada):
    tn = 1536
    return pl.pallas_call(
        _ada_kernel,
        grid=(DEPTH, 6 * D_MODEL // tn),
        in_specs=[
            pl.BlockSpec((SUBLANES, D_MODEL), lambda l, n: (0, 0)),
            pl.BlockSpec((1, D_MODEL, tn), lambda l, n: (l, 0, n)),
            pl.BlockSpec((1, 1, tn), lambda l, n: (l, 0, n)),
        ],
        out_specs=pl.BlockSpec((1, SUBLANES, tn), lambda l, n: (l, 0, n)),
        out_shape=jax.ShapeDtypeStruct((DEPTH, SUBLANES, 6 * D_MODEL), F32),
        compiler_params=_params(("parallel", "parallel")),
        name="ada_modulation",
    )(c_rows, w_ada, b_ada.reshape(DEPTH, 1, 6 * D_MODEL))


def _rotate_pairs(x, cos, sin_signed):
    n = x.shape[-1]
    lane = lax.broadcasted_iota(jnp.int32, x.shape, 1)
    first_half = (lane % HEAD_DIM) < HALF
    partner = jnp.where(first_half, pltpu.roll(x, n - HALF, 1), pltpu.roll(x, HALF, 1))
    reps = n // LANES
    return x * jnp.tile(cos, (1, reps)) + partner * jnp.tile(sin_signed, (1, reps))


def _inproj_kernel(x_ctx_ref, x_dec_ref, mod_ref, g_ref, w_ref, ones_ref, qg_ref, kg_ref, cos_ref, sin_ref,
                   u_ref, q_ref, kc_ref, vc_ref, kd_ref, vd_ref, sgs_ref, sga_ref, *x_copy_ref):
    mod = mod_ref[0]
    x = _pick_stream(x_ctx_ref, x_dec_ref)
    if x_copy_ref:
        x_copy_ref[0][...] = x
    h = _rms_modulate(x, g_ref[...], mod[1:2], mod[0:1]).astype(BF16)

    def proj(lo, width):
        return jnp.dot(h, w_ref[:, lo:lo + width], preferred_element_type=F32)

    def head_norm(t, gain):
        sq = (t * t).astype(BF16)
        ss = jnp.concatenate(
            [jnp.dot(sq[:, c:c + MXU_WIDTH], ones_ref[...], preferred_element_type=F32)
             for c in range(0, t.shape[-1], MXU_WIDTH)], axis=-1)
        return t * lax.rsqrt(ss * (1.0 / HEAD_DIM) + EPS) * gain

    o = 0
    u_ref[...] = proj(o, SSM_WIDTH)
    o += SSM_WIDTH
    q = head_norm(proj(o, Q_W), qg_ref[...]) * (1.0 / math.sqrt(HEAD_DIM))
    o += Q_W
    k = head_norm(proj(o, KV_W), kg_ref[...])
    o += KV_W
    v = proj(o, KV_W)
    o += KV_W
    sgs_ref[...] = _sigmoid(proj(o, D_MODEL))
    o += D_MODEL
    sga_ref[...] = _sigmoid(proj(o, D_MODEL))

    @pl.when(_is_ctx_tile())
    def _():
        q_ref[...] = q.astype(BF16)
        kc_ref[...] = k
        vc_ref[...] = v

    @pl.when(jnp.logical_not(_is_ctx_tile()))
    def _():
        cos = cos_ref[...]
        sin = sin_ref[...]
        q_ref[...] = _rotate_pairs(q, cos, sin).astype(BF16)
        kd_ref[...] = _rotate_pairs(k, cos, sin)
        vd_ref[...] = v


def _in_projection(x_ctx, x_dec, dec_first_block, mod, norm_g, w_in_bf16, ones_bd, q_gain, k_gain,
                   rope_cos, rope_sin, copy_x):
    tm = ROW_TILE
    n_tiles = ALL_TOKENS // tm
    row = lambda w: pl.BlockSpec((tm, w), lambda i: (i, 0))
    kv_ctx, kv_dec = _stream_specs(KV_W)
    out_specs = [row(SSM_WIDTH), row(Q_W), kv_ctx, kv_ctx, kv_dec, kv_dec, row(D_MODEL), row(D_MODEL)]
    out_shape = [jax.ShapeDtypeStruct((ALL_TOKENS, SSM_WIDTH), F32), jax.ShapeDtypeStruct((ALL_TOKENS, Q_W), BF16),
                 jax.ShapeDtypeStruct((CTX_TOKENS, KV_W), F32), jax.ShapeDtypeStruct((CTX_TOKENS, KV_W), F32),
                 jax.ShapeDtypeStruct((DEC_TOKENS, KV_W), F32), jax.ShapeDtypeStruct((DEC_TOKENS, KV_W), F32),
                 jax.ShapeDtypeStruct((ALL_TOKENS, D_MODEL), F32), jax.ShapeDtypeStruct((ALL_TOKENS, D_MODEL), F32)]
    if copy_x:
        out_specs.append(row(D_MODEL))
        out_shape.append(jax.ShapeDtypeStruct((ALL_TOKENS, D_MODEL), F32))
    return pl.pallas_call(
        _inproj_kernel,
        grid=(n_tiles,),
        in_specs=[
            *_stream_specs(D_MODEL, dec_first_block),
            pl.BlockSpec((1, 6, D_MODEL), lambda i: (_mod_row(i, tm), 0, 0)),
            _resident((1, D_MODEL), lambda i: (0, 0)),
            _resident((D_MODEL, IN_W), lambda i: (0, 0)),
            _resident((MXU_WIDTH, MXU_WIDTH), lambda i: (0, 0)),
            _resident((1, Q_W), lambda i: (0, 0)),
            _resident((1, KV_W), lambda i: (0, 0)),
            pl.BlockSpec((tm, LANES), lambda i: (_rope_block(i, tm), 0)),
            pl.BlockSpec((tm, LANES), lambda i: (_rope_block(i, tm), 0)),
        ],
        out_specs=out_specs,
        out_shape=out_shape,
        compiler_params=_params(("arbitrary",)),
        name="in_projection",
    )(x_ctx, x_dec, mod, norm_g, w_in_bf16, ones_bd, q_gain, k_gain, rope_cos, rope_sin)


def _attn_kernel(*refs, with_cache):
    if with_cache:
        q_ref, k_ref, v_ref, ck_ref, cv_ref, o_ref = refs
        k_parts = (lambda sl: ck_ref[0, 0, :, sl], lambda sl: k_ref[0, :, sl])
        v_parts = (lambda sl: cv_ref[0, 0, :, sl], lambda sl: v_ref[0, :, sl])
    else:
        q_ref, k_ref, v_ref, o_ref = refs
        k_parts = (lambda sl: k_ref[0, :, sl],)
        v_parts = (lambda sl: v_ref[0, :, sl],)
    lane = lax.broadcasted_iota(jnp.int32, (1, LANES), 1)
    low = lane < HEAD_DIM
    groups_per_slab = LANES // HEAD_DIM
    rep = N_HEADS // N_KV_HEADS
    for g in range(N_KV_HEADS):
        slab = g // groups_per_slab
        own_low = (g % groups_per_slab) == 0
        keep = low if own_low else jnp.logical_not(low)

        def halves(parts):
            lanes = slice(slab * LANES, (slab + 1) * LANES)
            rows = jnp.concatenate([part(lanes) for part in parts], axis=0)
            same = jnp.where(keep, rows, 0.0)
            swapped = pltpu.roll(same, HEAD_DIM, 1)
            lo, hi = (same, swapped) if own_low else (swapped, same)
            return lo.astype(BF16), hi.astype(BF16)

        k_lo, k_hi = halves(k_parts)
        v_lo, v_hi = halves(v_parts)
        for j in range(g * rep // groups_per_slab, (g + 1) * rep // groups_per_slab):
            qs = q_ref[0, :, j * LANES:(j + 1) * LANES]
            acc = None
            for kk, vv in ((k_lo, v_lo), (k_hi, v_hi)):
                s = lax.dot_general(qs, kk, (((1,), (1,)), ((), ())), preferred_element_type=F32)
                p = jnp.exp(s - jnp.max(s, axis=-1, keepdims=True))
                denom = jnp.sum(p, axis=-1, keepdims=True)
                part = jnp.dot(p.astype(BF16), vv, preferred_element_type=F32) / denom
                acc = part if acc is None else acc + part
            o_ref[:, j * LANES:(j + 1) * LANES] = acc.astype(BF16)


def _attention(q, k, v, n_batch, seq_len, first_token, name, cache=None):
    tq = ATTN_Q_TILE
    per_b = seq_len // tq
    q_tiles = q.reshape(ALL_TOKENS // tq, tq, Q_W)
    first_tile = first_token // tq
    kv_spec = pl.BlockSpec((1, seq_len, KV_W), lambda b, i: (b, 0, 0))
    in_specs = [pl.BlockSpec((1, tq, Q_W), lambda b, i: (first_tile + b * per_b + i, 0, 0)), kv_spec, kv_spec]
    args = [q_tiles, k.reshape(n_batch, seq_len, KV_W), v.reshape(n_batch, seq_len, KV_W)]
    if cache is not None:
        cache_k, cache_v, layer = cache
        cache_spec = pl.BlockSpec((1, 1, PAST_LEN, KV_W), lambda b, i: (b, layer, 0, 0))
        in_specs += [cache_spec, cache_spec]
        args += [cache_k, cache_v]
    return pl.pallas_call(
        functools.partial(_attn_kernel, with_cache=cache is not None),
        grid=(n_batch, per_b),
        in_specs=in_specs,
        out_specs=pl.BlockSpec((tq, Q_W), lambda b, i: (b * per_b + i, 0)),
        out_shape=jax.ShapeDtypeStruct((n_batch * seq_len, Q_W), BF16),
        compiler_params=_params(("parallel", "parallel")),
        name=name,
    )(*args)


def _s5_kernel(u_ref, bd_ref, a_ref, cd_ref, h0_ref, dskip_ref, y_ref, hfin_ref,
               u_scr, y_scr, bu_scr, hs_scr, *, chain):
    seq_len = u_ref.shape[0] // SUBLANES
    forward = pl.program_id(2) == 0
    last_dir = pl.program_id(2) == pl.num_programs(2) - 1
    a_re = jnp.broadcast_to(a_ref[0, 0, :, :S5_SET_ST], (SUBLANES, S5_SET_ST))
    a_im = jnp.broadcast_to(a_ref[0, 0, :, S5_SET_ST:], (SUBLANES, S5_SET_ST))

    @pl.when(forward)
    def _():
        for p in range(SUBLANES):
            u_scr[pl.ds(p, seq_len, stride=SUBLANES), :] = u_ref[p * seq_len:(p + 1) * seq_len, :]
        y_scr[...] = u_scr[...] * dskip_ref[...]

    bu_scr[...] = jnp.dot(u_scr[...].astype(BF16), bd_ref[0, 0], preferred_element_type=F32)

    def scan(init, store):
        def step(t, hc):
            h_re, h_im = hc
            tt = jnp.where(forward, t, seq_len - 1 - t)
            r = pl.multiple_of(tt * SUBLANES, SUBLANES)
            n_re = a_re * h_re - a_im * h_im + bu_scr[pl.ds(r, SUBLANES), :S5_SET_ST]
            n_im = a_re * h_im + a_im * h_re + bu_scr[pl.ds(r, SUBLANES), S5_SET_ST:]
            if store:
                hs_scr[pl.ds(r, SUBLANES), :S5_SET_ST] = n_re
                hs_scr[pl.ds(r, SUBLANES), S5_SET_ST:] = n_im
            return n_re, n_im

        return lax.fori_loop(0, seq_len, step, init, unroll=S5_SCAN_UNROLL)

    h0 = h0_ref[0, 0]
    init = (h0[:, :S5_SET_ST], h0[:, S5_SET_ST:])
    if chain > 1:
        zero = jnp.zeros((SUBLANES, S5_SET_ST), F32)
        end_re, end_im = scan((zero, zero), store=False)
        p_re, p_im = a_re, a_im
        for _ in range(seq_len.bit_length() - 1):
            p_re, p_im = p_re * p_re - p_im * p_im, 2.0 * (p_re * p_im)
        piece = lax.broadcasted_iota(jnp.int32, (SUBLANES, S5_SET_ST), 0) % chain
        entry = piece == jnp.where(forward, 0, chain - 1)

        def from_neighbour(x):
            return jnp.where(forward, pltpu.roll(x, 1, 0), pltpu.roll(x, SUBLANES - 1, 0))

        i_re = jnp.where(entry, init[0], 0.0)
        i_im = jnp.where(entry, init[1], 0.0)
        for _ in range(chain - 1):
            o_re = p_re * i_re - p_im * i_im + end_re
            o_im = p_re * i_im + p_im * i_re + end_im
            i_re = jnp.where(entry, init[0], from_neighbour(o_re))
            i_im = jnp.where(entry, init[1], from_neighbour(o_im))
        init = (i_re, i_im)
    h_re, h_im = scan(init, store=True)
    hfin_ref[0, 0, :, :S5_SET_ST] = h_re
    hfin_ref[0, 0, :, S5_SET_ST:] = h_im
    y_scr[...] += jnp.dot(hs_scr[...].astype(BF16), cd_ref[0, 0], preferred_element_type=F32)

    @pl.when(last_dir)
    def _():
        for p in range(SUBLANES):
            y_ref[p * seq_len:(p + 1) * seq_len, :] = y_scr[pl.ds(p, seq_len, stride=SUBLANES), :]


def _s5_branch(u_all, first_token, n_tokens, bd, a_bar, cd, h0, d_skip, chain, name):
    seq_len = S5_PIECE
    rows = seq_len * SUBLANES
    nb = n_tokens // seq_len
    first_block = first_token // rows
    assert seq_len & (seq_len - 1) == 0 and SUBLANES % chain == 0 and first_token % rows == 0
    st2 = 2 * S5_SET_ST
    par = lambda shape: pl.BlockSpec(shape, lambda b, s, d: (d, s, 0, 0))
    state = pl.BlockSpec((1, 1, SUBLANES, st2), lambda b, s, d: (d, s, b, 0))
    seq = pl.BlockSpec((rows, S5_SET_CH), lambda b, s, d: (b, s))
    return pl.pallas_call(
        functools.partial(_s5_kernel, chain=chain),
        grid=(nb // SUBLANES, S5_N_SETS, N_DIRS),
        in_specs=[
            pl.BlockSpec((rows, S5_SET_CH), lambda b, s, d: (first_block + b, s)),
            par((1, 1, S5_SET_CH, st2)),
            par((1, 1, 1, st2)),
            par((1, 1, st2, S5_SET_CH)),
            state,
            pl.BlockSpec((1, S5_SET_CH), lambda b, s, d: (0, s)),
        ],
        out_specs=[seq, state],
        out_shape=[jax.ShapeDtypeStruct((n_tokens, SSM_WIDTH), F32),
                   jax.ShapeDtypeStruct((N_DIRS, S5_N_SETS, nb, st2), F32)],
        scratch_shapes=[pltpu.VMEM((rows, S5_SET_CH), F32), pltpu.VMEM((rows, S5_SET_CH), F32),
                        pltpu.VMEM((rows, st2), F32), pltpu.VMEM((rows, st2), F32)],
        compiler_params=_params(("parallel", "parallel", "arbitrary")),
        name=name,
    )(u_all, bd, a_bar, cd, h0, d_skip)


def _s5_operators(l, P):
    a_re, a_im = P['ssm_a_re'][l].astype(F32), P['ssm_a_im'][l].astype(F32)
    dt = jnp.exp(P['ssm_log_dt'][l].astype(F32))[..., None]
    mag = jnp.exp(a_re * dt)
    ab_re, ab_im = mag * jnp.cos(a_im * dt), mag * jnp.sin(a_im * dt)
    den = a_re * a_re + a_im * a_im
    k_re = ((ab_re - 1.0) * a_re + ab_im * a_im) / den
    k_im = (ab_im * a_re - (ab_re - 1.0) * a_im) / den
    b_re, b_im = P['ssm_b_re'][l].astype(F32), P['ssm_b_im'][l].astype(F32)
    bb_re = k_re[..., None] * b_re - k_im[..., None] * b_im
    bb_im = k_re[..., None] * b_im + k_im[..., None] * b_re
    eye = jnp.eye(S5_GROUP_SET, dtype=F32)

    def sets(t):
        return t.reshape(N_DIRS, S5_N_SETS, S5_GROUP_SET, *t.shape[2:])

    def b_operator(part):
        return jnp.einsum('dsgnp,gh->dsgphn', part, eye).reshape(N_DIRS, S5_N_SETS, S5_SET_CH, S5_SET_ST)

    def c_operator(part):
        return jnp.einsum('dsgpn,gh->dshngp', part, eye).reshape(N_DIRS, S5_N_SETS, S5_SET_ST, S5_SET_CH)

    bd = jnp.concatenate([b_operator(sets(bb_re)), b_operator(sets(bb_im))], axis=-1).astype(BF16)
    c_re = sets(P['ssm_c_re'][l].astype(F32))
    c_im = sets(P['ssm_c_im'][l].astype(F32))
    cd = jnp.concatenate([c_operator(c_re), -c_operator(c_im)], axis=-2).astype(BF16)
    row = lambda t: sets(t).reshape(N_DIRS, S5_N_SETS, 1, S5_SET_ST)
    a_rows = jnp.concatenate([row(ab_re), row(ab_im)], axis=-1)
    return bd, a_rows, cd


def _mixer_out_kernel(attn_ctx_ref, attn_dec_ref, y_ctx_ref, y_dec_ref, sgs_ref, sga_ref, x_ref, mod_ref,
                      wab_ref, wglu_ref, wout_ref, o_ref):
    attn_br = jnp.dot(_pick_stream(attn_ctx_ref, attn_dec_ref), wab_ref[...], preferred_element_type=F32)
    y = _pick_stream(y_ctx_ref, y_dec_ref)
    gelu = 0.5 * y * (1.0 + jnp.tanh(math.sqrt(2.0 / math.pi) * (y + 0.044715 * (y * y * y))))
    glu = jnp.dot(gelu.astype(BF16), wglu_ref[...], preferred_element_type=F32)
    ssm_br = glu[:, :D_MODEL] * _sigmoid(glu[:, D_MODEL:])
    merged = sgs_ref[...] * ssm_br + sga_ref[...] * attn_br
    out = jnp.dot(merged.astype(BF16), wout_ref[...], preferred_element_type=F32)
    o_ref[...] = x_ref[...] + mod_ref[0][2:3] * out


def _mixer_out(attn_ctx, attn_dec, y_ctx, y_dec, sgs, sga, x_all, mod, w_attn_br, w_glu, w_out):
    tm = ROW_TILE
    row = lambda w: pl.BlockSpec((tm, w), lambda i: (i, 0))
    return pl.pallas_call(
        _mixer_out_kernel,
        grid=(ALL_TOKENS // tm,),
        in_specs=[
            *_stream_specs(Q_W), *_stream_specs(SSM_WIDTH),
            row(D_MODEL), row(D_MODEL), row(D_MODEL),
            pl.BlockSpec((1, 6, D_MODEL), lambda i: (_mod_row(i, tm), 0, 0)),
            _resident((Q_W, D_MODEL), lambda i: (0, 0)),
            _resident((SSM_WIDTH, 2 * D_MODEL), lambda i: (0, 0)),
            _resident((D_MODEL, D_MODEL), lambda i: (0, 0)),
        ],
        out_specs=row(D_MODEL),
        out_shape=jax.ShapeDtypeStruct((ALL_TOKENS, D_MODEL), F32),
        compiler_params=_params(("parallel",)),
        name="mixer_out",
    )(attn_ctx, attn_dec, y_ctx, y_dec, sgs, sga, x_all, mod, w_attn_br, w_glu, w_out)


ROUTE_IDX_LANE = 0
ROUTE_W_LANE = 2


def _router_kernel(x_ref, mod_ref, g_ref, r_ref, h_ref, route_ref):
    mod = mod_ref[0]
    h = _rms_modulate(x_ref[...], g_ref[...], mod[4:5], mod[3:4])
    h_ref[...] = h
    r = r_ref[...]
    h_hi, r_hi = h.astype(BF16), r.astype(BF16)
    h_lo = (h - h_hi.astype(F32)).astype(BF16)
    r_lo = (r - r_hi.astype(F32)).astype(BF16)
    logits = (jnp.dot(h_hi, r_hi, preferred_element_type=F32) + jnp.dot(h_lo, r_hi, preferred_element_type=F32)
              + jnp.dot(h_hi, r_lo, preferred_element_type=F32))
    lane = lax.broadcasted_iota(jnp.int32, logits.shape, 1).astype(F32)
    logits = jnp.where(lane < N_EXPERTS, logits, -jnp.inf)
    top1 = jnp.max(logits, axis=-1, keepdims=True)
    idx1 = jnp.min(jnp.where(logits == top1, lane, float(LANES)), axis=-1, keepdims=True)
    rest = jnp.where(lane == idx1, -jnp.inf, logits)
    top2 = jnp.max(rest, axis=-1, keepdims=True)
    idx2 = jnp.min(jnp.where(rest == top2, lane, float(LANES)), axis=-1, keepdims=True)
    e2 = jnp.exp(top2 - top1)
    w1 = 1.0 / (1.0 + e2)
    w2 = e2 / (1.0 + e2)
    route = jnp.where(lane == ROUTE_IDX_LANE, idx1, 0.0)
    route = jnp.where(lane == ROUTE_IDX_LANE + 1, idx2, route)
    route = jnp.where(lane == ROUTE_W_LANE, w1, route)
    route_ref[...] = jnp.where(lane == ROUTE_W_LANE + 1, w2, route)


def _prenorm_router(x_all, mod, norm_g, router_pad):
    tm = ROW_TILE
    row = lambda w: pl.BlockSpec((tm, w), lambda i: (i, 0))
    in_specs = [row(D_MODEL),
                pl.BlockSpec((1, 6, D_MODEL), lambda i: (_mod_row(i, tm), 0, 0)),
                _resident((1, D_MODEL), lambda i: (0, 0))]
    return pl.pallas_call(
        _router_kernel, grid=(ALL_TOKENS // tm,),
        in_specs=in_specs + [_resident((D_MODEL, LANES), lambda i: (0, 0))],
        out_specs=[row(D_MODEL), row(LANES)],
        out_shape=[jax.ShapeDtypeStruct((ALL_TOKENS, D_MODEL), F32),
                   jax.ShapeDtypeStruct((ALL_TOKENS, LANES), F32)],
        compiler_params=_params(("parallel",)), name="prenorm_router",
    )(x_all, mod, norm_g, router_pad)


def _swiglu_chains(h, gate_cols, up_cols, down_rows, d_ff):
    out = None
    for c in range(d_ff // FF_CHUNK):
        cols = slice(c * FF_CHUNK, (c + 1) * FF_CHUNK)
        a = jnp.dot(h, gate_cols(cols), preferred_element_type=F32)
        b = jnp.dot(h, up_cols(cols), preferred_element_type=F32)
        inter = ((a * _sigmoid(a)) * b).astype(BF16)
        part = jnp.dot(inter, down_rows(cols), preferred_element_type=F32)
        out = part if out is None else out + part
    return out


def _dense_ffn_kernel(x_ref, mod_ref, g_ref, wg_ref, wu_ref, wd_ref, fg_ref, o_ref, *, final):
    mod = mod_ref[0]
    x = x_ref[...]
    h = _rms_modulate(x, g_ref[...], mod[4:5], mod[3:4]).astype(BF16)
    f = _swiglu_chains(h, lambda c: wg_ref[:, c], lambda c: wu_ref[:, c], lambda c: wd_ref[c, :],
                       wg_ref.shape[-1])
    x = x + mod[5:6] * f
    if final:
        ms = jnp.mean(x * x, axis=-1, keepdims=True)
        x = x * lax.rsqrt(ms + EPS) * fg_ref[...]
    o_ref[...] = x


def _dense_ffn(x_all, mod, norm_g, w_gate, w_up, w_down, final_g, final):
    tm = ROW_TILE
    d_ff = w_gate.shape[-1]
    row = pl.BlockSpec((tm, D_MODEL), lambda i: (i, 0))
    return pl.pallas_call(
        functools.partial(_dense_ffn_kernel, final=final),
        grid=(ALL_TOKENS // tm,),
        in_specs=[row,
                  pl.BlockSpec((1, 6, D_MODEL), lambda i: (_mod_row(i, tm), 0, 0)),
                  _resident((1, D_MODEL), lambda i: (0, 0)),
                  _resident((D_MODEL, d_ff), lambda i: (0, 0)),
                  _resident((D_MODEL, d_ff), lambda i: (0, 0)),
                  _resident((d_ff, D_MODEL), lambda i: (0, 0)),
                  _resident((1, D_MODEL), lambda i: (0, 0))],
        out_specs=row,
        out_shape=jax.ShapeDtypeStruct((ALL_TOKENS, D_MODEL), F32),
        compiler_params=_params(("parallel",)),
        name="ffn_dense",
    )(x_all, mod, norm_g, w_gate, w_up, w_down, final_g)


def _ffn_kernel(tile_ref, expert_ref, valid_ref, h_ref, wg_ref, wu_ref, wd_ref, o_ref):
    s = pl.program_id(0)
    j = pl.program_id(1)

    @pl.when(valid_ref[s] == 1)
    def _():
        contrib = _swiglu_chains(h_ref[...].astype(BF16), lambda c: wg_ref[0, :, c], lambda c: wu_ref[0, :, c],
                                 lambda c: wd_ref[0, c, :], wg_ref.shape[-1])

        @pl.when(j == 0)
        def _():
            o_ref[...] = contrib

        @pl.when(j != 0)
        def _():
            o_ref[...] += contrib

    @pl.when(jnp.logical_and(valid_ref[s] == 0, j == 0))
    def _():
        o_ref[...] = jnp.zeros_like(o_ref)


def _expert_ffn(h, w_gate, w_up, w_down, step_tile, step_expert, step_valid, tm, tf, name):
    n_rows = h.shape[0]
    d_ff = w_gate.shape[-1]
    n_steps = step_tile.shape[0]
    n_ff = d_ff // tf
    ff = lambda s, j, f: jnp.where(f[s] == 1, j, n_ff - 1)
    grid_spec = pltpu.PrefetchScalarGridSpec(
        num_scalar_prefetch=3,
        grid=(n_steps, n_ff),
        in_specs=[
            pl.BlockSpec((tm, D_MODEL), lambda s, j, t, e, f: (t[s], 0)),
            pl.BlockSpec((1, D_MODEL, tf), lambda s, j, t, e, f: (e[s], 0, ff(s, j, f))),
            pl.BlockSpec((1, D_MODEL, tf), lambda s, j, t, e, f: (e[s], 0, ff(s, j, f))),
            pl.BlockSpec((1, tf, D_MODEL), lambda s, j, t, e, f: (e[s], ff(s, j, f), 0)),
        ],
        out_specs=pl.BlockSpec((tm, D_MODEL), lambda s, j, t, e, f: (t[s], 0)),
    )
    return pl.pallas_call(
        _ffn_kernel,
        grid_spec=grid_spec,
        out_shape=jax.ShapeDtypeStruct((n_rows, D_MODEL), F32),
        compiler_params=_params(("arbitrary", "arbitrary")),
        name=name,
    )(step_tile, step_expert, step_valid, h, w_gate, w_up, w_down)


TOP_K = 2
MOE_ROW_TILE = 512
DISPATCH_STEPS = 32
MOE_ROWS = TOP_K * ALL_TOKENS + N_EXPERTS * MOE_ROW_TILE


def _route_plan(route):
    tm = MOE_ROW_TILE
    n_steps = MOE_ROWS // tm
    expert = route[:, ROUTE_IDX_LANE:ROUTE_IDX_LANE + TOP_K].astype(jnp.int32).reshape(-1)
    onehot = (expert[:, None] == jnp.arange(N_EXPERTS, dtype=jnp.int32)[None, :]).astype(jnp.int32)
    running = jnp.cumsum(onehot, axis=0)
    rank = jnp.sum(onehot * (running - 1), axis=1)
    counts = running[-1]
    tiles = (counts + tm - 1) // tm
    tile_end = jnp.cumsum(tiles)
    starts = (tile_end - tiles) * tm
    pos = jnp.sum(onehot * starts[None, :], axis=1) + rank
    last = tile_end[-1] - 1
    step = jnp.arange(n_steps, dtype=jnp.int32)
    step_expert = jnp.sum((jnp.minimum(step, last)[:, None] >= tile_end[None, :]).astype(jnp.int32), axis=1)
    step_valid = (step <= last).astype(jnp.int32)
    return pos.astype(jnp.int32), step, step_expert.astype(jnp.int32), step_valid


def _dispatch_kernel(pos_ref, h_ref, init_ref, wg_ref, wu_ref, wd_ref, xs_ref, wgb_ref, wub_ref, wdb_ref, sem):
    del init_ref
    tm = h_ref.shape[0]
    base = pl.program_id(0) * tm

    def issue(r, carry):
        for k in range(TOP_K):
            dst = pos_ref[TOP_K * (base + r) + k]
            pltpu.make_async_copy(h_ref.at[pl.ds(r, 1)], xs_ref.at[pl.ds(dst, 1)], sem).start()
        return carry

    lax.fori_loop(0, tm, issue, 0)
    wgb_ref[...] = wg_ref[...].astype(BF16)
    wub_ref[...] = wu_ref[...].astype(BF16)
    wdb_ref[...] = wd_ref[...].astype(BF16)
    for k in range(TOP_K):
        pltpu.make_async_copy(h_ref, xs_ref.at[pl.ds(0, tm)], sem).wait()


def _dispatch(h, pos, w_gate, w_up, w_down):
    n_steps = DISPATCH_STEPS
    tm = ALL_TOKENS // n_steps
    wg2 = w_gate.reshape(-1, w_gate.shape[-1])
    wu2 = w_up.reshape(-1, w_up.shape[-1])
    wd2 = w_down.reshape(-1, w_down.shape[-1])
    slab = lambda w: pl.BlockSpec((w.shape[0] // n_steps, w.shape[1]), lambda i, p: (i, 0))
    grid_spec = pltpu.PrefetchScalarGridSpec(
        num_scalar_prefetch=1,
        grid=(n_steps,),
        in_specs=[pl.BlockSpec((tm, D_MODEL), lambda i, p: (i, 0)),
                  pl.BlockSpec(memory_space=pl.ANY),
                  slab(wg2), slab(wu2), slab(wd2)],
        out_specs=[pl.BlockSpec(memory_space=pl.ANY), slab(wg2), slab(wu2), slab(wd2)],
        scratch_shapes=[pltpu.SemaphoreType.DMA(())],
    )
    xs, wgb, wub, wdb = pl.pallas_call(
        _dispatch_kernel,
        grid_spec=grid_spec,
        out_shape=[jax.ShapeDtypeStruct((MOE_ROWS, D_MODEL), F32)]
        + [jax.ShapeDtypeStruct(w.shape, BF16) for w in (wg2, wu2, wd2)],
        input_output_aliases={2: 0},
        compiler_params=_params(("arbitrary",)),
        name="moe_dispatch",
    )(pos, h, jnp.zeros((MOE_ROWS, D_MODEL), F32), wg2, wu2, wd2)
    return xs, wgb.reshape(w_gate.shape), wub.reshape(w_up.shape), wdb.reshape(w_down.shape)


def _combine_kernel(pos_ref, x_ref, route_ref, mod_ref, g_ref, ys_ref, *rest, final):
    out_refs, (y_scr, sems) = rest[:-2], rest[-2:]
    tm = x_ref.shape[0]
    i = pl.program_id(0)
    n = pl.num_programs(0)

    def gather(tile, slot):
        base = tile * tm

        def issue(r, carry):
            for k in range(TOP_K):
                src = pos_ref[TOP_K * (base + r) + k]
                pltpu.make_async_copy(ys_ref.at[pl.ds(src, 1)], y_scr.at[slot, k, pl.ds(r, 1)],
                                      sems.at[slot, k]).start()
            return carry

        lax.fori_loop(0, tm, issue, 0)

    @pl.when(i == 0)
    def _():
        gather(0, 0)

    @pl.when(i + 1 < n)
    def _():
        gather(i + 1, (i + 1) % 2)

    slot = i % 2
    for k in range(TOP_K):
        pltpu.make_async_copy(ys_ref.at[pl.ds(0, tm)], y_scr.at[slot, k], sems.at[slot, k]).wait()
    route = route_ref[...]
    f = (route[:, ROUTE_W_LANE:ROUTE_W_LANE + 1] * y_scr[slot, 0]
         + route[:, ROUTE_W_LANE + 1:ROUTE_W_LANE + 2] * y_scr[slot, 1])
    x = x_ref[...] + mod_ref[0][5:6] * f
    if not final:
        out_refs[0][...] = x
        return
    ms = jnp.mean(x * x, axis=-1, keepdims=True)
    x = x * lax.rsqrt(ms + EPS) * g_ref[...]
    is_ctx = i < CTX_TOKENS // tm
    @pl.when(is_ctx)
    def _():
        out_refs[0][...] = x

    @pl.when(jnp.logical_not(is_ctx))
    def _():
        out_refs[1][...] = x


def _combine(x_all, y_sorted, pos, route, mod, final_g, final):
    tm = ROW_TILE
    n_ctx = CTX_TOKENS // tm
    row = lambda w: pl.BlockSpec((tm, w), lambda i, p: (i, 0))
    if final:
        out_specs = [pl.BlockSpec((tm, D_MODEL), lambda i, p: (jnp.minimum(i, n_ctx - 1), 0)),
                     pl.BlockSpec((tm, D_MODEL), lambda i, p: (jnp.maximum(i - n_ctx, 0), 0))]
        out_shape = [jax.ShapeDtypeStruct((CTX_TOKENS, D_MODEL), F32),
                     jax.ShapeDtypeStruct((DEC_TOKENS, D_MODEL), F32)]
    else:
        out_specs = [row(D_MODEL)]
        out_shape = [jax.ShapeDtypeStruct((ALL_TOKENS, D_MODEL), F32)]
    grid_spec = pltpu.PrefetchScalarGridSpec(
        num_scalar_prefetch=1,
        grid=(ALL_TOKENS // tm,),
        in_specs=[row(D_MODEL), row(LANES),
                  pl.BlockSpec((1, 6, D_MODEL), lambda i, p: (_mod_row(i, tm), 0, 0)),
                  pl.BlockSpec((1, D_MODEL), lambda i, p: (0, 0)),
                  pl.BlockSpec(memory_space=pl.ANY)],
        out_specs=out_specs,
        scratch_shapes=[pltpu.VMEM((2, TOP_K, tm, D_MODEL), F32), pltpu.SemaphoreType.DMA((2, TOP_K))],
    )
    return pl.pallas_call(
        functools.partial(_combine_kernel, final=final),
        grid_spec=grid_spec,
        out_shape=out_shape,
        compiler_params=_params(("arbitrary",)),
        name="moe_combine",
    )(pos, x_all, route, mod, final_g, y_sorted)


def _rope_tables():
    rows = DEC_SEQ // GRID_W
    row = jnp.repeat(jnp.arange(rows, dtype=F32), GRID_W)
    col = jnp.tile(jnp.arange(GRID_W, dtype=F32), rows)
    inv = ROPE_THETA ** (-jnp.arange(0, AXIS_DIM, 2, dtype=F32) / AXIS_DIM)
    ang = jnp.concatenate([row[:, None] * inv, col[:, None] * inv], axis=-1)
    cos, sin = jnp.cos(ang), jnp.sin(ang)
    reps = LANES // HEAD_DIM
    cos_l = jnp.tile(jnp.concatenate([cos, cos], axis=-1), (1, reps))
    sin_l = jnp.tile(jnp.concatenate([-sin, sin], axis=-1), (1, reps))
    return cos_l, sin_l


def kernel(x_prompt, x_sample, cache_k, cache_v, state_ssm, c, c_ctx, w_ada, b_ada, norm1_g, norm2_g, w_in, q_norm_g, k_norm_g, ssm_a_re, ssm_a_im, ssm_log_dt, ssm_b_re, ssm_b_im, ssm_c_re, ssm_c_im, ssm_d, w_glu, w_attn_br, w_out, ffn_w_gate, ffn_w_up, ffn_w_down, moe_router, moe_w_gate, moe_w_up, moe_w_down, final_norm_g):
    P = dict(ssm_a_re=ssm_a_re, ssm_a_im=ssm_a_im, ssm_log_dt=ssm_log_dt, ssm_b_re=ssm_b_re,
             ssm_b_im=ssm_b_im, ssm_c_re=ssm_c_re, ssm_c_im=ssm_c_im)

    c_rows = jnp.zeros((SUBLANES, D_MODEL), F32).at[0].set(c_ctx).at[1:1 + DEC_BATCH].set(c)
    mod_all = _ada_modulation(c_rows, w_ada, b_ada)
    mod_all = mod_all[:, :N_MOD_ROWS].reshape(DEPTH, N_MOD_ROWS, 6, D_MODEL)

    rope_cos, rope_sin = _rope_tables()
    head_id = jnp.arange(MXU_WIDTH, dtype=jnp.int32) // HEAD_DIM
    ones_bd = (head_id[:, None] == head_id[None, :]).astype(BF16)

    cache_k4 = cache_k.reshape(DEC_BATCH, DEPTH, PAST_LEN, KV_W)
    cache_v4 = cache_v.reshape(DEC_BATCH, DEPTH, PAST_LEN, KV_W)
    chain = DEC_SEQ // S5_PIECE
    h0_ctx = jnp.zeros((N_DIRS, S5_N_SETS, BATCH, 2 * S5_SET_ST), F32)
    ks, vs, ss = [], [], []
    x_all = None

    for l in range(DEPTH):
        mod = mod_all[l]
        first = x_all is None
        x_streams = ((x_prompt.reshape(CTX_TOKENS, D_MODEL), x_sample.reshape(DEC_TOKENS, D_MODEL), 0) if first
                     else (x_all, x_all, N_CTX_TILES))
        u, q, k_ctx, v_ctx, k_dec, v_dec, sgs, sga, *x_copy = _in_projection(
            *x_streams, mod, norm1_g[l][None, :], w_in[l].astype(BF16), ones_bd,
            jnp.tile(q_norm_g[l], N_HEADS)[None, :], jnp.tile(k_norm_g[l], N_KV_HEADS)[None, :],
            rope_cos, rope_sin, first)
        if first:
            x_all = x_copy[0]

        attn_ctx = _attention(q, k_ctx, v_ctx, BATCH, SEQ, 0, "attention_context")
        attn_dec = _attention(q, k_dec, v_dec, DEC_BATCH, DEC_SEQ, CTX_TOKENS, "attention_latent",
                              cache=(cache_k4, cache_v4, l))
        ks.append(k_ctx.reshape(BATCH, SEQ, N_KV_HEADS, HEAD_DIM))
        vs.append(v_ctx.reshape(BATCH, SEQ, N_KV_HEADS, HEAD_DIM))

        bd, a_rows, cd = _s5_operators(l, P)
        d_skip = ssm_d[l][None, :]
        y_ctx, fin = _s5_branch(u, 0, CTX_TOKENS, bd, a_rows, cd, h0_ctx, d_skip, 1, "s5_context")
        h0_dec = state_ssm[:, l].reshape(DEC_BATCH, N_DIRS, S5_N_SETS, S5_GROUP_SET * STATE_N, 2)
        h0_dec = h0_dec.transpose(1, 2, 0, 4, 3).reshape(N_DIRS, S5_N_SETS, DEC_BATCH, 2 * S5_SET_ST)
        h0_dec = jnp.repeat(h0_dec, chain, axis=2)
        y_dec, _ = _s5_branch(u, CTX_TOKENS, DEC_TOKENS, bd, a_rows, cd, h0_dec, d_skip, chain, "s5_latent")
        fin = fin.reshape(N_DIRS, S5_N_SETS, BATCH, 2, S5_GROUP_SET, STATE_N)
        ss.append(fin.transpose(2, 0, 1, 4, 5, 3).reshape(BATCH, N_DIRS, N_GROUPS, STATE_N, 2))

        x_all = _mixer_out(attn_ctx, attn_dec, y_ctx, y_dec, sgs, sga, x_all, mod, w_attn_br[l].astype(BF16),
                           w_glu[l].astype(BF16), w_out[l].astype(BF16))

        i = l // 2
        last_layer = l == DEPTH - 1
        if l % 2 == 0:
            out = _dense_ffn(x_all, mod, norm2_g[l][None, :], ffn_w_gate[i].astype(BF16), ffn_w_up[i].astype(BF16),
                             ffn_w_down[i].astype(BF16), final_norm_g[None, :], last_layer)
            x_all = out
            streams = (out[:CTX_TOKENS], out[CTX_TOKENS:])
        else:
            router_pad = jnp.pad(moe_router[i].astype(F32), ((0, 0), (0, LANES - N_EXPERTS)))
            h2, route = _prenorm_router(x_all, mod, norm2_g[l][None, :], router_pad)
            pos, tile, expert, valid = _route_plan(route)
            h_sorted, wg, wu, wd = _dispatch(h2, pos, moe_w_gate[i], moe_w_up[i], moe_w_down[i])
            y_sorted = _expert_ffn(h_sorted, wg, wu, wd, tile, expert, valid, MOE_ROW_TILE, D_FF_EXPERT // 2,
                                   "ffn_experts")
            out = _combine(x_all, y_sorted, pos, route, mod, final_norm_g[None, :], last_layer)
            if last_layer:
                streams = tuple(out)
            else:
                x_all = out[0]
                streams = (x_all[:CTX_TOKENS], x_all[CTX_TOKENS:])

    y_prompt = streams[0].reshape(BATCH, SEQ, D_MODEL)
    y_sample = streams[1].reshape(DEC_BATCH, DEC_SEQ, D_MODEL)
    return (y_prompt, y_sample, jnp.stack(ks, axis=1), jnp.stack(vs, axis=1), jnp.stack(ss, axis=1))
```

```python
import functools
import math

import jax
import jax.numpy as jnp
from jax import lax
from jax.experimental import pallas as pl
from jax.experimental.pallas import tpu as pltpu

F32 = jnp.float32
BF16 = jnp.bfloat16

D_MODEL = 1024
BATCH = 32
SEQ = 256
DEPTH = 2
DEC_BATCH = 2
DEC_SEQ = 1024
PAST_LEN = 256
GRID_W = 64
HEAD_DIM = 64
N_HEADS = 16
N_KV_HEADS = 4
ROPE_THETA = 10000.0
AXIS_DIM = HEAD_DIM // 2
HALF = HEAD_DIM // 2
SSM_GROUP = 16
SSM_WIDTH = 512
N_GROUPS = SSM_WIDTH // SSM_GROUP
STATE_N = 64
N_DIRS = 2
D_FF = 2816
N_EXPERTS = 8
D_FF_EXPERT = 3584
EPS = 1e-6
Q_W = N_HEADS * HEAD_DIM
KV_W = N_KV_HEADS * HEAD_DIM
IN_W = SSM_WIDTH + Q_W + 2 * KV_W + 2 * D_MODEL

CTX_TOKENS = BATCH * SEQ
DEC_TOKENS = DEC_BATCH * DEC_SEQ
ALL_TOKENS = CTX_TOKENS + DEC_TOKENS

LANES = 128
MXU_WIDTH = 256
FF_CHUNK = MXU_WIDTH
SUBLANES = 8
VMEM_LIMIT_BYTES = 56 * 1024 * 1024

ROW_TILE = 512
ATTN_Q_TILE = 256
S5_GROUP_SET = 8
S5_SET_CH = S5_GROUP_SET * SSM_GROUP
S5_SET_ST = S5_GROUP_SET * STATE_N
S5_N_SETS = N_GROUPS // S5_GROUP_SET
S5_SCAN_UNROLL = 4
S5_PIECE = SEQ
N_MOD_ROWS = 1 + DEC_BATCH


def _params(sem):
    return pltpu.CompilerParams(dimension_semantics=sem, vmem_limit_bytes=VMEM_LIMIT_BYTES)


def _resident(shape, index_map):
    return pl.BlockSpec(shape, index_map, pipeline_mode=pl.Buffered(1))


def _sigmoid(x):
    return 1.0 / (1.0 + jnp.exp(-x))


def _mod_row(i, tm):
    nctx = CTX_TOKENS // tm
    per_b = DEC_SEQ // tm
    return jnp.where(i < nctx, 0, 1 + (i - nctx) // per_b)


def _rope_block(i, tm):
    nctx = CTX_TOKENS // tm
    per_b = DEC_SEQ // tm
    return jnp.maximum(i - nctx, 0) % per_b


N_CTX_TILES = CTX_TOKENS // ROW_TILE


def _stream_specs(width, dec_first_block=0):
    return (pl.BlockSpec((ROW_TILE, width), lambda i, *_: (jnp.minimum(i, N_CTX_TILES - 1), 0)),
            pl.BlockSpec((ROW_TILE, width), lambda i, *_: (dec_first_block + jnp.maximum(i - N_CTX_TILES, 0), 0)))


def _is_ctx_tile():
    return pl.program_id(0) < N_CTX_TILES


def _pick_stream(ctx_ref, dec_ref):
    return jnp.where(_is_ctx_tile(), ctx_ref[...], dec_ref[...])


def _rms_modulate(x, g, scale, shift):
    ms = jnp.mean(x * x, axis=-1, keepdims=True)
    return (x * lax.rsqrt(ms + EPS) * g) * (1.0 + scale) + shift


def _ada_kernel(c_ref, w_ref, b_ref, o_ref):
    c = c_ref[...]
    act = c * _sigmoid(c)
    o_ref[0] = jnp.dot(act.astype(BF16), w_ref[0].astype(BF16), preferred_element_type=F32) + b_ref[0]


def _ada_modulation(c_rows, w_ada, b_ada):
    tn = 1536
    return pl.pallas_call(
        _ada_kernel,
        grid=(DEPTH, 6 * D_MODEL // tn),
        in_specs=[
            pl.BlockSpec((SUBLANES, D_MODEL), lambda l, n: (0, 0)),
            pl.BlockSpec((1, D_MODEL, tn), lambda l, n: (l, 0, n)),
            pl.BlockSpec((1, 1, tn), lambda l, n: (l, 0, n)),
        ],
        out_specs=pl.BlockSpec((1, SUBLANES, tn), lambda l, n: (l, 0, n)),
        out_shape=jax.ShapeDtypeStruct((DEPTH, SUBLANES, 6 * D_MODEL), F32),
        compiler_params=_params(("parallel", "parallel")),
        name="ada_modulation",
    )(c_rows, w_ada, b_ada.reshape(DEPTH, 1, 6 * D_MODEL))


def _rotate_pairs(x, cos, sin_signed):
    n = x.shape[-1]
    lane = lax.broadcasted_iota(jnp.int32, x.shape, 1)
    first_half = (lane % HEAD_DIM) < HALF
    partner = jnp.where(first_half, pltpu.roll(x, n - HALF, 1), pltpu.roll(x, HALF, 1))
    reps = n // LANES
    return x * jnp.tile(cos, (1, reps)) + partner * jnp.tile(sin_signed, (1, reps))


def _inproj_kernel(x_ctx_ref, x_dec_ref, mod_ref, g_ref, w_ref, ones_ref, qg_ref, kg_ref, cos_ref, sin_ref,
                   u_ref, q_ref, kc_ref, vc_ref, kd_ref, vd_ref, sgs_ref, sga_ref, *x_copy_ref):
    mod = mod_ref[0]
    x = _pick_stream(x_ctx_ref, x_dec_ref)
    if x_copy_ref:
        x_copy_ref[0][...] = x
    h = _rms_modulate(x, g_ref[...], mod[1:2], mod[0:1]).astype(BF16)

    def proj(lo, width):
        return jnp.dot(h, w_ref[:, lo:lo + width], preferred_element_type=F32)

    def head_norm(t, gain):
        sq = (t * t).astype(BF16)
        ss = jnp.concatenate(
            [jnp.dot(sq[:, c:c + MXU_WIDTH], ones_ref[...], preferred_element_type=F32)
             for c in range(0, t.shape[-1], MXU_WIDTH)], axis=-1)
        return t * lax.rsqrt(ss * (1.0 / HEAD_DIM) + EPS) * gain

    o = 0
    u_ref[...] = proj(o, SSM_WIDTH)
    o += SSM_WIDTH
    q = head_norm(proj(o, Q_W), qg_ref[...]) * (1.0 / math.sqrt(HEAD_DIM))
    o += Q_W
    k = head_norm(proj(o, KV_W), kg_ref[...])
    o += KV_W
    v = proj(o, KV_W)
    o += KV_W
    sgs_ref[...] = _sigmoid(proj(o, D_MODEL))
    o += D_MODEL
    sga_ref[...] = _sigmoid(proj(o, D_MODEL))

    @pl.when(_is_ctx_tile())
    def _():
        q_ref[...] = q.astype(BF16)
        kc_ref[...] = k
        vc_ref[...] = v

    @pl.when(jnp.logical_not(_is_ctx_tile()))
    def _():
        cos = cos_ref[...]
        sin = sin_ref[...]
        q_ref[...] = _rotate_pairs(q, cos, sin).astype(BF16)
        kd_ref[...] = _rotate_pairs(k, cos, sin)
        vd_ref[...] = v


def _in_projection(x_ctx, x_dec, dec_first_block, mod, norm_g, w_in_bf16, ones_bd, q_gain, k_gain,
                   rope_cos, rope_sin, copy_x):
    tm = ROW_TILE
    n_tiles = ALL_TOKENS // tm
    row = lambda w: pl.BlockSpec((tm, w), lambda i: (i, 0))
    kv_ctx, kv_dec = _stream_specs(KV_W)
    out_specs = [row(SSM_WIDTH), row(Q_W), kv_ctx, kv_ctx, kv_dec, kv_dec, row(D_MODEL), row(D_MODEL)]
    out_shape = [jax.ShapeDtypeStruct((ALL_TOKENS, SSM_WIDTH), F32), jax.ShapeDtypeStruct((ALL_TOKENS, Q_W), BF16),
                 jax.ShapeDtypeStruct((CTX_TOKENS, KV_W), F32), jax.ShapeDtypeStruct((CTX_TOKENS, KV_W), F32),
                 jax.ShapeDtypeStruct((DEC_TOKENS, KV_W), F32), jax.ShapeDtypeStruct((DEC_TOKENS, KV_W), F32),
                 jax.ShapeDtypeStruct((ALL_TOKENS, D_MODEL), F32), jax.ShapeDtypeStruct((ALL_TOKENS, D_MODEL), F32)]
    if copy_x:
        out_specs.append(row(D_MODEL))
        out_shape.append(jax.ShapeDtypeStruct((ALL_TOKENS, D_MODEL), F32))
    return pl.pallas_call(
        _inproj_kernel,
        grid=(n_tiles,),
        in_specs=[
            *_stream_specs(D_MODEL, dec_first_block),
            pl.BlockSpec((1, 6, D_MODEL), lambda i: (_mod_row(i, tm), 0, 0)),
            _resident((1, D_MODEL), lambda i: (0, 0)),
            _resident((D_MODEL, IN_W), lambda i: (0, 0)),
            _resident((MXU_WIDTH, MXU_WIDTH), lambda i: (0, 0)),
            _resident((1, Q_W), lambda i: (0, 0)),
            _resident((1, KV_W), lambda i: (0, 0)),
            pl.BlockSpec((tm, LANES), lambda i: (_rope_block(i, tm), 0)),
            pl.BlockSpec((tm, LANES), lambda i: (_rope_block(i, tm), 0)),
        ],
        out_specs=out_specs,
        out_shape=out_shape,
        compiler_params=_params(("arbitrary",)),
        name="in_projection",
    )(x_ctx, x_dec, mod, norm_g, w_in_bf16, ones_bd, q_gain, k_gain, rope_cos, rope_sin)


def _attn_kernel(*refs, with_cache):
    if with_cache:
        q_ref, k_ref, v_ref, ck_ref, cv_ref, o_ref = refs
        k_parts = (lambda sl: ck_ref[0, 0, :, sl], lambda sl: k_ref[0, :, sl])
        v_parts = (lambda sl: cv_ref[0, 0, :, sl], lambda sl: v_ref[0, :, sl])
    else:
        q_ref, k_ref, v_ref, o_ref = refs
        k_parts = (lambda sl: k_ref[0, :, sl],)
        v_parts = (lambda sl: v_ref[0, :, sl],)
    lane = lax.broadcasted_iota(jnp.int32, (1, LANES), 1)
    low = lane < HEAD_DIM
    groups_per_slab = LANES // HEAD_DIM
    rep = N_HEADS // N_KV_HEADS
    for g in range(N_KV_HEADS):
        slab = g // groups_per_slab
        own_low = (g % groups_per_slab) == 0
        keep = low if own_low else jnp.logical_not(low)

        def halves(parts):
            lanes = slice(slab * LANES, (slab + 1) * LANES)
            rows = jnp.concatenate([part(lanes) for part in parts], axis=0)
            same = jnp.where(keep, rows, 0.0)
            swapped = pltpu.roll(same, HEAD_DIM, 1)
            lo, hi = (same, swapped) if own_low else (swapped, same)
            return lo.astype(BF16), hi.astype(BF16)

        k_lo, k_hi = halves(k_parts)
        v_lo, v_hi = halves(v_parts)
        for j in range(g * rep // groups_per_slab, (g + 1) * rep // groups_per_slab):
            qs = q_ref[0, :, j * LANES:(j + 1) * LANES]
            acc = None
            for kk, vv in ((k_lo, v_lo), (k_hi, v_hi)):
                s = lax.dot_general(qs, kk, (((1,), (1,)), ((), ())), preferred_element_type=F32)
                p = jnp.exp(s - jnp.max(s, axis=-1, keepdims=True))
                denom = jnp.sum(p, axis=-1, keepdims=True)
                part = jnp.dot(p.astype(BF16), vv, preferred_element_type=F32) / denom
                acc = part if acc is None else acc + part
            o_ref[:, j * LANES:(j + 1) * LANES] = acc.astype(BF16)


def _attention(q, k, v, n_batch, seq_len, first_token, name, cache=None):
    tq = ATTN_Q_TILE
    per_b = seq_len // tq
    q_tiles = q.reshape(ALL_TOKENS // tq, tq, Q_W)
    first_tile = first_token // tq
    kv_spec = pl.BlockSpec((1, seq_len, KV_W), lambda b, i: (b, 0, 0))
    in_specs = [pl.BlockSpec((1, tq, Q_W), lambda b, i: (first_tile + b * per_b + i, 0, 0)), kv_spec, kv_spec]
    args = [q_tiles, k.reshape(n_batch, seq_len, KV_W), v.reshape(n_batch, seq_len, KV_W)]
    if cache is not None:
        cache_k, cache_v, layer = cache
        cache_spec = pl.BlockSpec((1, 1, PAST_LEN, KV_W), lambda b, i: (b, layer, 0, 0))
        in_specs += [cache_spec, cache_spec]
        args += [cache_k, cache_v]
    return pl.pallas_call(
        functools.partial(_attn_kernel, with_cache=cache is not None),
        grid=(n_batch, per_b),
        in_specs=in_specs,
        out_specs=pl.BlockSpec((tq, Q_W), lambda b, i: (b * per_b + i, 0)),
        out_shape=jax.ShapeDtypeStruct((n_batch * seq_len, Q_W), BF16),
        compiler_params=_params(("parallel", "parallel")),
        name=name,
    )(*args)


def _s5_kernel(u_ref, bd_ref, a_ref, cd_ref, h0_ref, dskip_ref, y_ref, hfin_ref,
               u_scr, y_scr, bu_scr, hs_scr, *, chain):
    seq_len = u_ref.shape[0] // SUBLANES
    forward = pl.program_id(2) == 0
    last_dir = pl.program_id(2) == pl.num_programs(2) - 1
    a_re = jnp.broadcast_to(a_ref[0, 0, :, :S5_SET_ST], (SUBLANES, S5_SET_ST))
    a_im = jnp.broadcast_to(a_ref[0, 0, :, S5_SET_ST:], (SUBLANES, S5_SET_ST))

    @pl.when(forward)
    def _():
        for p in range(SUBLANES):
            u_scr[pl.ds(p, seq_len, stride=SUBLANES), :] = u_ref[p * seq_len:(p + 1) * seq_len, :]
        y_scr[...] = u_scr[...] * dskip_ref[...]

    bu_scr[...] = jnp.dot(u_scr[...].astype(BF16), bd_ref[0, 0], preferred_element_type=F32)

    def scan(init, store):
        def step(t, hc):
            h_re, h_im = hc
            tt = jnp.where(forward, t, seq_len - 1 - t)
            r = pl.multiple_of(tt * SUBLANES, SUBLANES)
            n_re = a_re * h_re - a_im * h_im + bu_scr[pl.ds(r, SUBLANES), :S5_SET_ST]
            n_im = a_re * h_im + a_im * h_re + bu_scr[pl.ds(r, SUBLANES), S5_SET_ST:]
            if store:
                hs_scr[pl.ds(r, SUBLANES), :S5_SET_ST] = n_re
                hs_scr[pl.ds(r, SUBLANES), S5_SET_ST:] = n_im
            return n_re, n_im

        return lax.fori_loop(0, seq_len, step, init, unroll=S5_SCAN_UNROLL)

    h0 = h0_ref[0, 0]
    init = (h0[:, :S5_SET_ST], h0[:, S5_SET_ST:])
    if chain > 1:
        zero = jnp.zeros((SUBLANES, S5_SET_ST), F32)
        end_re, end_im = scan((zero, zero), store=False)
        p_re, p_im = a_re, a_im
        for _ in range(seq_len.bit_length() - 1):
            p_re, p_im = p_re * p_re - p_im * p_im, 2.0 * (p_re * p_im)
        piece = lax.broadcasted_iota(jnp.int32, (SUBLANES, S5_SET_ST), 0) % chain
        entry = piece == jnp.where(forward, 0, chain - 1)

        def from_neighbour(x):
            return jnp.where(forward, pltpu.roll(x, 1, 0), pltpu.roll(x, SUBLANES - 1, 0))

        i_re = jnp.where(entry, init[0], 0.0)
        i_im = jnp.where(entry, init[1], 0.0)
        for _ in range(chain - 1):
            o_re = p_re * i_re - p_im * i_im + end_re
            o_im = p_re * i_im + p_im * i_re + end_im
            i_re = jnp.where(entry, init[0], from_neighbour(o_re))
            i_im = jnp.where(entry, init[1], from_neighbour(o_im))
        init = (i_re, i_im)
    h_re, h_im = scan(init, store=True)
    hfin_ref[0, 0, :, :S5_SET_ST] = h_re
    hfin_ref[0, 0, :, S5_SET_ST:] = h_im
    y_scr[...] += jnp.dot(hs_scr[...].astype(BF16), cd_ref[0, 0], preferred_element_type=F32)

    @pl.when(last_dir)
    def _():
        for p in range(SUBLANES):
            y_ref[p * seq_len:(p + 1) * seq_len, :] = y_scr[pl.ds(p, seq_len, stride=SUBLANES), :]


def _s5_branch(u_all, first_token, n_tokens, bd, a_bar, cd, h0, d_skip, chain, name):
    seq_len = S5_PIECE
    rows = seq_len * SUBLANES
    nb = n_tokens // seq_len
    first_block = first_token // rows
    assert seq_len & (seq_len - 1) == 0 and SUBLANES % chain == 0 and first_token % rows == 0
    st2 = 2 * S5_SET_ST
    par = lambda shape: pl.BlockSpec(shape, lambda b, s, d: (d, s, 0, 0))
    state = pl.BlockSpec((1, 1, SUBLANES, st2), lambda b, s, d: (d, s, b, 0))
    seq = pl.BlockSpec((rows, S5_SET_CH), lambda b, s, d: (b, s))
    return pl.pallas_call(
        functools.partial(_s5_kernel, chain=chain),
        grid=(nb // SUBLANES, S5_N_SETS, N_DIRS),
        in_specs=[
            pl.BlockSpec((rows, S5_SET_CH), lambda b, s, d: (first_block + b, s)),
            par((1, 1, S5_SET_CH, st2)),
            par((1, 1, 1, st2)),
            par((1, 1, st2, S5_SET_CH)),
            state,
            pl.BlockSpec((1, S5_SET_CH), lambda b, s, d: (0, s)),
        ],
        out_specs=[seq, state],
        out_shape=[jax.ShapeDtypeStruct((n_tokens, SSM_WIDTH), F32),
                   jax.ShapeDtypeStruct((N_DIRS, S5_N_SETS, nb, st2), F32)],
        scratch_shapes=[pltpu.VMEM((rows, S5_SET_CH), F32), pltpu.VMEM((rows, S5_SET_CH), F32),
                        pltpu.VMEM((rows, st2), F32), pltpu.VMEM((rows, st2), F32)],
        compiler_params=_params(("parallel", "parallel", "arbitrary")),
        name=name,
    )(u_all, bd, a_bar, cd, h0, d_skip)


def _s5_operators(l, P):
    a_re, a_im = P['ssm_a_re'][l].astype(F32), P['ssm_a_im'][l].astype(F32)
    dt = jnp.exp(P['ssm_log_dt'][l].astype(F32))[..., None]
    mag = jnp.exp(a_re * dt)
    ab_re, ab_im = mag * jnp.cos(a_im * dt), mag * jnp.sin(a_im * dt)
    den = a_re * a_re + a_im * a_im
    k_re = ((ab_re - 1.0) * a_re + ab_im * a_im) / den
    k_im = (ab_im * a_re - (ab_re - 1.0) * a_im) / den
    b_re, b_im = P['ssm_b_re'][l].astype(F32), P['ssm_b_im'][l].astype(F32)
    bb_re = k_re[..., None] * b_re - k_im[..., None] * b_im
    bb_im = k_re[..., None] * b_im + k_im[..., None] * b_re
    eye = jnp.eye(S5_GROUP_SET, dtype=F32)

    def sets(t):
        return t.reshape(N_DIRS, S5_N_SETS, S5_GROUP_SET, *t.shape[2:])

    def b_operator(part):
        return jnp.einsum('dsgnp,gh->dsgphn', part, eye).reshape(N_DIRS, S5_N_SETS, S5_SET_CH, S5_SET_ST)

    def c_operator(part):
        return jnp.einsum('dsgpn,gh->dshngp', part, eye).reshape(N_DIRS, S5_N_SETS, S5_SET_ST, S5_SET_CH)

    bd = jnp.concatenate([b_operator(sets(bb_re)), b_operator(sets(bb_im))], axis=-1).astype(BF16)
    c_re = sets(P['ssm_c_re'][l].astype(F32))
    c_im = sets(P['ssm_c_im'][l].astype(F32))
    cd = jnp.concatenate([c_operator(c_re), -c_operator(c_im)], axis=-2).astype(BF16)
    row = lambda t: sets(t).reshape(N_DIRS, S5_N_SETS, 1, S5_SET_ST)
    a_rows = jnp.concatenate([row(ab_re), row(ab_im)], axis=-1)
    return bd, a_rows, cd


def _mixer_out_kernel(attn_ctx_ref, attn_dec_ref, y_ctx_ref, y_dec_ref, sgs_ref, sga_ref, x_ref, mod_ref,
                      wab_ref, wglu_ref, wout_ref, o_ref):
    attn_br = jnp.dot(_pick_stream(attn_ctx_ref, attn_dec_ref), wab_ref[...], preferred_element_type=F32)
    y = _pick_stream(y_ctx_ref, y_dec_ref)
    gelu = 0.5 * y * (1.0 + jnp.tanh(math.sqrt(2.0 / math.pi) * (y + 0.044715 * (y * y * y))))
    glu = jnp.dot(gelu.astype(BF16), wglu_ref[...], preferred_element_type=F32)
    ssm_br = glu[:, :D_MODEL] * _sigmoid(glu[:, D_MODEL:])
    merged = sgs_ref[...] * ssm_br + sga_ref[...] * attn_br
    out = jnp.dot(merged.astype(BF16), wout_ref[...], preferred_element_type=F32)
    o_ref[...] = x_ref[...] + mod_ref[0][2:3] * out


def _mixer_out(attn_ctx, attn_dec, y_ctx, y_dec, sgs, sga, x_all, mod, w_attn_br, w_glu, w_out):
    tm = ROW_TILE
    row = lambda w: pl.BlockSpec((tm, w), lambda i: (i, 0))
    return pl.pallas_call(
        _mixer_out_kernel,
        grid=(ALL_TOKENS // tm,),
        in_specs=[
            *_stream_specs(Q_W), *_stream_specs(SSM_WIDTH),
            row(D_MODEL), row(D_MODEL), row(D_MODEL),
            pl.BlockSpec((1, 6, D_MODEL), lambda i: (_mod_row(i, tm), 0, 0)),
            _resident((Q_W, D_MODEL), lambda i: (0, 0)),
            _resident((SSM_WIDTH, 2 * D_MODEL), lambda i: (0, 0)),
            _resident((D_MODEL, D_MODEL), lambda i: (0, 0)),
        ],
        out_specs=row(D_MODEL),
        out_shape=jax.ShapeDtypeStruct((ALL_TOKENS, D_MODEL), F32),
        compiler_params=_params(("parallel",)),
        name="mixer_out",
    )(attn_ctx, attn_dec, y_ctx, y_dec, sgs, sga, x_all, mod, w_attn_br, w_glu, w_out)


ROUTE_IDX_LANE = 0
ROUTE_W_LANE = 2


ROW_CHUNKS = D_MODEL // LANES


def _store_rows_as_tiles(ref, lead, value):
    n = value.shape[0]
    for c in range(ROW_CHUNKS):
        ref[(*lead, pl.ds(c, n, stride=ROW_CHUNKS), slice(None))] = value[:, c * LANES:(c + 1) * LANES]


def _load_rows_from_tiles(ref, lead, n, first=0, every=1):
    stride = every * ROW_CHUNKS
    return jnp.concatenate(
        [ref[(*lead, pl.ds(first * ROW_CHUNKS + c, n, stride=stride), slice(None))] for c in range(ROW_CHUNKS)],
        axis=-1)


def _router_kernel(x_ref, mod_ref, g_ref, r_ref, h_ref, route_ref):
    mod = mod_ref[0]
    h = _rms_modulate(x_ref[...], g_ref[...], mod[4:5], mod[3:4])
    _store_rows_as_tiles(h_ref, (), h)
    r = r_ref[...]
    h_hi, r_hi = h.astype(BF16), r.astype(BF16)
    h_lo = (h - h_hi.astype(F32)).astype(BF16)
    r_lo = (r - r_hi.astype(F32)).astype(BF16)
    logits = (jnp.dot(h_hi, r_hi, preferred_element_type=F32) + jnp.dot(h_lo, r_hi, preferred_element_type=F32)
              + jnp.dot(h_hi, r_lo, preferred_element_type=F32))
    lane = lax.broadcasted_iota(jnp.int32, logits.shape, 1).astype(F32)
    logits = jnp.where(lane < N_EXPERTS, logits, -jnp.inf)
    top1 = jnp.max(logits, axis=-1, keepdims=True)
    idx1 = jnp.min(jnp.where(logits == top1, lane, float(LANES)), axis=-1, keepdims=True)
    rest = jnp.where(lane == idx1, -jnp.inf, logits)
    top2 = jnp.max(rest, axis=-1, keepdims=True)
    idx2 = jnp.min(jnp.where(rest == top2, lane, float(LANES)), axis=-1, keepdims=True)
    e2 = jnp.exp(top2 - top1)
    w1 = 1.0 / (1.0 + e2)
    w2 = e2 / (1.0 + e2)
    route = jnp.where(lane == ROUTE_IDX_LANE, idx1, 0.0)
    route = jnp.where(lane == ROUTE_IDX_LANE + 1, idx2, route)
    route = jnp.where(lane == ROUTE_W_LANE, w1, route)
    route_ref[...] = jnp.where(lane == ROUTE_W_LANE + 1, w2, route)


def _prenorm_router(x_all, mod, norm_g, router_pad):
    tm = ROW_TILE
    row = lambda w: pl.BlockSpec((tm, w), lambda i: (i, 0))
    in_specs = [row(D_MODEL),
                pl.BlockSpec((1, 6, D_MODEL), lambda i: (_mod_row(i, tm), 0, 0)),
                _resident((1, D_MODEL), lambda i: (0, 0))]
    return pl.pallas_call(
        _router_kernel, grid=(ALL_TOKENS // tm,),
        in_specs=in_specs + [_resident((D_MODEL, LANES), lambda i: (0, 0))],
        out_specs=[pl.BlockSpec((tm * ROW_CHUNKS, LANES), lambda i: (i, 0)), row(LANES)],
        out_shape=[jax.ShapeDtypeStruct((ALL_TOKENS * ROW_CHUNKS, LANES), F32),
                   jax.ShapeDtypeStruct((ALL_TOKENS, LANES), F32)],
        compiler_params=_params(("parallel",)), name="prenorm_router",
    )(x_all, mod, norm_g, router_pad)


def _swiglu_chains(h, gate_cols, up_cols, down_rows, d_ff):
    out = None
    for c in range(d_ff // FF_CHUNK):
        cols = slice(c * FF_CHUNK, (c + 1) * FF_CHUNK)
        a = jnp.dot(h, gate_cols(cols), preferred_element_type=F32)
        b = jnp.dot(h, up_cols(cols), preferred_element_type=F32)
        inter = ((a * _sigmoid(a)) * b).astype(BF16)
        part = jnp.dot(inter, down_rows(cols), preferred_element_type=F32)
        out = part if out is None else out + part
    return out


def _dense_ffn_kernel(x_ref, mod_ref, g_ref, wg_ref, wu_ref, wd_ref, fg_ref, o_ref, *, final):
    mod = mod_ref[0]
    x = x_ref[...]
    h = _rms_modulate(x, g_ref[...], mod[4:5], mod[3:4]).astype(BF16)
    f = _swiglu_chains(h, lambda c: wg_ref[:, c], lambda c: wu_ref[:, c], lambda c: wd_ref[c, :],
                       wg_ref.shape[-1])
    x = x + mod[5:6] * f
    if final:
        ms = jnp.mean(x * x, axis=-1, keepdims=True)
        x = x * lax.rsqrt(ms + EPS) * fg_ref[...]
    o_ref[...] = x


def _dense_ffn(x_all, mod, norm_g, w_gate, w_up, w_down, final_g, final):
    tm = ROW_TILE
    d_ff = w_gate.shape[-1]
    row = pl.BlockSpec((tm, D_MODEL), lambda i: (i, 0))
    return pl.pallas_call(
        functools.partial(_dense_ffn_kernel, final=final),
        grid=(ALL_TOKENS // tm,),
        in_specs=[row,
                  pl.BlockSpec((1, 6, D_MODEL), lambda i: (_mod_row(i, tm), 0, 0)),
                  _resident((1, D_MODEL), lambda i: (0, 0)),
                  _resident((D_MODEL, d_ff), lambda i: (0, 0)),
                  _resident((D_MODEL, d_ff), lambda i: (0, 0)),
                  _resident((d_ff, D_MODEL), lambda i: (0, 0)),
                  _resident((1, D_MODEL), lambda i: (0, 0))],
        out_specs=row,
        out_shape=jax.ShapeDtypeStruct((ALL_TOKENS, D_MODEL), F32),
        compiler_params=_params(("parallel",)),
        name="ffn_dense",
    )(x_all, mod, norm_g, w_gate, w_up, w_down, final_g)


def _moe_ffn_kernel(tok_ref, slot_ref, expert_ref, valid_ref, h_ref, wg_ref, wu_ref, wd_ref, y_ref,
                    xbuf, ybuf, acc, sems):
    s = pl.program_id(0)
    j = pl.program_id(1)
    n_j = pl.num_programs(1)
    tm = acc.shape[0]
    share = tm // n_j
    cur = s % 2
    other = 1 - cur
    gather_sem, scatter_sem = sems.at[0], sems.at[1]

    def tile_of(row):
        return pl.ds(pl.multiple_of(row * ROW_CHUNKS, ROW_CHUNKS), ROW_CHUNKS)

    def whole(n_rows):
        return pl.ds(0, n_rows * ROW_CHUNKS)

    @pl.when(jnp.logical_and(s == 0, j == 0))
    def _():
        ybuf[...] = jnp.zeros_like(ybuf)

        def first_tile(r, carry):
            pltpu.make_async_copy(h_ref.at[tile_of(tok_ref[tm + r])], xbuf.at[0, tile_of(r)], gather_sem).start()
            return carry

        lax.fori_loop(0, tm, first_tile, 0)
        pltpu.make_async_copy(h_ref.at[whole(tm)], xbuf.at[0], gather_sem).wait()
        spill = pltpu.make_async_copy(
            ybuf.at[0], y_ref.at[pl.ds(TOP_K * ALL_TOKENS * ROW_CHUNKS, tm * ROW_CHUNKS)], scatter_sem)
        spill.start()
        spill.wait()

    def start_row_copies():
        row0 = j * share
        for r in range(share):
            row = row0 + r
            pltpu.make_async_copy(h_ref.at[tile_of(tok_ref[(s + 2) * tm + row])], xbuf.at[other, tile_of(row)],
                                  gather_sem).start()
            pltpu.make_async_copy(ybuf.at[other, tile_of(row)], y_ref.at[tile_of(slot_ref[s * tm + row])],
                                  scatter_sem).start()

    def drain_row_copies():
        pltpu.make_async_copy(h_ref.at[whole(share)], xbuf.at[0, whole(share)], gather_sem).wait()
        pltpu.make_async_copy(ybuf.at[0, whole(share)], y_ref.at[whole(share)], scatter_sem).wait()

    @pl.when(valid_ref[s] == 1)
    def _():
        h = _load_rows_from_tiles(xbuf, (cur,), tm).astype(BF16)
        start_row_copies()
        contrib = _swiglu_chains(h, lambda c: wg_ref[0, :, c], lambda c: wu_ref[0, :, c],
                                 lambda c: wd_ref[0, c, :], wg_ref.shape[-1])

        @pl.when(j == 0)
        def _():
            acc[...] = contrib

        @pl.when(jnp.logical_and(j != 0, j != n_j - 1))
        def _():
            acc[...] += contrib

        @pl.when(j == n_j - 1)
        def _():
            _store_rows_as_tiles(ybuf, (cur,), acc[...] + contrib)

        drain_row_copies()

    @pl.when(valid_ref[s] == 0)
    def _():
        start_row_copies()
        drain_row_copies()


def _moe_ffn(h_tiles, row_token, row_slot, w_gate, w_up, w_down, step_expert, step_valid, tf):
    tm = MOE_ROW_TILE
    d_ff = w_gate.shape[-1]
    n_steps = step_expert.shape[0]
    n_ff = d_ff // tf
    assert n_ff >= 2
    ff = lambda s, j, f: jnp.where(f[s] == 1, j, n_ff - 1)
    grid_spec = pltpu.PrefetchScalarGridSpec(
        num_scalar_prefetch=4,
        grid=(n_steps, n_ff),
        in_specs=[
            pl.BlockSpec(memory_space=pl.ANY),
            pl.BlockSpec((1, D_MODEL, tf), lambda s, j, t, d, e, f: (e[s], 0, ff(s, j, f))),
            pl.BlockSpec((1, D_MODEL, tf), lambda s, j, t, d, e, f: (e[s], 0, ff(s, j, f))),
            pl.BlockSpec((1, tf, D_MODEL), lambda s, j, t, d, e, f: (e[s], ff(s, j, f), 0)),
        ],
        out_specs=pl.BlockSpec(memory_space=pl.ANY),
        scratch_shapes=[pltpu.VMEM((2, tm * ROW_CHUNKS, LANES), F32), pltpu.VMEM((2, tm * ROW_CHUNKS, LANES), F32),
                        pltpu.VMEM((tm, D_MODEL), F32), pltpu.SemaphoreType.DMA((2,))],
    )
    return pl.pallas_call(
        _moe_ffn_kernel,
        grid_spec=grid_spec,
        out_shape=jax.ShapeDtypeStruct(((TOP_K * ALL_TOKENS + tm) * ROW_CHUNKS, LANES), F32),
        compiler_params=_params(("arbitrary", "arbitrary")),
        name="ffn_experts",
    )(row_token, row_slot, step_expert, step_valid, h_tiles, w_gate, w_up, w_down)


TOP_K = 2
MOE_ROW_TILE = 512
NARROW_STEPS = 32
MOE_ROWS = TOP_K * ALL_TOKENS + N_EXPERTS * MOE_ROW_TILE


def _route_plan(route):
    tm = MOE_ROW_TILE
    n_slots = TOP_K * ALL_TOKENS
    n_steps = MOE_ROWS // tm + 1
    expert = route[:, ROUTE_IDX_LANE:ROUTE_IDX_LANE + TOP_K].astype(jnp.int32).reshape(-1)
    experts = jnp.arange(N_EXPERTS, dtype=jnp.int32)[None, :]
    order = jnp.argsort(expert, stable=True).astype(jnp.int32)
    counts = jnp.sum((expert[:, None] == experts).astype(jnp.int32), axis=0)
    tiles = (counts + tm - 1) // tm
    tile_end = jnp.cumsum(tiles)
    starts = (tile_end - tiles) * tm
    before = jnp.cumsum(counts) - counts
    last = tile_end[-1] - 1
    step = jnp.arange(n_steps, dtype=jnp.int32)
    step_expert = jnp.sum((jnp.minimum(step, last)[:, None] >= tile_end[None, :]).astype(jnp.int32), axis=1)
    step_valid = (step <= last).astype(jnp.int32)
    row = jnp.arange((n_steps + 2) * tm, dtype=jnp.int32) - tm
    row_expert = jnp.sum(((row // tm)[:, None] >= tile_end[None, :]).astype(jnp.int32), axis=1)
    pick = (jnp.minimum(row_expert, N_EXPERTS - 1)[:, None] == experts).astype(jnp.int32)
    rank = row - jnp.sum(pick * starts[None, :], axis=1)
    real = (row >= 0) & (row_expert < N_EXPERTS) & (rank < jnp.sum(pick * counts[None, :], axis=1))
    sorted_pos = jnp.clip(jnp.sum(pick * before[None, :], axis=1) + rank, 0, n_slots - 1)
    slot = order[sorted_pos]
    token = slot // TOP_K
    row_token = jnp.where(real, token, 0)
    row_slot = jnp.where(real, (slot % TOP_K) * ALL_TOKENS + token, n_slots + row % tm)
    return row_token.astype(jnp.int32), row_slot.astype(jnp.int32), step_expert.astype(jnp.int32), step_valid


def _narrow_kernel(wg_ref, wu_ref, wd_ref, wgb_ref, wub_ref, wdb_ref):
    wgb_ref[...] = wg_ref[...].astype(BF16)
    wub_ref[...] = wu_ref[...].astype(BF16)
    wdb_ref[...] = wd_ref[...].astype(BF16)


def _narrow_expert_weights(w_gate, w_up, w_down):
    n_steps = NARROW_STEPS
    flat = [w.reshape(-1, w.shape[-1]) for w in (w_gate, w_up, w_down)]
    slabs = [pl.BlockSpec((w.shape[0] // n_steps, w.shape[1]), lambda i: (i, 0)) for w in flat]
    out = pl.pallas_call(
        _narrow_kernel,
        grid=(n_steps,),
        in_specs=slabs,
        out_specs=slabs,
        out_shape=[jax.ShapeDtypeStruct(w.shape, BF16) for w in flat],
        compiler_params=_params(("parallel",)),
        name="moe_weight_narrow",
    )(*flat)
    return [o.reshape(w.shape) for o, w in zip(out, (w_gate, w_up, w_down))]


def _combine_kernel(x_ref, route_ref, mod_ref, g_ref, y0_ref, y1_ref, *out_refs, final):
    route = route_ref[...]
    tm = x_ref.shape[0]
    f = (route[:, ROUTE_W_LANE:ROUTE_W_LANE + 1] * _load_rows_from_tiles(y0_ref, (), tm)
         + route[:, ROUTE_W_LANE + 1:ROUTE_W_LANE + 2] * _load_rows_from_tiles(y1_ref, (), tm))
    x = x_ref[...] + mod_ref[0][5:6] * f
    if not final:
        out_refs[0][...] = x
        return
    ms = jnp.mean(x * x, axis=-1, keepdims=True)
    x = x * lax.rsqrt(ms + EPS) * g_ref[...]
    is_ctx = _is_ctx_tile()
    @pl.when(is_ctx)
    def _():
        out_refs[0][...] = x

    @pl.when(jnp.logical_not(is_ctx))
    def _():
        out_refs[1][...] = x


def _combine(x_all, y_slots, route, mod, final_g, final):
    tm = ROW_TILE
    n_tiles = ALL_TOKENS // tm
    row = lambda w: pl.BlockSpec((tm, w), lambda i: (i, 0))
    if final:
        out_specs = list(_stream_specs(D_MODEL))
        out_shape = [jax.ShapeDtypeStruct((CTX_TOKENS, D_MODEL), F32),
                     jax.ShapeDtypeStruct((DEC_TOKENS, D_MODEL), F32)]
    else:
        out_specs = [row(D_MODEL)]
        out_shape = [jax.ShapeDtypeStruct((ALL_TOKENS, D_MODEL), F32)]
    return pl.pallas_call(
        functools.partial(_combine_kernel, final=final),
        grid=(n_tiles,),
        in_specs=[row(D_MODEL), row(LANES),
                  pl.BlockSpec((1, 6, D_MODEL), lambda i: (_mod_row(i, tm), 0, 0)),
                  _resident((1, D_MODEL), lambda i: (0, 0)),
                  pl.BlockSpec((tm * ROW_CHUNKS, LANES), lambda i: (i, 0)),
                  pl.BlockSpec((tm * ROW_CHUNKS, LANES), lambda i: (n_tiles + i, 0))],
        out_specs=out_specs,
        out_shape=out_shape,
        compiler_params=_params(("arbitrary",)),
        name="moe_combine",
    )(x_all, route, mod, final_g, y_slots, y_slots)


def _rope_tables():
    rows = DEC_SEQ // GRID_W
    row = jnp.repeat(jnp.arange(rows, dtype=F32), GRID_W)
    col = jnp.tile(jnp.arange(GRID_W, dtype=F32), rows)
    inv = ROPE_THETA ** (-jnp.arange(0, AXIS_DIM, 2, dtype=F32) / AXIS_DIM)
    ang = jnp.concatenate([row[:, None] * inv, col[:, None] * inv], axis=-1)
    cos, sin = jnp.cos(ang), jnp.sin(ang)
    reps = LANES // HEAD_DIM
    cos_l = jnp.tile(jnp.concatenate([cos, cos], axis=-1), (1, reps))
    sin_l = jnp.tile(jnp.concatenate([-sin, sin], axis=-1), (1, reps))
    return cos_l, sin_l


def kernel(x_prompt, x_sample, cache_k, cache_v, state_ssm, c, c_ctx, w_ada, b_ada, norm1_g, norm2_g, w_in, q_norm_g, k_norm_g, ssm_a_re, ssm_a_im, ssm_log_dt, ssm_b_re, ssm_b_im, ssm_c_re, ssm_c_im, ssm_d, w_glu, w_attn_br, w_out, ffn_w_gate, ffn_w_up, ffn_w_down, moe_router, moe_w_gate, moe_w_up, moe_w_down, final_norm_g):
    P = dict(ssm_a_re=ssm_a_re, ssm_a_im=ssm_a_im, ssm_log_dt=ssm_log_dt, ssm_b_re=ssm_b_re,
             ssm_b_im=ssm_b_im, ssm_c_re=ssm_c_re, ssm_c_im=ssm_c_im)

    c_rows = jnp.zeros((SUBLANES, D_MODEL), F32).at[0].set(c_ctx).at[1:1 + DEC_BATCH].set(c)
    mod_all = _ada_modulation(c_rows, w_ada, b_ada)
    mod_all = mod_all[:, :N_MOD_ROWS].reshape(DEPTH, N_MOD_ROWS, 6, D_MODEL)

    rope_cos, rope_sin = _rope_tables()
    head_id = jnp.arange(MXU_WIDTH, dtype=jnp.int32) // HEAD_DIM
    ones_bd = (head_id[:, None] == head_id[None, :]).astype(BF16)

    cache_k4 = cache_k.reshape(DEC_BATCH, DEPTH, PAST_LEN, KV_W)
    cache_v4 = cache_v.reshape(DEC_BATCH, DEPTH, PAST_LEN, KV_W)
    chain = DEC_SEQ // S5_PIECE
    h0_ctx = jnp.zeros((N_DIRS, S5_N_SETS, BATCH, 2 * S5_SET_ST), F32)
    ks, vs, ss = [], [], []
    x_all = None

    for l in range(DEPTH):
        mod = mod_all[l]
        first = x_all is None
        x_streams = ((x_prompt.reshape(CTX_TOKENS, D_MODEL), x_sample.reshape(DEC_TOKENS, D_MODEL), 0) if first
                     else (x_all, x_all, N_CTX_TILES))
        u, q, k_ctx, v_ctx, k_dec, v_dec, sgs, sga, *x_copy = _in_projection(
            *x_streams, mod, norm1_g[l][None, :], w_in[l].astype(BF16), ones_bd,
            jnp.tile(q_norm_g[l], N_HEADS)[None, :], jnp.tile(k_norm_g[l], N_KV_HEADS)[None, :],
            rope_cos, rope_sin, first)
        if first:
            x_all = x_copy[0]

        attn_ctx = _attention(q, k_ctx, v_ctx, BATCH, SEQ, 0, "attention_context")
        attn_dec = _attention(q, k_dec, v_dec, DEC_BATCH, DEC_SEQ, CTX_TOKENS, "attention_latent",
                              cache=(cache_k4, cache_v4, l))
        ks.append(k_ctx.reshape(BATCH, SEQ, N_KV_HEADS, HEAD_DIM))
        vs.append(v_ctx.reshape(BATCH, SEQ, N_KV_HEADS, HEAD_DIM))

        bd, a_rows, cd = _s5_operators(l, P)
        d_skip = ssm_d[l][None, :]
        y_ctx, fin = _s5_branch(u, 0, CTX_TOKENS, bd, a_rows, cd, h0_ctx, d_skip, 1, "s5_context")
        h0_dec = state_ssm[:, l].reshape(DEC_BATCH, N_DIRS, S5_N_SETS, S5_GROUP_SET * STATE_N, 2)
        h0_dec = h0_dec.transpose(1, 2, 0, 4, 3).reshape(N_DIRS, S5_N_SETS, DEC_BATCH, 2 * S5_SET_ST)
        h0_dec = jnp.repeat(h0_dec, chain, axis=2)
        y_dec, _ = _s5_branch(u, CTX_TOKENS, DEC_TOKENS, bd, a_rows, cd, h0_dec, d_skip, chain, "s5_latent")
        fin = fin.reshape(N_DIRS, S5_N_SETS, BATCH, 2, S5_GROUP_SET, STATE_N)
        ss.append(fin.transpose(2, 0, 1, 4, 5, 3).reshape(BATCH, N_DIRS, N_GROUPS, STATE_N, 2))

        x_all = _mixer_out(attn_ctx, attn_dec, y_ctx, y_dec, sgs, sga, x_all, mod, w_attn_br[l].astype(BF16),
                           w_glu[l].astype(BF16), w_out[l].astype(BF16))

        i = l // 2
        last_layer = l == DEPTH - 1
        if l % 2 == 0:
            out = _dense_ffn(x_all, mod, norm2_g[l][None, :], ffn_w_gate[i].astype(BF16), ffn_w_up[i].astype(BF16),
                             ffn_w_down[i].astype(BF16), final_norm_g[None, :], last_layer)
            x_all = out
            streams = (out[:CTX_TOKENS], out[CTX_TOKENS:])
        else:
            router_pad = jnp.pad(moe_router[i].astype(F32), ((0, 0), (0, LANES - N_EXPERTS)))
            h2, route = _prenorm_router(x_all, mod, norm2_g[l][None, :], router_pad)
            row_token, row_slot, expert, valid = _route_plan(route)
            wg, wu, wd = _narrow_expert_weights(moe_w_gate[i], moe_w_up[i], moe_w_down[i])
            y_slots = _moe_ffn(h2, row_token, row_slot, wg, wu, wd, expert, valid, D_FF_EXPERT // 2)
            out = _combine(x_all, y_slots, route, mod, final_norm_g[None, :], last_layer)
            if last_layer:
                streams = tuple(out)
            else:
                x_all = out[0]
                streams = (x_all[:CTX_TOKENS], x_all[CTX_TOKENS:])

    y_prompt = streams[0].reshape(BATCH, SEQ, D_MODEL)
    y_sample = streams[1].reshape(DEC_BATCH, DEC_SEQ, D_MODEL)
    return (y_prompt, y_sample, jnp.stack(ks, axis=1), jnp.stack(vs, axis=1), jnp.stack(ss, axis=1))
```

```python
import functools
import math

import jax
import jax.numpy as jnp
from jax import lax
from jax.experimental import pallas as pl
from jax.experimental.pallas import tpu as pltpu

F32 = jnp.float32
BF16 = jnp.bfloat16

D_MODEL = 1024
BATCH = 32
SEQ = 256
DEPTH = 2
DEC_BATCH = 2
DEC_SEQ = 1024
PAST_LEN = 256
GRID_W = 64
HEAD_DIM = 64
N_HEADS = 16
N_KV_HEADS = 4
ROPE_THETA = 10000.0
AXIS_DIM = HEAD_DIM // 2
HALF = HEAD_DIM // 2
SSM_GROUP = 16
SSM_WIDTH = 512
N_GROUPS = SSM_WIDTH // SSM_GROUP
STATE_N = 64
N_DIRS = 2
D_FF = 2816
N_EXPERTS = 8
D_FF_EXPERT = 3584
EPS = 1e-6
Q_W = N_HEADS * HEAD_DIM
KV_W = N_KV_HEADS * HEAD_DIM
IN_W = SSM_WIDTH + Q_W + 2 * KV_W + 2 * D_MODEL

CTX_TOKENS = BATCH * SEQ
DEC_TOKENS = DEC_BATCH * DEC_SEQ
ALL_TOKENS = CTX_TOKENS + DEC_TOKENS

LANES = 128
MXU_WIDTH = 256
FF_CHUNK = MXU_WIDTH
SUBLANES = 8
VMEM_LIMIT_BYTES = 56 * 1024 * 1024

ROW_TILE = 512
ATTN_Q_TILE = 256
S5_GROUP_SET = 8
S5_SET_CH = S5_GROUP_SET * SSM_GROUP
S5_SET_ST = S5_GROUP_SET * STATE_N
S5_N_SETS = N_GROUPS // S5_GROUP_SET
S5_SCAN_UNROLL = 4
S5_PIECE = SEQ
N_MOD_ROWS = 1 + DEC_BATCH


def _params(sem):
    return pltpu.CompilerParams(dimension_semantics=sem, vmem_limit_bytes=VMEM_LIMIT_BYTES)


def _resident(shape, index_map):
    return pl.BlockSpec(shape, index_map, pipeline_mode=pl.Buffered(1))


def _sigmoid(x):
    return 1.0 / (1.0 + jnp.exp(-x))


def _mod_row(i, tm):
    nctx = CTX_TOKENS // tm
    per_b = DEC_SEQ // tm
    return jnp.where(i < nctx, 0, 1 + (i - nctx) // per_b)


def _rope_block(i, tm):
    nctx = CTX_TOKENS // tm
    per_b = DEC_SEQ // tm
    return jnp.maximum(i - nctx, 0) % per_b


N_CTX_TILES = CTX_TOKENS // ROW_TILE


def _stream_specs(width, dec_first_block=0):
    return (pl.BlockSpec((ROW_TILE, width), lambda i, *_: (jnp.minimum(i, N_CTX_TILES - 1), 0)),
            pl.BlockSpec((ROW_TILE, width), lambda i, *_: (dec_first_block + jnp.maximum(i - N_CTX_TILES, 0), 0)))


def _is_ctx_tile():
    return pl.program_id(0) < N_CTX_TILES


def _pick_stream(ctx_ref, dec_ref):
    return jnp.where(_is_ctx_tile(), ctx_ref[...], dec_ref[...])


def _rms_modulate(x, g, scale, shift):
    ms = jnp.mean(x * x, axis=-1, keepdims=True)
    return (x * lax.rsqrt(ms + EPS) * g) * (1.0 + scale) + shift


def _ada_kernel(c_ref, w_ref, b_ref, o_ref):
    c = c_ref[...]
    act = c * _sigmoid(c)
    o_ref[0] = jnp.dot(act.astype(BF16), w_ref[0].astype(BF16), preferred_element_type=F32) + b_ref[0]


def _ada_modulation(c_rows, w_ada, b_ada):
    tn = 1536
    return pl.pallas_call(
        _ada_kernel,
        grid=(DEPTH, 6 * D_MODEL // tn),
        in_specs=[
            pl.BlockSpec((SUBLANES, D_MODEL), lambda l, n: (0, 0)),
            pl.BlockSpec((1, D_MODEL, tn), lambda l, n: (l, 0, n)),
            pl.BlockSpec((1, 1, tn), lambda l, n: (l, 0, n)),
        ],
        out_specs=pl.BlockSpec((1, SUBLANES, tn), lambda l, n: (l, 0, n)),
        out_shape=jax.ShapeDtypeStruct((DEPTH, SUBLANES, 6 * D_MODEL), F32),
        compiler_params=_params(("parallel", "parallel")),
        name="ada_modulation",
    )(c_rows, w_ada, b_ada.reshape(DEPTH, 1, 6 * D_MODEL))


def _rotate_pairs(x, cos, sin_signed):
    n = x.shape[-1]
    lane = lax.broadcasted_iota(jnp.int32, x.shape, 1)
    first_half = (lane % HEAD_DIM) < HALF
    partner = jnp.where(first_half, pltpu.roll(x, n - HALF, 1), pltpu.roll(x, HALF, 1))
    reps = n // LANES
    return x * jnp.tile(cos, (1, reps)) + partner * jnp.tile(sin_signed, (1, reps))


def _inproj_kernel(x_ctx_ref, x_dec_ref, mod_ref, g_ref, w_ref, ones_ref, qg_ref, kg_ref, cos_ref, sin_ref,
                   u_ref, q_ref, kc_ref, vc_ref, kd_ref, vd_ref, sgs_ref, sga_ref, *x_copy_ref):
    mod = mod_ref[0]
    x = _pick_stream(x_ctx_ref, x_dec_ref)
    if x_copy_ref:
        x_copy_ref[0][...] = x
    h = _rms_modulate(x, g_ref[...], mod[1:2], mod[0:1]).astype(BF16)

    def proj(lo, width):
        return jnp.dot(h, w_ref[:, lo:lo + width], preferred_element_type=F32)

    def head_norm(t, gain):
        sq = (t * t).astype(BF16)
        ss = jnp.concatenate(
            [jnp.dot(sq[:, c:c + MXU_WIDTH], ones_ref[...], preferred_element_type=F32)
             for c in range(0, t.shape[-1], MXU_WIDTH)], axis=-1)
        return t * lax.rsqrt(ss * (1.0 / HEAD_DIM) + EPS) * gain

    o = 0
    u_ref[...] = proj(o, SSM_WIDTH)
    o += SSM_WIDTH
    q = head_norm(proj(o, Q_W), qg_ref[...]) * (1.0 / math.sqrt(HEAD_DIM))
    o += Q_W
    k = head_norm(proj(o, KV_W), kg_ref[...])
    o += KV_W
    v = proj(o, KV_W)
    o += KV_W
    sgs_ref[...] = _sigmoid(proj(o, D_MODEL))
    o += D_MODEL
    sga_ref[...] = _sigmoid(proj(o, D_MODEL))

    @pl.when(_is_ctx_tile())
    def _():
        q_ref[...] = q.astype(BF16)
        kc_ref[...] = k
        vc_ref[...] = v

    @pl.when(jnp.logical_not(_is_ctx_tile()))
    def _():
        cos = cos_ref[...]
        sin = sin_ref[...]
        q_ref[...] = _rotate_pairs(q, cos, sin).astype(BF16)
        kd_ref[...] = _rotate_pairs(k, cos, sin)
        vd_ref[...] = v


def _in_projection(x_ctx, x_dec, dec_first_block, mod, norm_g, w_in_bf16, ones_bd, q_gain, k_gain,
                   rope_cos, rope_sin, copy_x):
    tm = ROW_TILE
    n_tiles = ALL_TOKENS // tm
    row = lambda w: pl.BlockSpec((tm, w), lambda i: (i, 0))
    kv_ctx, kv_dec = _stream_specs(KV_W)
    out_specs = [row(SSM_WIDTH), row(Q_W), kv_ctx, kv_ctx, kv_dec, kv_dec, row(D_MODEL), row(D_MODEL)]
    out_shape = [jax.ShapeDtypeStruct((ALL_TOKENS, SSM_WIDTH), F32), jax.ShapeDtypeStruct((ALL_TOKENS, Q_W), BF16),
                 jax.ShapeDtypeStruct((CTX_TOKENS, KV_W), F32), jax.ShapeDtypeStruct((CTX_TOKENS, KV_W), F32),
                 jax.ShapeDtypeStruct((DEC_TOKENS, KV_W), F32), jax.ShapeDtypeStruct((DEC_TOKENS, KV_W), F32),
                 jax.ShapeDtypeStruct((ALL_TOKENS, D_MODEL), F32), jax.ShapeDtypeStruct((ALL_TOKENS, D_MODEL), F32)]
    if copy_x:
        out_specs.append(row(D_MODEL))
        out_shape.append(jax.ShapeDtypeStruct((ALL_TOKENS, D_MODEL), F32))
    return pl.pallas_call(
        _inproj_kernel,
        grid=(n_tiles,),
        in_specs=[
            *_stream_specs(D_MODEL, dec_first_block),
            pl.BlockSpec((1, 6, D_MODEL), lambda i: (_mod_row(i, tm), 0, 0)),
            _resident((1, D_MODEL), lambda i: (0, 0)),
            _resident((D_MODEL, IN_W), lambda i: (0, 0)),
            _resident((MXU_WIDTH, MXU_WIDTH), lambda i: (0, 0)),
            _resident((1, Q_W), lambda i: (0, 0)),
            _resident((1, KV_W), lambda i: (0, 0)),
            pl.BlockSpec((tm, LANES), lambda i: (_rope_block(i, tm), 0)),
            pl.BlockSpec((tm, LANES), lambda i: (_rope_block(i, tm), 0)),
        ],
        out_specs=out_specs,
        out_shape=out_shape,
        compiler_params=_params(("arbitrary",)),
        name="in_projection",
    )(x_ctx, x_dec, mod, norm_g, w_in_bf16, ones_bd, q_gain, k_gain, rope_cos, rope_sin)


def _attn_kernel(*refs, with_cache):
    if with_cache:
        q_ref, k_ref, v_ref, ck_ref, cv_ref, o_ref = refs
        k_parts = (lambda sl: ck_ref[0, 0, :, sl], lambda sl: k_ref[0, :, sl])
        v_parts = (lambda sl: cv_ref[0, 0, :, sl], lambda sl: v_ref[0, :, sl])
    else:
        q_ref, k_ref, v_ref, o_ref = refs
        k_parts = (lambda sl: k_ref[0, :, sl],)
        v_parts = (lambda sl: v_ref[0, :, sl],)
    lane = lax.broadcasted_iota(jnp.int32, (1, LANES), 1)
    low = lane < HEAD_DIM
    groups_per_slab = LANES // HEAD_DIM
    rep = N_HEADS // N_KV_HEADS
    for g in range(N_KV_HEADS):
        slab = g // groups_per_slab
        own_low = (g % groups_per_slab) == 0
        keep = low if own_low else jnp.logical_not(low)

        def halves(parts):
            lanes = slice(slab * LANES, (slab + 1) * LANES)
            rows = jnp.concatenate([part(lanes) for part in parts], axis=0)
            same = jnp.where(keep, rows, 0.0)
            swapped = pltpu.roll(same, HEAD_DIM, 1)
            lo, hi = (same, swapped) if own_low else (swapped, same)
            return lo.astype(BF16), hi.astype(BF16)

        k_lo, k_hi = halves(k_parts)
        v_lo, v_hi = halves(v_parts)
        for j in range(g * rep // groups_per_slab, (g + 1) * rep // groups_per_slab):
            qs = q_ref[0, :, j * LANES:(j + 1) * LANES]
            acc = None
            for kk, vv in ((k_lo, v_lo), (k_hi, v_hi)):
                s = lax.dot_general(qs, kk, (((1,), (1,)), ((), ())), preferred_element_type=F32)
                p = jnp.exp(s - jnp.max(s, axis=-1, keepdims=True))
                denom = jnp.sum(p, axis=-1, keepdims=True)
                part = jnp.dot(p.astype(BF16), vv, preferred_element_type=F32) / denom
                acc = part if acc is None else acc + part
            o_ref[:, j * LANES:(j + 1) * LANES] = acc.astype(BF16)


def _attention(q, k, v, n_batch, seq_len, first_token, name, cache=None):
    tq = ATTN_Q_TILE
    per_b = seq_len // tq
    q_tiles = q.reshape(ALL_TOKENS // tq, tq, Q_W)
    first_tile = first_token // tq
    kv_spec = pl.BlockSpec((1, seq_len, KV_W), lambda b, i: (b, 0, 0))
    in_specs = [pl.BlockSpec((1, tq, Q_W), lambda b, i: (first_tile + b * per_b + i, 0, 0)), kv_spec, kv_spec]
    args = [q_tiles, k.reshape(n_batch, seq_len, KV_W), v.reshape(n_batch, seq_len, KV_W)]
    if cache is not None:
        cache_k, cache_v, layer = cache
        cache_spec = pl.BlockSpec((1, 1, PAST_LEN, KV_W), lambda b, i: (b, layer, 0, 0))
        in_specs += [cache_spec, cache_spec]
        args += [cache_k, cache_v]
    return pl.pallas_call(
        functools.partial(_attn_kernel, with_cache=cache is not None),
        grid=(n_batch, per_b),
        in_specs=in_specs,
        out_specs=pl.BlockSpec((tq, Q_W), lambda b, i: (b * per_b + i, 0)),
        out_shape=jax.ShapeDtypeStruct((n_batch * seq_len, Q_W), BF16),
        compiler_params=_params(("parallel", "parallel")),
        name=name,
    )(*args)


def _s5_kernel(u_ref, bd_ref, a_ref, cd_ref, h0_ref, dskip_ref, *rest, chain, narrow):
    if narrow:
        wide_ref, y_ref, hfin_ref, narrow_ref, u_scr, y_scr, bu_scr, hs_scr = rest
        narrow_ref[...] = wide_ref[...].astype(BF16)
    else:
        y_ref, hfin_ref, u_scr, y_scr, bu_scr, hs_scr = rest
    _s5_body(u_ref, bd_ref, a_ref, cd_ref, h0_ref, dskip_ref, y_ref, hfin_ref, u_scr, y_scr, bu_scr, hs_scr, chain)


def _s5_body(u_ref, bd_ref, a_ref, cd_ref, h0_ref, dskip_ref, y_ref, hfin_ref,
             u_scr, y_scr, bu_scr, hs_scr, chain):
    seq_len = u_ref.shape[0] // SUBLANES
    forward = pl.program_id(2) == 0
    last_dir = pl.program_id(2) == pl.num_programs(2) - 1
    a_re = jnp.broadcast_to(a_ref[0, 0, :, :S5_SET_ST], (SUBLANES, S5_SET_ST))
    a_im = jnp.broadcast_to(a_ref[0, 0, :, S5_SET_ST:], (SUBLANES, S5_SET_ST))

    @pl.when(forward)
    def _():
        for p in range(SUBLANES):
            u_scr[pl.ds(p, seq_len, stride=SUBLANES), :] = u_ref[p * seq_len:(p + 1) * seq_len, :]
        y_scr[...] = u_scr[...] * dskip_ref[...]

    bu_scr[...] = jnp.dot(u_scr[...].astype(BF16), bd_ref[0, 0], preferred_element_type=F32)

    def scan(init, store):
        def step(t, hc):
            h_re, h_im = hc
            tt = jnp.where(forward, t, seq_len - 1 - t)
            r = pl.multiple_of(tt * SUBLANES, SUBLANES)
            n_re = a_re * h_re - a_im * h_im + bu_scr[pl.ds(r, SUBLANES), :S5_SET_ST]
            n_im = a_re * h_im + a_im * h_re + bu_scr[pl.ds(r, SUBLANES), S5_SET_ST:]
            if store:
                hs_scr[pl.ds(r, SUBLANES), :S5_SET_ST] = n_re
                hs_scr[pl.ds(r, SUBLANES), S5_SET_ST:] = n_im
            return n_re, n_im

        return lax.fori_loop(0, seq_len, step, init, unroll=S5_SCAN_UNROLL)

    h0 = h0_ref[0, 0]
    init = (h0[:, :S5_SET_ST], h0[:, S5_SET_ST:])
    if chain > 1:
        zero = jnp.zeros((SUBLANES, S5_SET_ST), F32)
        end_re, end_im = scan((zero, zero), store=False)
        p_re, p_im = a_re, a_im
        for _ in range(seq_len.bit_length() - 1):
            p_re, p_im = p_re * p_re - p_im * p_im, 2.0 * (p_re * p_im)
        piece = lax.broadcasted_iota(jnp.int32, (SUBLANES, S5_SET_ST), 0) % chain
        entry = piece == jnp.where(forward, 0, chain - 1)

        def from_neighbour(x):
            return jnp.where(forward, pltpu.roll(x, 1, 0), pltpu.roll(x, SUBLANES - 1, 0))

        i_re = jnp.where(entry, init[0], 0.0)
        i_im = jnp.where(entry, init[1], 0.0)
        for _ in range(chain - 1):
            o_re = p_re * i_re - p_im * i_im + end_re
            o_im = p_re * i_im + p_im * i_re + end_im
            i_re = jnp.where(entry, init[0], from_neighbour(o_re))
            i_im = jnp.where(entry, init[1], from_neighbour(o_im))
        init = (i_re, i_im)
    h_re, h_im = scan(init, store=True)
    hfin_ref[0, 0, :, :S5_SET_ST] = h_re
    hfin_ref[0, 0, :, S5_SET_ST:] = h_im
    y_scr[...] += jnp.dot(hs_scr[...].astype(BF16), cd_ref[0, 0], preferred_element_type=F32)

    @pl.when(last_dir)
    def _():
        for p in range(SUBLANES):
            y_ref[p * seq_len:(p + 1) * seq_len, :] = y_scr[pl.ds(p, seq_len, stride=SUBLANES), :]


def _s5_branch(u_all, first_token, n_tokens, bd, a_bar, cd, h0, d_skip, chain, name, rider=None):
    seq_len = S5_PIECE
    rows = seq_len * SUBLANES
    nb = n_tokens // seq_len
    first_block = first_token // rows
    assert seq_len & (seq_len - 1) == 0 and SUBLANES % chain == 0 and first_token % rows == 0
    st2 = 2 * S5_SET_ST
    par = lambda shape: pl.BlockSpec(shape, lambda b, s, d: (d, s, 0, 0))
    state = pl.BlockSpec((1, 1, SUBLANES, st2), lambda b, s, d: (d, s, b, 0))
    seq = pl.BlockSpec((rows, S5_SET_CH), lambda b, s, d: (b, s))
    grid = (nb // SUBLANES, S5_N_SETS, N_DIRS)
    in_specs = [
        pl.BlockSpec((rows, S5_SET_CH), lambda b, s, d: (first_block + b, s)),
        par((1, 1, S5_SET_CH, st2)),
        par((1, 1, 1, st2)),
        par((1, 1, st2, S5_SET_CH)),
        state,
        pl.BlockSpec((1, S5_SET_CH), lambda b, s, d: (0, s)),
    ]
    out_specs = [seq, state]
    out_shape = [jax.ShapeDtypeStruct((n_tokens, SSM_WIDTH), F32),
                 jax.ShapeDtypeStruct((N_DIRS, S5_N_SETS, nb, st2), F32)]
    args = [u_all, bd, a_bar, cd, h0, d_skip]
    if rider is not None:
        n_steps = grid[0] * grid[1] * grid[2]
        slab = pl.BlockSpec((rider.shape[0] // n_steps, rider.shape[1]),
                            lambda b, s, d: ((b * S5_N_SETS + s) * N_DIRS + d, 0))
        in_specs.append(slab)
        out_specs.append(slab)
        out_shape.append(jax.ShapeDtypeStruct(rider.shape, BF16))
        args.append(rider)
    return pl.pallas_call(
        functools.partial(_s5_kernel, chain=chain, narrow=rider is not None),
        grid=grid,
        in_specs=in_specs,
        out_specs=out_specs,
        out_shape=out_shape,
        scratch_shapes=[pltpu.VMEM((rows, S5_SET_CH), F32), pltpu.VMEM((rows, S5_SET_CH), F32),
                        pltpu.VMEM((rows, st2), F32), pltpu.VMEM((rows, st2), F32)],
        compiler_params=_params(("parallel", "parallel", "arbitrary")),
        name=name,
    )(*args)


def _s5_operators(l, P):
    a_re, a_im = P['ssm_a_re'][l].astype(F32), P['ssm_a_im'][l].astype(F32)
    dt = jnp.exp(P['ssm_log_dt'][l].astype(F32))[..., None]
    mag = jnp.exp(a_re * dt)
    ab_re, ab_im = mag * jnp.cos(a_im * dt), mag * jnp.sin(a_im * dt)
    den = a_re * a_re + a_im * a_im
    k_re = ((ab_re - 1.0) * a_re + ab_im * a_im) / den
    k_im = (ab_im * a_re - (ab_re - 1.0) * a_im) / den
    b_re, b_im = P['ssm_b_re'][l].astype(F32), P['ssm_b_im'][l].astype(F32)
    bb_re = k_re[..., None] * b_re - k_im[..., None] * b_im
    bb_im = k_re[..., None] * b_im + k_im[..., None] * b_re
    eye = jnp.eye(S5_GROUP_SET, dtype=F32)

    def sets(t):
        return t.reshape(N_DIRS, S5_N_SETS, S5_GROUP_SET, *t.shape[2:])

    def b_operator(part):
        return jnp.einsum('dsgnp,gh->dsgphn', part, eye).reshape(N_DIRS, S5_N_SETS, S5_SET_CH, S5_SET_ST)

    def c_operator(part):
        return jnp.einsum('dsgpn,gh->dshngp', part, eye).reshape(N_DIRS, S5_N_SETS, S5_SET_ST, S5_SET_CH)

    bd = jnp.concatenate([b_operator(sets(bb_re)), b_operator(sets(bb_im))], axis=-1).astype(BF16)
    c_re = sets(P['ssm_c_re'][l].astype(F32))
    c_im = sets(P['ssm_c_im'][l].astype(F32))
    cd = jnp.concatenate([c_operator(c_re), -c_operator(c_im)], axis=-2).astype(BF16)
    row = lambda t: sets(t).reshape(N_DIRS, S5_N_SETS, 1, S5_SET_ST)
    a_rows = jnp.concatenate([row(ab_re), row(ab_im)], axis=-1)
    return bd, a_rows, cd


def _mixer_out_kernel(attn_ctx_ref, attn_dec_ref, y_ctx_ref, y_dec_ref, sgs_ref, sga_ref, x_ref, mod_ref,
                      wab_ref, wglu_ref, wout_ref, o_ref):
    attn_br = jnp.dot(_pick_stream(attn_ctx_ref, attn_dec_ref), wab_ref[...], preferred_element_type=F32)
    y = _pick_stream(y_ctx_ref, y_dec_ref)
    gelu = 0.5 * y * (1.0 + jnp.tanh(math.sqrt(2.0 / math.pi) * (y + 0.044715 * (y * y * y))))
    glu = jnp.dot(gelu.astype(BF16), wglu_ref[...], preferred_element_type=F32)
    ssm_br = glu[:, :D_MODEL] * _sigmoid(glu[:, D_MODEL:])
    merged = sgs_ref[...] * ssm_br + sga_ref[...] * attn_br
    out = jnp.dot(merged.astype(BF16), wout_ref[...], preferred_element_type=F32)
    o_ref[...] = x_ref[...] + mod_ref[0][2:3] * out


def _mixer_out(attn_ctx, attn_dec, y_ctx, y_dec, sgs, sga, x_all, mod, w_attn_br, w_glu, w_out):
    tm = ROW_TILE
    row = lambda w: pl.BlockSpec((tm, w), lambda i: (i, 0))
    return pl.pallas_call(
        _mixer_out_kernel,
        grid=(ALL_TOKENS // tm,),
        in_specs=[
            *_stream_specs(Q_W), *_stream_specs(SSM_WIDTH),
            row(D_MODEL), row(D_MODEL), row(D_MODEL),
            pl.BlockSpec((1, 6, D_MODEL), lambda i: (_mod_row(i, tm), 0, 0)),
            _resident((Q_W, D_MODEL), lambda i: (0, 0)),
            _resident((SSM_WIDTH, 2 * D_MODEL), lambda i: (0, 0)),
            _resident((D_MODEL, D_MODEL), lambda i: (0, 0)),
        ],
        out_specs=row(D_MODEL),
        out_shape=jax.ShapeDtypeStruct((ALL_TOKENS, D_MODEL), F32),
        compiler_params=_params(("parallel",)),
        name="mixer_out",
    )(attn_ctx, attn_dec, y_ctx, y_dec, sgs, sga, x_all, mod, w_attn_br, w_glu, w_out)


ROUTE_IDX_LANE = 0
ROUTE_W_LANE = 2


ROW_CHUNKS = D_MODEL // LANES


def _store_rows_as_tiles(ref, lead, value):
    n = value.shape[0]
    for c in range(ROW_CHUNKS):
        ref[(*lead, pl.ds(c, n, stride=ROW_CHUNKS), slice(None))] = value[:, c * LANES:(c + 1) * LANES]


def _load_rows_from_tiles(ref, lead, n, first=0, every=1):
    stride = every * ROW_CHUNKS
    return jnp.concatenate(
        [ref[(*lead, pl.ds(first * ROW_CHUNKS + c, n, stride=stride), slice(None))] for c in range(ROW_CHUNKS)],
        axis=-1)


def _router_kernel(x_ref, mod_ref, g_ref, r_ref, h_ref, route_ref):
    mod = mod_ref[0]
    h = _rms_modulate(x_ref[...], g_ref[...], mod[4:5], mod[3:4])
    _store_rows_as_tiles(h_ref, (), h)
    r = r_ref[...]
    h_hi, r_hi = h.astype(BF16), r.astype(BF16)
    h_lo = (h - h_hi.astype(F32)).astype(BF16)
    r_lo = (r - r_hi.astype(F32)).astype(BF16)
    logits = (jnp.dot(h_hi, r_hi, preferred_element_type=F32) + jnp.dot(h_lo, r_hi, preferred_element_type=F32)
              + jnp.dot(h_hi, r_lo, preferred_element_type=F32))
    lane = lax.broadcasted_iota(jnp.int32, logits.shape, 1).astype(F32)
    logits = jnp.where(lane < N_EXPERTS, logits, -jnp.inf)
    top1 = jnp.max(logits, axis=-1, keepdims=True)
    idx1 = jnp.min(jnp.where(logits == top1, lane, float(LANES)), axis=-1, keepdims=True)
    rest = jnp.where(lane == idx1, -jnp.inf, logits)
    top2 = jnp.max(rest, axis=-1, keepdims=True)
    idx2 = jnp.min(jnp.where(rest == top2, lane, float(LANES)), axis=-1, keepdims=True)
    e2 = jnp.exp(top2 - top1)
    w1 = 1.0 / (1.0 + e2)
    w2 = e2 / (1.0 + e2)
    route = jnp.where(lane == ROUTE_IDX_LANE, idx1, 0.0)
    route = jnp.where(lane == ROUTE_IDX_LANE + 1, idx2, route)
    route = jnp.where(lane == ROUTE_W_LANE, w1, route)
    route_ref[...] = jnp.where(lane == ROUTE_W_LANE + 1, w2, route)


def _prenorm_router(x_all, mod, norm_g, router_pad):
    tm = ROW_TILE
    row = lambda w: pl.BlockSpec((tm, w), lambda i: (i, 0))
    in_specs = [row(D_MODEL),
                pl.BlockSpec((1, 6, D_MODEL), lambda i: (_mod_row(i, tm), 0, 0)),
                _resident((1, D_MODEL), lambda i: (0, 0))]
    return pl.pallas_call(
        _router_kernel, grid=(ALL_TOKENS // tm,),
        in_specs=in_specs + [_resident((D_MODEL, LANES), lambda i: (0, 0))],
        out_specs=[pl.BlockSpec((tm * ROW_CHUNKS, LANES), lambda i: (i, 0)), row(LANES)],
        out_shape=[jax.ShapeDtypeStruct((ALL_TOKENS * ROW_CHUNKS, LANES), F32),
                   jax.ShapeDtypeStruct((ALL_TOKENS, LANES), F32)],
        compiler_params=_params(("parallel",)), name="prenorm_router",
    )(x_all, mod, norm_g, router_pad)


def _swiglu_chains(h, gate_cols, up_cols, down_rows, d_ff):
    out = None
    for c in range(d_ff // FF_CHUNK):
        cols = slice(c * FF_CHUNK, (c + 1) * FF_CHUNK)
        a = jnp.dot(h, gate_cols(cols), preferred_element_type=F32)
        b = jnp.dot(h, up_cols(cols), preferred_element_type=F32)
        inter = ((a * _sigmoid(a)) * b).astype(BF16)
        part = jnp.dot(inter, down_rows(cols), preferred_element_type=F32)
        out = part if out is None else out + part
    return out


def _dense_ffn_kernel(x_ref, mod_ref, g_ref, wg_ref, wu_ref, wd_ref, fg_ref, o_ref, *, final):
    mod = mod_ref[0]
    x = x_ref[...]
    h = _rms_modulate(x, g_ref[...], mod[4:5], mod[3:4]).astype(BF16)
    f = _swiglu_chains(h, lambda c: wg_ref[:, c], lambda c: wu_ref[:, c], lambda c: wd_ref[c, :],
                       wg_ref.shape[-1])
    x = x + mod[5:6] * f
    if final:
        ms = jnp.mean(x * x, axis=-1, keepdims=True)
        x = x * lax.rsqrt(ms + EPS) * fg_ref[...]
    o_ref[...] = x


def _dense_ffn(x_all, mod, norm_g, w_gate, w_up, w_down, final_g, final):
    tm = ROW_TILE
    d_ff = w_gate.shape[-1]
    row = pl.BlockSpec((tm, D_MODEL), lambda i: (i, 0))
    return pl.pallas_call(
        functools.partial(_dense_ffn_kernel, final=final),
        grid=(ALL_TOKENS // tm,),
        in_specs=[row,
                  pl.BlockSpec((1, 6, D_MODEL), lambda i: (_mod_row(i, tm), 0, 0)),
                  _resident((1, D_MODEL), lambda i: (0, 0)),
                  _resident((D_MODEL, d_ff), lambda i: (0, 0)),
                  _resident((D_MODEL, d_ff), lambda i: (0, 0)),
                  _resident((d_ff, D_MODEL), lambda i: (0, 0)),
                  _resident((1, D_MODEL), lambda i: (0, 0))],
        out_specs=row,
        out_shape=jax.ShapeDtypeStruct((ALL_TOKENS, D_MODEL), F32),
        compiler_params=_params(("parallel",)),
        name="ffn_dense",
    )(x_all, mod, norm_g, w_gate, w_up, w_down, final_g)


def _moe_ffn_kernel(tok_ref, slot_ref, expert_ref, valid_ref, h_ref, wg_ref, wu_ref, wd_ref, y_ref,
                    xbuf, ybuf, acc, sems):
    s = pl.program_id(0)
    j = pl.program_id(1)
    n_j = pl.num_programs(1)
    tm = acc.shape[0]
    share = tm // n_j
    cur = s % 2
    other = 1 - cur
    gather_sem, scatter_sem = sems.at[0], sems.at[1]

    def tile_of(row):
        return pl.ds(pl.multiple_of(row * ROW_CHUNKS, ROW_CHUNKS), ROW_CHUNKS)

    def whole(n_rows):
        return pl.ds(0, n_rows * ROW_CHUNKS)

    @pl.when(jnp.logical_and(s == 0, j == 0))
    def _():
        ybuf[...] = jnp.zeros_like(ybuf)

        def first_tile(r, carry):
            pltpu.make_async_copy(h_ref.at[tile_of(tok_ref[tm + r])], xbuf.at[0, tile_of(r)], gather_sem).start()
            return carry

        lax.fori_loop(0, tm, first_tile, 0)
        pltpu.make_async_copy(h_ref.at[whole(tm)], xbuf.at[0], gather_sem).wait()
        spill = pltpu.make_async_copy(
            ybuf.at[0], y_ref.at[pl.ds(TOP_K * ALL_TOKENS * ROW_CHUNKS, tm * ROW_CHUNKS)], scatter_sem)
        spill.start()
        spill.wait()

    def start_row_copies():
        row0 = j * share
        for r in range(share):
            row = row0 + r
            pltpu.make_async_copy(h_ref.at[tile_of(tok_ref[(s + 2) * tm + row])], xbuf.at[other, tile_of(row)],
                                  gather_sem).start()
            pltpu.make_async_copy(ybuf.at[other, tile_of(row)], y_ref.at[tile_of(slot_ref[s * tm + row])],
                                  scatter_sem).start()

    def drain_row_copies():
        pltpu.make_async_copy(h_ref.at[whole(share)], xbuf.at[0, whole(share)], gather_sem).wait()
        pltpu.make_async_copy(ybuf.at[0, whole(share)], y_ref.at[whole(share)], scatter_sem).wait()

    @pl.when(s >= 0)
    def _():
        start_row_copies()

    @pl.when(valid_ref[s] == 1)
    def _():
        h = _load_rows_from_tiles(xbuf, (cur,), tm).astype(BF16)
        contrib = _swiglu_chains(h, lambda c: wg_ref[0, :, c], lambda c: wu_ref[0, :, c],
                                 lambda c: wd_ref[0, c, :], wg_ref.shape[-1])

        @pl.when(j == 0)
        def _():
            acc[...] = contrib

        @pl.when(jnp.logical_and(j != 0, j != n_j - 1))
        def _():
            acc[...] += contrib

        @pl.when(j == n_j - 1)
        def _():
            _store_rows_as_tiles(ybuf, (cur,), acc[...] + contrib)

    drain_row_copies()


def _moe_ffn(h_tiles, row_token, row_slot, w_gate, w_up, w_down, step_expert, step_valid, tf):
    tm = MOE_ROW_TILE
    d_ff = w_gate.shape[-1]
    n_steps = step_expert.shape[0]
    n_ff = d_ff // tf
    assert n_ff >= 2
    ff = lambda s, j, f: jnp.where(f[s] == 1, j, n_ff - 1)
    grid_spec = pltpu.PrefetchScalarGridSpec(
        num_scalar_prefetch=4,
        grid=(n_steps, n_ff),
        in_specs=[
            pl.BlockSpec(memory_space=pl.ANY),
            pl.BlockSpec((1, D_MODEL, tf), lambda s, j, t, d, e, f: (e[s], 0, ff(s, j, f))),
            pl.BlockSpec((1, D_MODEL, tf), lambda s, j, t, d, e, f: (e[s], 0, ff(s, j, f))),
            pl.BlockSpec((1, tf, D_MODEL), lambda s, j, t, d, e, f: (e[s], ff(s, j, f), 0)),
        ],
        out_specs=pl.BlockSpec(memory_space=pl.ANY),
        scratch_shapes=[pltpu.VMEM((2, tm * ROW_CHUNKS, LANES), F32), pltpu.VMEM((2, tm * ROW_CHUNKS, LANES), F32),
                        pltpu.VMEM((tm, D_MODEL), F32), pltpu.SemaphoreType.DMA((2,))],
    )
    return pl.pallas_call(
        _moe_ffn_kernel,
        grid_spec=grid_spec,
        out_shape=jax.ShapeDtypeStruct(((TOP_K * ALL_TOKENS + tm) * ROW_CHUNKS, LANES), F32),
        compiler_params=_params(("arbitrary", "arbitrary")),
        name="ffn_experts",
    )(row_token, row_slot, step_expert, step_valid, h_tiles, w_gate, w_up, w_down)


TOP_K = 2
MOE_ROW_TILE = 512
NARROW_STEPS = 32
MOE_ROWS = TOP_K * ALL_TOKENS + N_EXPERTS * MOE_ROW_TILE


def _route_plan(route):
    tm = MOE_ROW_TILE
    n_slots = TOP_K * ALL_TOKENS
    n_steps = MOE_ROWS // tm + 1
    expert = route[:, ROUTE_IDX_LANE:ROUTE_IDX_LANE + TOP_K].astype(jnp.int32).reshape(-1)
    experts = jnp.arange(N_EXPERTS, dtype=jnp.int32)[None, :]
    order = jnp.argsort(expert, stable=True).astype(jnp.int32)
    counts = jnp.sum((expert[:, None] == experts).astype(jnp.int32), axis=0)
    tiles = (counts + tm - 1) // tm
    tile_end = jnp.cumsum(tiles)
    starts = (tile_end - tiles) * tm
    before = jnp.cumsum(counts) - counts
    last = tile_end[-1] - 1
    step = jnp.arange(n_steps, dtype=jnp.int32)
    step_expert = jnp.sum((jnp.minimum(step, last)[:, None] >= tile_end[None, :]).astype(jnp.int32), axis=1)
    step_valid = (step <= last).astype(jnp.int32)
    row = jnp.arange((n_steps + 2) * tm, dtype=jnp.int32) - tm
    row_expert = jnp.sum(((row // tm)[:, None] >= tile_end[None, :]).astype(jnp.int32), axis=1)
    pick = (jnp.minimum(row_expert, N_EXPERTS - 1)[:, None] == experts).astype(jnp.int32)
    rank = row - jnp.sum(pick * starts[None, :], axis=1)
    real = (row >= 0) & (row_expert < N_EXPERTS) & (rank < jnp.sum(pick * counts[None, :], axis=1))
    sorted_pos = jnp.clip(jnp.sum(pick * before[None, :], axis=1) + rank, 0, n_slots - 1)
    slot = order[sorted_pos]
    token = slot // TOP_K
    row_token = jnp.where(real, token, 0)
    row_slot = jnp.where(real, (slot % TOP_K) * ALL_TOKENS + token, n_slots + row % tm)
    return row_token.astype(jnp.int32), row_slot.astype(jnp.int32), step_expert.astype(jnp.int32), step_valid


def _narrow_kernel(wide_ref, narrow_ref):
    narrow_ref[...] = wide_ref[...].astype(BF16)


def _narrow(w):
    n_steps = NARROW_STEPS
    slab = pl.BlockSpec((w.shape[0] // n_steps, w.shape[1]), lambda i: (i, 0))
    return pl.pallas_call(
        _narrow_kernel,
        grid=(n_steps,),
        in_specs=[slab],
        out_specs=slab,
        out_shape=jax.ShapeDtypeStruct(w.shape, BF16),
        compiler_params=_params(("parallel",)),
        name="moe_weight_narrow",
    )(w)


def _combine_kernel(x_ref, route_ref, mod_ref, g_ref, y0_ref, y1_ref, *out_refs, final):
    route = route_ref[...]
    tm = x_ref.shape[0]
    f = (route[:, ROUTE_W_LANE:ROUTE_W_LANE + 1] * _load_rows_from_tiles(y0_ref, (), tm)
         + route[:, ROUTE_W_LANE + 1:ROUTE_W_LANE + 2] * _load_rows_from_tiles(y1_ref, (), tm))
    x = x_ref[...] + mod_ref[0][5:6] * f
    if not final:
        out_refs[0][...] = x
        return
    ms = jnp.mean(x * x, axis=-1, keepdims=True)
    x = x * lax.rsqrt(ms + EPS) * g_ref[...]
    is_ctx = _is_ctx_tile()
    @pl.when(is_ctx)
    def _():
        out_refs[0][...] = x

    @pl.when(jnp.logical_not(is_ctx))
    def _():
        out_refs[1][...] = x


def _combine(x_all, y_slots, route, mod, final_g, final):
    tm = ROW_TILE
    n_tiles = ALL_TOKENS // tm
    row = lambda w: pl.BlockSpec((tm, w), lambda i: (i, 0))
    if final:
        out_specs = list(_stream_specs(D_MODEL))
        out_shape = [jax.ShapeDtypeStruct((CTX_TOKENS, D_MODEL), F32),
                     jax.ShapeDtypeStruct((DEC_TOKENS, D_MODEL), F32)]
    else:
        out_specs = [row(D_MODEL)]
        out_shape = [jax.ShapeDtypeStruct((ALL_TOKENS, D_MODEL), F32)]
    return pl.pallas_call(
        functools.partial(_combine_kernel, final=final),
        grid=(n_tiles,),
        in_specs=[row(D_MODEL), row(LANES),
                  pl.BlockSpec((1, 6, D_MODEL), lambda i: (_mod_row(i, tm), 0, 0)),
                  _resident((1, D_MODEL), lambda i: (0, 0)),
                  pl.BlockSpec((tm * ROW_CHUNKS, LANES), lambda i: (i, 0)),
                  pl.BlockSpec((tm * ROW_CHUNKS, LANES), lambda i: (n_tiles + i, 0))],
        out_specs=out_specs,
        out_shape=out_shape,
        compiler_params=_params(("arbitrary",)),
        name="moe_combine",
    )(x_all, route, mod, final_g, y_slots, y_slots)


def _rope_tables():
    rows = DEC_SEQ // GRID_W
    row = jnp.repeat(jnp.arange(rows, dtype=F32), GRID_W)
    col = jnp.tile(jnp.arange(GRID_W, dtype=F32), rows)
    inv = ROPE_THETA ** (-jnp.arange(0, AXIS_DIM, 2, dtype=F32) / AXIS_DIM)
    ang = jnp.concatenate([row[:, None] * inv, col[:, None] * inv], axis=-1)
    cos, sin = jnp.cos(ang), jnp.sin(ang)
    reps = LANES // HEAD_DIM
    cos_l = jnp.tile(jnp.concatenate([cos, cos], axis=-1), (1, reps))
    sin_l = jnp.tile(jnp.concatenate([-sin, sin], axis=-1), (1, reps))
    return cos_l, sin_l


def kernel(x_prompt, x_sample, cache_k, cache_v, state_ssm, c, c_ctx, w_ada, b_ada, norm1_g, norm2_g, w_in, q_norm_g, k_norm_g, ssm_a_re, ssm_a_im, ssm_log_dt, ssm_b_re, ssm_b_im, ssm_c_re, ssm_c_im, ssm_d, w_glu, w_attn_br, w_out, ffn_w_gate, ffn_w_up, ffn_w_down, moe_router, moe_w_gate, moe_w_up, moe_w_down, final_norm_g):
    P = dict(ssm_a_re=ssm_a_re, ssm_a_im=ssm_a_im, ssm_log_dt=ssm_log_dt, ssm_b_re=ssm_b_re,
             ssm_b_im=ssm_b_im, ssm_c_re=ssm_c_re, ssm_c_im=ssm_c_im)

    c_rows = jnp.zeros((SUBLANES, D_MODEL), F32).at[0].set(c_ctx).at[1:1 + DEC_BATCH].set(c)
    mod_all = _ada_modulation(c_rows, w_ada, b_ada)
    mod_all = mod_all[:, :N_MOD_ROWS].reshape(DEPTH, N_MOD_ROWS, 6, D_MODEL)

    rope_cos, rope_sin = _rope_tables()
    head_id = jnp.arange(MXU_WIDTH, dtype=jnp.int32) // HEAD_DIM
    ones_bd = (head_id[:, None] == head_id[None, :]).astype(BF16)

    cache_k4 = cache_k.reshape(DEC_BATCH, DEPTH, PAST_LEN, KV_W)
    cache_v4 = cache_v.reshape(DEC_BATCH, DEPTH, PAST_LEN, KV_W)
    chain = DEC_SEQ // S5_PIECE
    h0_ctx = jnp.zeros((N_DIRS, S5_N_SETS, BATCH, 2 * S5_SET_ST), F32)
    ks, vs, ss = [], [], []
    x_all = None
    assert DEPTH == 2
    flat2 = lambda w: w.reshape(-1, w.shape[-1])
    riders = [flat2(moe_w_gate[0]), flat2(moe_w_up[0])]
    narrowed = []

    for l in range(DEPTH):
        mod = mod_all[l]
        first = x_all is None
        x_streams = ((x_prompt.reshape(CTX_TOKENS, D_MODEL), x_sample.reshape(DEC_TOKENS, D_MODEL), 0) if first
                     else (x_all, x_all, N_CTX_TILES))
        u, q, k_ctx, v_ctx, k_dec, v_dec, sgs, sga, *x_copy = _in_projection(
            *x_streams, mod, norm1_g[l][None, :], w_in[l].astype(BF16), ones_bd,
            jnp.tile(q_norm_g[l], N_HEADS)[None, :], jnp.tile(k_norm_g[l], N_KV_HEADS)[None, :],
            rope_cos, rope_sin, first)
        if first:
            x_all = x_copy[0]

        attn_ctx = _attention(q, k_ctx, v_ctx, BATCH, SEQ, 0, "attention_context")
        attn_dec = _attention(q, k_dec, v_dec, DEC_BATCH, DEC_SEQ, CTX_TOKENS, "attention_latent",
                              cache=(cache_k4, cache_v4, l))
        ks.append(k_ctx.reshape(BATCH, SEQ, N_KV_HEADS, HEAD_DIM))
        vs.append(v_ctx.reshape(BATCH, SEQ, N_KV_HEADS, HEAD_DIM))

        bd, a_rows, cd = _s5_operators(l, P)
        d_skip = ssm_d[l][None, :]
        rider = riders.pop(0) if riders else None
        y_ctx, fin, *narrowed_now = _s5_branch(u, 0, CTX_TOKENS, bd, a_rows, cd, h0_ctx, d_skip, 1, "s5_context",
                                               rider=rider)
        narrowed += narrowed_now
        h0_dec = state_ssm[:, l].reshape(DEC_BATCH, N_DIRS, S5_N_SETS, S5_GROUP_SET * STATE_N, 2)
        h0_dec = h0_dec.transpose(1, 2, 0, 4, 3).reshape(N_DIRS, S5_N_SETS, DEC_BATCH, 2 * S5_SET_ST)
        h0_dec = jnp.repeat(h0_dec, chain, axis=2)
        y_dec, _ = _s5_branch(u, CTX_TOKENS, DEC_TOKENS, bd, a_rows, cd, h0_dec, d_skip, chain, "s5_latent")
        fin = fin.reshape(N_DIRS, S5_N_SETS, BATCH, 2, S5_GROUP_SET, STATE_N)
        ss.append(fin.transpose(2, 0, 1, 4, 5, 3).reshape(BATCH, N_DIRS, N_GROUPS, STATE_N, 2))

        x_all = _mixer_out(attn_ctx, attn_dec, y_ctx, y_dec, sgs, sga, x_all, mod, w_attn_br[l].astype(BF16),
                           w_glu[l].astype(BF16), w_out[l].astype(BF16))

        i = l // 2
        last_layer = l == DEPTH - 1
        if l % 2 == 0:
            out = _dense_ffn(x_all, mod, norm2_g[l][None, :], ffn_w_gate[i].astype(BF16), ffn_w_up[i].astype(BF16),
                             ffn_w_down[i].astype(BF16), final_norm_g[None, :], last_layer)
            x_all = out
            streams = (out[:CTX_TOKENS], out[CTX_TOKENS:])
        else:
            router_pad = jnp.pad(moe_router[i].astype(F32), ((0, 0), (0, LANES - N_EXPERTS)))
            h2, route = _prenorm_router(x_all, mod, norm2_g[l][None, :], router_pad)
            row_token, row_slot, expert, valid = _route_plan(route)
            wg, wu = (w.reshape(N_EXPERTS, D_MODEL, D_FF_EXPERT) for w in narrowed)
            wd = _narrow(flat2(moe_w_down[i])).reshape(N_EXPERTS, D_FF_EXPERT, D_MODEL)
            y_slots = _moe_ffn(h2, row_token, row_slot, wg, wu, wd, expert, valid, D_FF_EXPERT // 2)
            out = _combine(x_all, y_slots, route, mod, final_norm_g[None, :], last_layer)
            if last_layer:
                streams = tuple(out)
            else:
                x_all = out[0]
                streams = (x_all[:CTX_TOKENS], x_all[CTX_TOKENS:])

    y_prompt = streams[0].reshape(BATCH, SEQ, D_MODEL)
    y_sample = streams[1].reshape(DEC_BATCH, DEC_SEQ, D_MODEL)
    return (y_prompt, y_sample, jnp.stack(ks, axis=1), jnp.stack(vs, axis=1), jnp.stack(ss, axis=1))
```

```python
import functools
import math

import jax
import jax.numpy as jnp
from jax import lax
from jax.experimental import pallas as pl
from jax.experimental.pallas import tpu as pltpu

F32 = jnp.float32
BF16 = jnp.bfloat16

D_MODEL = 1024
BATCH = 32
SEQ = 256
DEPTH = 2
DEC_BATCH = 2
DEC_SEQ = 1024
PAST_LEN = 256
GRID_W = 64
HEAD_DIM = 64
N_HEADS = 16
N_KV_HEADS = 4
ROPE_THETA = 10000.0
AXIS_DIM = HEAD_DIM // 2
HALF = HEAD_DIM // 2
SSM_GROUP = 16
SSM_WIDTH = 512
N_GROUPS = SSM_WIDTH // SSM_GROUP
STATE_N = 64
N_DIRS = 2
D_FF = 2816
N_EXPERTS = 8
D_FF_EXPERT = 3584
EPS = 1e-6
Q_W = N_HEADS * HEAD_DIM
KV_W = N_KV_HEADS * HEAD_DIM
IN_W = SSM_WIDTH + Q_W + 2 * KV_W + 2 * D_MODEL

CTX_TOKENS = BATCH * SEQ
DEC_TOKENS = DEC_BATCH * DEC_SEQ
ALL_TOKENS = CTX_TOKENS + DEC_TOKENS

LANES = 128
MXU_WIDTH = 256
FF_CHUNK = MXU_WIDTH
SUBLANES = 8
VMEM_LIMIT_BYTES = 56 * 1024 * 1024

ROW_TILE = 512
ATTN_Q_TILE = 256
S5_GROUP_SET = 8
S5_SET_CH = S5_GROUP_SET * SSM_GROUP
S5_SET_ST = S5_GROUP_SET * STATE_N
S5_N_SETS = N_GROUPS // S5_GROUP_SET
S5_SCAN_UNROLL = 4
S5_PIECE = SEQ
N_MOD_ROWS = 1 + DEC_BATCH


def _params(sem):
    return pltpu.CompilerParams(dimension_semantics=sem, vmem_limit_bytes=VMEM_LIMIT_BYTES)


def _resident(shape, index_map):
    return pl.BlockSpec(shape, index_map, pipeline_mode=pl.Buffered(1))


def _sigmoid(x):
    return 1.0 / (1.0 + jnp.exp(-x))


def _mod_row(i, tm):
    nctx = CTX_TOKENS // tm
    per_b = DEC_SEQ // tm
    return jnp.where(i < nctx, 0, 1 + (i - nctx) // per_b)


def _rope_block(i, tm):
    nctx = CTX_TOKENS // tm
    per_b = DEC_SEQ // tm
    return jnp.maximum(i - nctx, 0) % per_b


N_CTX_TILES = CTX_TOKENS // ROW_TILE


def _stream_specs(width, dec_first_block=0):
    return (pl.BlockSpec((ROW_TILE, width), lambda i, *_: (jnp.minimum(i, N_CTX_TILES - 1), 0)),
            pl.BlockSpec((ROW_TILE, width), lambda i, *_: (dec_first_block + jnp.maximum(i - N_CTX_TILES, 0), 0)))


def _is_ctx_tile():
    return pl.program_id(0) < N_CTX_TILES


def _pick_stream(ctx_ref, dec_ref):
    return jnp.where(_is_ctx_tile(), ctx_ref[...], dec_ref[...])


def _rms_modulate(x, g, scale, shift):
    ms = jnp.mean(x * x, axis=-1, keepdims=True)
    return (x * lax.rsqrt(ms + EPS) * g) * (1.0 + scale) + shift


def _ada_kernel(c_ref, w_ref, b_ref, o_ref):
    c = c_ref[...]
    act = c * _sigmoid(c)
    o_ref[0] = jnp.dot(act.astype(BF16), w_ref[0].astype(BF16), preferred_element_type=F32) + b_ref[0]


def _ada_modulation(c_rows, w_ada, b_ada):
    tn = 1536
    return pl.pallas_call(
        _ada_kernel,
        grid=(DEPTH, 6 * D_MODEL // tn),
        in_specs=[
            pl.BlockSpec((SUBLANES, D_MODEL), lambda l, n: (0, 0)),
            pl.BlockSpec((1, D_MODEL, tn), lambda l, n: (l, 0, n)),
            pl.BlockSpec((1, 1, tn), lambda l, n: (l, 0, n)),
        ],
        out_specs=pl.BlockSpec((1, SUBLANES, tn), lambda l, n: (l, 0, n)),
        out_shape=jax.ShapeDtypeStruct((DEPTH, SUBLANES, 6 * D_MODEL), F32),
        compiler_params=_params(("parallel", "parallel")),
        name="ada_modulation",
    )(c_rows, w_ada, b_ada.reshape(DEPTH, 1, 6 * D_MODEL))


def _rotate_pairs(x, cos, sin_signed):
    n = x.shape[-1]
    lane = lax.broadcasted_iota(jnp.int32, x.shape, 1)
    first_half = (lane % HEAD_DIM) < HALF
    partner = jnp.where(first_half, pltpu.roll(x, n - HALF, 1), pltpu.roll(x, HALF, 1))
    reps = n // LANES
    return x * jnp.tile(cos, (1, reps)) + partner * jnp.tile(sin_signed, (1, reps))


def _inproj_kernel(x_ctx_ref, x_dec_ref, mod_ref, g_ref, w_ref, ones_ref, qg_ref, kg_ref, cos_ref, sin_ref,
                   u_ref, q_ref, kc_ref, vc_ref, kd_ref, vd_ref, sgs_ref, sga_ref, *x_copy_ref):
    mod = mod_ref[0]
    x = _pick_stream(x_ctx_ref, x_dec_ref)
    if x_copy_ref:
        x_copy_ref[0][...] = x
    h = _rms_modulate(x, g_ref[...], mod[1:2], mod[0:1]).astype(BF16)

    def proj(lo, width):
        return jnp.dot(h, w_ref[:, lo:lo + width], preferred_element_type=F32)

    def head_norm(t, gain):
        sq = (t * t).astype(BF16)
        ss = jnp.concatenate(
            [jnp.dot(sq[:, c:c + MXU_WIDTH], ones_ref[...], preferred_element_type=F32)
             for c in range(0, t.shape[-1], MXU_WIDTH)], axis=-1)
        return t * lax.rsqrt(ss * (1.0 / HEAD_DIM) + EPS) * gain

    o = 0
    u_ref[...] = proj(o, SSM_WIDTH)
    o += SSM_WIDTH
    q = head_norm(proj(o, Q_W), qg_ref[...]) * (1.0 / math.sqrt(HEAD_DIM))
    o += Q_W
    k = head_norm(proj(o, KV_W), kg_ref[...])
    o += KV_W
    v = proj(o, KV_W)
    o += KV_W
    sgs_ref[...] = _sigmoid(proj(o, D_MODEL))
    o += D_MODEL
    sga_ref[...] = _sigmoid(proj(o, D_MODEL))

    @pl.when(_is_ctx_tile())
    def _():
        q_ref[...] = q.astype(BF16)
        kc_ref[...] = k
        vc_ref[...] = v

    @pl.when(jnp.logical_not(_is_ctx_tile()))
    def _():
        cos = cos_ref[...]
        sin = sin_ref[...]
        q_ref[...] = _rotate_pairs(q, cos, sin).astype(BF16)
        kd_ref[...] = _rotate_pairs(k, cos, sin)
        vd_ref[...] = v


def _in_projection(x_ctx, x_dec, dec_first_block, mod, norm_g, w_in_bf16, ones_bd, q_gain, k_gain,
                   rope_cos, rope_sin, copy_x):
    tm = ROW_TILE
    n_tiles = ALL_TOKENS // tm
    row = lambda w: pl.BlockSpec((tm, w), lambda i: (i, 0))
    kv_ctx, kv_dec = _stream_specs(KV_W)
    out_specs = [row(SSM_WIDTH), row(Q_W), kv_ctx, kv_ctx, kv_dec, kv_dec, row(D_MODEL), row(D_MODEL)]
    out_shape = [jax.ShapeDtypeStruct((ALL_TOKENS, SSM_WIDTH), F32), jax.ShapeDtypeStruct((ALL_TOKENS, Q_W), BF16),
                 jax.ShapeDtypeStruct((CTX_TOKENS, KV_W), F32), jax.ShapeDtypeStruct((CTX_TOKENS, KV_W), F32),
                 jax.ShapeDtypeStruct((DEC_TOKENS, KV_W), F32), jax.ShapeDtypeStruct((DEC_TOKENS, KV_W), F32),
                 jax.ShapeDtypeStruct((ALL_TOKENS, D_MODEL), F32), jax.ShapeDtypeStruct((ALL_TOKENS, D_MODEL), F32)]
    if copy_x:
        out_specs.append(row(D_MODEL))
        out_shape.append(jax.ShapeDtypeStruct((ALL_TOKENS, D_MODEL), F32))
    return pl.pallas_call(
        _inproj_kernel,
        grid=(n_tiles,),
        in_specs=[
            *_stream_specs(D_MODEL, dec_first_block),
            pl.BlockSpec((1, 6, D_MODEL), lambda i: (_mod_row(i, tm), 0, 0)),
            _resident((1, D_MODEL), lambda i: (0, 0)),
            _resident((D_MODEL, IN_W), lambda i: (0, 0)),
            _resident((MXU_WIDTH, MXU_WIDTH), lambda i: (0, 0)),
            _resident((1, Q_W), lambda i: (0, 0)),
            _resident((1, KV_W), lambda i: (0, 0)),
            pl.BlockSpec((tm, LANES), lambda i: (_rope_block(i, tm), 0)),
            pl.BlockSpec((tm, LANES), lambda i: (_rope_block(i, tm), 0)),
        ],
        out_specs=out_specs,
        out_shape=out_shape,
        compiler_params=_params(("arbitrary",)),
        name="in_projection",
    )(x_ctx, x_dec, mod, norm_g, w_in_bf16, ones_bd, q_gain, k_gain, rope_cos, rope_sin)


RIDER_ROWS = 1024


def _rider_specs(riders, n_steps, linear_step):
    specs = [pl.BlockSpec((w.shape[0] // n_steps, w.shape[1]), lambda *idx: (linear_step(*idx), 0)) for w in riders]
    return specs, [jax.ShapeDtypeStruct(w.shape, BF16) for w in riders]


def _narrow_riders(wide_refs, narrow_refs):
    for wide_ref, narrow_ref in zip(wide_refs, narrow_refs):
        narrow_ref[...] = wide_ref[...].astype(BF16)


def _attn_kernel(*refs, with_cache, n_riders):
    if n_riders:
        n_in = len(refs) - 1 - 2 * n_riders
        _narrow_riders(refs[n_in:n_in + n_riders], refs[n_in + n_riders + 1:])
        refs = refs[:n_in] + refs[n_in + n_riders:n_in + n_riders + 1]
    if with_cache:
        q_ref, k_ref, v_ref, ck_ref, cv_ref, o_ref = refs
        k_parts = (lambda sl: ck_ref[0, 0, :, sl], lambda sl: k_ref[0, :, sl])
        v_parts = (lambda sl: cv_ref[0, 0, :, sl], lambda sl: v_ref[0, :, sl])
    else:
        q_ref, k_ref, v_ref, o_ref = refs
        k_parts = (lambda sl: k_ref[0, :, sl],)
        v_parts = (lambda sl: v_ref[0, :, sl],)
    lane = lax.broadcasted_iota(jnp.int32, (1, LANES), 1)
    low = lane < HEAD_DIM
    groups_per_slab = LANES // HEAD_DIM
    rep = N_HEADS // N_KV_HEADS
    for g in range(N_KV_HEADS):
        slab = g // groups_per_slab
        own_low = (g % groups_per_slab) == 0
        keep = low if own_low else jnp.logical_not(low)

        def halves(parts):
            lanes = slice(slab * LANES, (slab + 1) * LANES)
            rows = jnp.concatenate([part(lanes) for part in parts], axis=0)
            same = jnp.where(keep, rows, 0.0)
            swapped = pltpu.roll(same, HEAD_DIM, 1)
            lo, hi = (same, swapped) if own_low else (swapped, same)
            return lo.astype(BF16), hi.astype(BF16)

        k_lo, k_hi = halves(k_parts)
        v_lo, v_hi = halves(v_parts)
        for j in range(g * rep // groups_per_slab, (g + 1) * rep // groups_per_slab):
            qs = q_ref[0, :, j * LANES:(j + 1) * LANES]
            acc = None
            for kk, vv in ((k_lo, v_lo), (k_hi, v_hi)):
                s = lax.dot_general(qs, kk, (((1,), (1,)), ((), ())), preferred_element_type=F32)
                p = jnp.exp(s - jnp.max(s, axis=-1, keepdims=True))
                denom = jnp.sum(p, axis=-1, keepdims=True)
                part = jnp.dot(p.astype(BF16), vv, preferred_element_type=F32) / denom
                acc = part if acc is None else acc + part
            o_ref[:, j * LANES:(j + 1) * LANES] = acc.astype(BF16)


def _attention(q, k, v, n_batch, seq_len, first_token, name, cache=None, riders=()):
    tq = ATTN_Q_TILE
    per_b = seq_len // tq
    q_tiles = q.reshape(ALL_TOKENS // tq, tq, Q_W)
    first_tile = first_token // tq
    kv_spec = pl.BlockSpec((1, seq_len, KV_W), lambda b, i: (b, 0, 0))
    in_specs = [pl.BlockSpec((1, tq, Q_W), lambda b, i: (first_tile + b * per_b + i, 0, 0)), kv_spec, kv_spec]
    args = [q_tiles, k.reshape(n_batch, seq_len, KV_W), v.reshape(n_batch, seq_len, KV_W)]
    if cache is not None:
        cache_k, cache_v, layer = cache
        cache_spec = pl.BlockSpec((1, 1, PAST_LEN, KV_W), lambda b, i: (b, layer, 0, 0))
        in_specs += [cache_spec, cache_spec]
        args += [cache_k, cache_v]
    rider_specs, rider_shapes = _rider_specs(riders, n_batch * per_b, lambda b, i: b * per_b + i)
    return pl.pallas_call(
        functools.partial(_attn_kernel, with_cache=cache is not None, n_riders=len(riders)),
        grid=(n_batch, per_b),
        in_specs=in_specs + rider_specs,
        out_specs=[pl.BlockSpec((tq, Q_W), lambda b, i: (b * per_b + i, 0))] + rider_specs,
        out_shape=[jax.ShapeDtypeStruct((n_batch * seq_len, Q_W), BF16)] + rider_shapes,
        compiler_params=_params(("parallel", "parallel")),
        name=name,
    )(*args, *riders)


def _s5_kernel(u_ref, bd_ref, a_ref, cd_ref, h0_ref, dskip_ref, *rest, chain, n_riders):
    wide_refs, (y_ref, hfin_ref), rest = rest[:n_riders], rest[n_riders:n_riders + 2], rest[n_riders + 2:]
    narrow_refs, (u_scr, y_scr, bu_scr, hs_scr) = rest[:n_riders], rest[n_riders:]
    _narrow_riders(wide_refs, narrow_refs)
    _s5_body(u_ref, bd_ref, a_ref, cd_ref, h0_ref, dskip_ref, y_ref, hfin_ref, u_scr, y_scr, bu_scr, hs_scr, chain)


def _s5_body(u_ref, bd_ref, a_ref, cd_ref, h0_ref, dskip_ref, y_ref, hfin_ref,
             u_scr, y_scr, bu_scr, hs_scr, chain):
    seq_len = u_ref.shape[0] // SUBLANES
    forward = pl.program_id(2) == 0
    last_dir = pl.program_id(2) == pl.num_programs(2) - 1
    a_re = jnp.broadcast_to(a_ref[0, 0, :, :S5_SET_ST], (SUBLANES, S5_SET_ST))
    a_im = jnp.broadcast_to(a_ref[0, 0, :, S5_SET_ST:], (SUBLANES, S5_SET_ST))

    @pl.when(forward)
    def _():
        for p in range(SUBLANES):
            u_scr[pl.ds(p, seq_len, stride=SUBLANES), :] = u_ref[p * seq_len:(p + 1) * seq_len, :]
        y_scr[...] = u_scr[...] * dskip_ref[...]

    bu_scr[...] = jnp.dot(u_scr[...].astype(BF16), bd_ref[0, 0], preferred_element_type=F32)

    def scan(init, store):
        def step(t, hc):
            h_re, h_im = hc
            tt = jnp.where(forward, t, seq_len - 1 - t)
            r = pl.multiple_of(tt * SUBLANES, SUBLANES)
            n_re = a_re * h_re - a_im * h_im + bu_scr[pl.ds(r, SUBLANES), :S5_SET_ST]
            n_im = a_re * h_im + a_im * h_re + bu_scr[pl.ds(r, SUBLANES), S5_SET_ST:]
            if store:
                hs_scr[pl.ds(r, SUBLANES), :S5_SET_ST] = n_re
                hs_scr[pl.ds(r, SUBLANES), S5_SET_ST:] = n_im
            return n_re, n_im

        return lax.fori_loop(0, seq_len, step, init, unroll=S5_SCAN_UNROLL)

    h0 = h0_ref[0, 0]
    init = (h0[:, :S5_SET_ST], h0[:, S5_SET_ST:])
    if chain > 1:
        zero = jnp.zeros((SUBLANES, S5_SET_ST), F32)
        end_re, end_im = scan((zero, zero), store=False)
        p_re, p_im = a_re, a_im
        for _ in range(seq_len.bit_length() - 1):
            p_re, p_im = p_re * p_re - p_im * p_im, 2.0 * (p_re * p_im)
        piece = lax.broadcasted_iota(jnp.int32, (SUBLANES, S5_SET_ST), 0) % chain
        entry = piece == jnp.where(forward, 0, chain - 1)

        def from_neighbour(x):
            return jnp.where(forward, pltpu.roll(x, 1, 0), pltpu.roll(x, SUBLANES - 1, 0))

        i_re = jnp.where(entry, init[0], 0.0)
        i_im = jnp.where(entry, init[1], 0.0)
        for _ in range(chain - 1):
            o_re = p_re * i_re - p_im * i_im + end_re
            o_im = p_re * i_im + p_im * i_re + end_im
            i_re = jnp.where(entry, init[0], from_neighbour(o_re))
            i_im = jnp.where(entry, init[1], from_neighbour(o_im))
        init = (i_re, i_im)
    h_re, h_im = scan(init, store=True)
    hfin_ref[0, 0, :, :S5_SET_ST] = h_re
    hfin_ref[0, 0, :, S5_SET_ST:] = h_im
    y_scr[...] += jnp.dot(hs_scr[...].astype(BF16), cd_ref[0, 0], preferred_element_type=F32)

    @pl.when(last_dir)
    def _():
        for p in range(SUBLANES):
            y_ref[p * seq_len:(p + 1) * seq_len, :] = y_scr[pl.ds(p, seq_len, stride=SUBLANES), :]


def _s5_branch(u_all, first_token, n_tokens, bd, a_bar, cd, h0, d_skip, chain, name, riders=()):
    seq_len = S5_PIECE
    rows = seq_len * SUBLANES
    nb = n_tokens // seq_len
    first_block = first_token // rows
    assert seq_len & (seq_len - 1) == 0 and SUBLANES % chain == 0 and first_token % rows == 0
    st2 = 2 * S5_SET_ST
    par = lambda shape: pl.BlockSpec(shape, lambda b, s, d: (d, s, 0, 0))
    state = pl.BlockSpec((1, 1, SUBLANES, st2), lambda b, s, d: (d, s, b, 0))
    seq = pl.BlockSpec((rows, S5_SET_CH), lambda b, s, d: (b, s))
    grid = (nb // SUBLANES, S5_N_SETS, N_DIRS)
    in_specs = [
        pl.BlockSpec((rows, S5_SET_CH), lambda b, s, d: (first_block + b, s)),
        par((1, 1, S5_SET_CH, st2)),
        par((1, 1, 1, st2)),
        par((1, 1, st2, S5_SET_CH)),
        state,
        pl.BlockSpec((1, S5_SET_CH), lambda b, s, d: (0, s)),
    ]
    out_specs = [seq, state]
    out_shape = [jax.ShapeDtypeStruct((n_tokens, SSM_WIDTH), F32),
                 jax.ShapeDtypeStruct((N_DIRS, S5_N_SETS, nb, st2), F32)]
    args = [u_all, bd, a_bar, cd, h0, d_skip, *riders]
    rider_specs, rider_shapes = _rider_specs(riders, grid[0] * grid[1] * grid[2],
                                             lambda b, s, d: (b * S5_N_SETS + s) * N_DIRS + d)
    return pl.pallas_call(
        functools.partial(_s5_kernel, chain=chain, n_riders=len(riders)),
        grid=grid,
        in_specs=in_specs + rider_specs,
        out_specs=out_specs + rider_specs,
        out_shape=out_shape + rider_shapes,
        scratch_shapes=[pltpu.VMEM((rows, S5_SET_CH), F32), pltpu.VMEM((rows, S5_SET_CH), F32),
                        pltpu.VMEM((rows, st2), F32), pltpu.VMEM((rows, st2), F32)],
        compiler_params=_params(("parallel", "parallel", "arbitrary")),
        name=name,
    )(*args)


def _s5_operators(l, P):
    a_re, a_im = P['ssm_a_re'][l].astype(F32), P['ssm_a_im'][l].astype(F32)
    dt = jnp.exp(P['ssm_log_dt'][l].astype(F32))[..., None]
    mag = jnp.exp(a_re * dt)
    ab_re, ab_im = mag * jnp.cos(a_im * dt), mag * jnp.sin(a_im * dt)
    den = a_re * a_re + a_im * a_im
    k_re = ((ab_re - 1.0) * a_re + ab_im * a_im) / den
    k_im = (ab_im * a_re - (ab_re - 1.0) * a_im) / den
    b_re, b_im = P['ssm_b_re'][l].astype(F32), P['ssm_b_im'][l].astype(F32)
    bb_re = k_re[..., None] * b_re - k_im[..., None] * b_im
    bb_im = k_re[..., None] * b_im + k_im[..., None] * b_re
    eye = jnp.eye(S5_GROUP_SET, dtype=F32)

    def sets(t):
        return t.reshape(N_DIRS, S5_N_SETS, S5_GROUP_SET, *t.shape[2:])

    def b_operator(part):
        return jnp.einsum('dsgnp,gh->dsgphn', part, eye).reshape(N_DIRS, S5_N_SETS, S5_SET_CH, S5_SET_ST)

    def c_operator(part):
        return jnp.einsum('dsgpn,gh->dshngp', part, eye).reshape(N_DIRS, S5_N_SETS, S5_SET_ST, S5_SET_CH)

    bd = jnp.concatenate([b_operator(sets(bb_re)), b_operator(sets(bb_im))], axis=-1).astype(BF16)
    c_re = sets(P['ssm_c_re'][l].astype(F32))
    c_im = sets(P['ssm_c_im'][l].astype(F32))
    cd = jnp.concatenate([c_operator(c_re), -c_operator(c_im)], axis=-2).astype(BF16)
    row = lambda t: sets(t).reshape(N_DIRS, S5_N_SETS, 1, S5_SET_ST)
    a_rows = jnp.concatenate([row(ab_re), row(ab_im)], axis=-1)
    return bd, a_rows, cd


def _mixer_out_kernel(attn_ctx_ref, attn_dec_ref, y_ctx_ref, y_dec_ref, sgs_ref, sga_ref, x_ref, mod_ref,
                      wab_ref, wglu_ref, wout_ref, o_ref):
    attn_br = jnp.dot(_pick_stream(attn_ctx_ref, attn_dec_ref), wab_ref[...], preferred_element_type=F32)
    y = _pick_stream(y_ctx_ref, y_dec_ref)
    gelu = 0.5 * y * (1.0 + jnp.tanh(math.sqrt(2.0 / math.pi) * (y + 0.044715 * (y * y * y))))
    glu = jnp.dot(gelu.astype(BF16), wglu_ref[...], preferred_element_type=F32)
    ssm_br = glu[:, :D_MODEL] * _sigmoid(glu[:, D_MODEL:])
    merged = sgs_ref[...] * ssm_br + sga_ref[...] * attn_br
    out = jnp.dot(merged.astype(BF16), wout_ref[...], preferred_element_type=F32)
    o_ref[...] = x_ref[...] + mod_ref[0][2:3] * out


def _mixer_out(attn_ctx, attn_dec, y_ctx, y_dec, sgs, sga, x_all, mod, w_attn_br, w_glu, w_out):
    tm = ROW_TILE
    row = lambda w: pl.BlockSpec((tm, w), lambda i: (i, 0))
    return pl.pallas_call(
        _mixer_out_kernel,
        grid=(ALL_TOKENS // tm,),
        in_specs=[
            *_stream_specs(Q_W), *_stream_specs(SSM_WIDTH),
            row(D_MODEL), row(D_MODEL), row(D_MODEL),
            pl.BlockSpec((1, 6, D_MODEL), lambda i: (_mod_row(i, tm), 0, 0)),
            _resident((Q_W, D_MODEL), lambda i: (0, 0)),
            _resident((SSM_WIDTH, 2 * D_MODEL), lambda i: (0, 0)),
            _resident((D_MODEL, D_MODEL), lambda i: (0, 0)),
        ],
        out_specs=row(D_MODEL),
        out_shape=jax.ShapeDtypeStruct((ALL_TOKENS, D_MODEL), F32),
        compiler_params=_params(("parallel",)),
        name="mixer_out",
    )(attn_ctx, attn_dec, y_ctx, y_dec, sgs, sga, x_all, mod, w_attn_br, w_glu, w_out)


ROUTE_IDX_LANE = 0
ROUTE_W_LANE = 2


ROW_CHUNKS = D_MODEL // LANES


def _store_rows_as_tiles(ref, lead, value):
    n = value.shape[0]
    for c in range(ROW_CHUNKS):
        ref[(*lead, pl.ds(c, n, stride=ROW_CHUNKS), slice(None))] = value[:, c * LANES:(c + 1) * LANES]


def _load_rows_from_tiles(ref, lead, n, first=0, every=1):
    stride = every * ROW_CHUNKS
    return jnp.concatenate(
        [ref[(*lead, pl.ds(first * ROW_CHUNKS + c, n, stride=stride), slice(None))] for c in range(ROW_CHUNKS)],
        axis=-1)


def _router_kernel(x_ref, mod_ref, g_ref, r_ref, h_ref, route_ref):
    mod = mod_ref[0]
    h = _rms_modulate(x_ref[...], g_ref[...], mod[4:5], mod[3:4])
    _store_rows_as_tiles(h_ref, (), h)
    r = r_ref[...]
    h_hi, r_hi = h.astype(BF16), r.astype(BF16)
    h_lo = (h - h_hi.astype(F32)).astype(BF16)
    r_lo = (r - r_hi.astype(F32)).astype(BF16)
    logits = (jnp.dot(h_hi, r_hi, preferred_element_type=F32) + jnp.dot(h_lo, r_hi, preferred_element_type=F32)
              + jnp.dot(h_hi, r_lo, preferred_element_type=F32))
    lane = lax.broadcasted_iota(jnp.int32, logits.shape, 1).astype(F32)
    logits = jnp.where(lane < N_EXPERTS, logits, -jnp.inf)
    top1 = jnp.max(logits, axis=-1, keepdims=True)
    idx1 = jnp.min(jnp.where(logits == top1, lane, float(LANES)), axis=-1, keepdims=True)
    rest = jnp.where(lane == idx1, -jnp.inf, logits)
    top2 = jnp.max(rest, axis=-1, keepdims=True)
    idx2 = jnp.min(jnp.where(rest == top2, lane, float(LANES)), axis=-1, keepdims=True)
    e2 = jnp.exp(top2 - top1)
    w1 = 1.0 / (1.0 + e2)
    w2 = e2 / (1.0 + e2)
    route = jnp.where(lane == ROUTE_IDX_LANE, idx1, 0.0)
    route = jnp.where(lane == ROUTE_IDX_LANE + 1, idx2, route)
    route = jnp.where(lane == ROUTE_W_LANE, w1, route)
    route_ref[...] = jnp.where(lane == ROUTE_W_LANE + 1, w2, route)


def _prenorm_router(x_all, mod, norm_g, router_pad):
    tm = ROW_TILE
    row = lambda w: pl.BlockSpec((tm, w), lambda i: (i, 0))
    in_specs = [row(D_MODEL),
                pl.BlockSpec((1, 6, D_MODEL), lambda i: (_mod_row(i, tm), 0, 0)),
                _resident((1, D_MODEL), lambda i: (0, 0))]
    return pl.pallas_call(
        _router_kernel, grid=(ALL_TOKENS // tm,),
        in_specs=in_specs + [_resident((D_MODEL, LANES), lambda i: (0, 0))],
        out_specs=[pl.BlockSpec((tm * ROW_CHUNKS, LANES), lambda i: (i, 0)), row(LANES)],
        out_shape=[jax.ShapeDtypeStruct((ALL_TOKENS * ROW_CHUNKS, LANES), F32),
                   jax.ShapeDtypeStruct((ALL_TOKENS, LANES), F32)],
        compiler_params=_params(("parallel",)), name="prenorm_router",
    )(x_all, mod, norm_g, router_pad)


def _swiglu_chains(h, gate_cols, up_cols, down_rows, d_ff):
    out = None
    for c in range(d_ff // FF_CHUNK):
        cols = slice(c * FF_CHUNK, (c + 1) * FF_CHUNK)
        a = jnp.dot(h, gate_cols(cols), preferred_element_type=F32)
        b = jnp.dot(h, up_cols(cols), preferred_element_type=F32)
        inter = ((a * _sigmoid(a)) * b).astype(BF16)
        part = jnp.dot(inter, down_rows(cols), preferred_element_type=F32)
        out = part if out is None else out + part
    return out


def _dense_ffn_kernel(x_ref, mod_ref, g_ref, wg_ref, wu_ref, wd_ref, fg_ref, o_ref, *, final):
    mod = mod_ref[0]
    x = x_ref[...]
    h = _rms_modulate(x, g_ref[...], mod[4:5], mod[3:4]).astype(BF16)
    f = _swiglu_chains(h, lambda c: wg_ref[:, c], lambda c: wu_ref[:, c], lambda c: wd_ref[c, :],
                       wg_ref.shape[-1])
    x = x + mod[5:6] * f
    if final:
        ms = jnp.mean(x * x, axis=-1, keepdims=True)
        x = x * lax.rsqrt(ms + EPS) * fg_ref[...]
    o_ref[...] = x


def _dense_ffn(x_all, mod, norm_g, w_gate, w_up, w_down, final_g, final):
    tm = ROW_TILE
    d_ff = w_gate.shape[-1]
    row = pl.BlockSpec((tm, D_MODEL), lambda i: (i, 0))
    return pl.pallas_call(
        functools.partial(_dense_ffn_kernel, final=final),
        grid=(ALL_TOKENS // tm,),
        in_specs=[row,
                  pl.BlockSpec((1, 6, D_MODEL), lambda i: (_mod_row(i, tm), 0, 0)),
                  _resident((1, D_MODEL), lambda i: (0, 0)),
                  _resident((D_MODEL, d_ff), lambda i: (0, 0)),
                  _resident((D_MODEL, d_ff), lambda i: (0, 0)),
                  _resident((d_ff, D_MODEL), lambda i: (0, 0)),
                  _resident((1, D_MODEL), lambda i: (0, 0))],
        out_specs=row,
        out_shape=jax.ShapeDtypeStruct((ALL_TOKENS, D_MODEL), F32),
        compiler_params=_params(("parallel",)),
        name="ffn_dense",
    )(x_all, mod, norm_g, w_gate, w_up, w_down, final_g)


def _moe_ffn_kernel(tok_ref, slot_ref, expert_ref, valid_ref, h_ref, wg_ref, wu_ref, wd_ref, y_ref,
                    xbuf, ybuf, acc, sems):
    s = pl.program_id(0)
    j = pl.program_id(1)
    n_j = pl.num_programs(1)
    tm = acc.shape[0]
    share = tm // n_j
    cur = s % 2
    other = 1 - cur
    gather_sem, scatter_sem = sems.at[0], sems.at[1]

    def tile_of(row):
        return pl.ds(pl.multiple_of(row * ROW_CHUNKS, ROW_CHUNKS), ROW_CHUNKS)

    def whole(n_rows):
        return pl.ds(0, n_rows * ROW_CHUNKS)

    @pl.when(jnp.logical_and(s == 0, j == 0))
    def _():
        ybuf[...] = jnp.zeros_like(ybuf)

        def first_tile(r, carry):
            pltpu.make_async_copy(h_ref.at[tile_of(tok_ref[tm + r])], xbuf.at[0, tile_of(r)], gather_sem).start()
            return carry

        lax.fori_loop(0, tm, first_tile, 0)
        pltpu.make_async_copy(h_ref.at[whole(tm)], xbuf.at[0], gather_sem).wait()
        spill = pltpu.make_async_copy(
            ybuf.at[0], y_ref.at[pl.ds(TOP_K * ALL_TOKENS * ROW_CHUNKS, tm * ROW_CHUNKS)], scatter_sem)
        spill.start()
        spill.wait()

    def start_row_copies():
        row0 = j * share
        for r in range(share):
            row = row0 + r
            pltpu.make_async_copy(h_ref.at[tile_of(tok_ref[(s + 2) * tm + row])], xbuf.at[other, tile_of(row)],
                                  gather_sem).start()
            pltpu.make_async_copy(ybuf.at[other, tile_of(row)], y_ref.at[tile_of(slot_ref[s * tm + row])],
                                  scatter_sem).start()

    def drain_row_copies():
        pltpu.make_async_copy(h_ref.at[whole(share)], xbuf.at[0, whole(share)], gather_sem).wait()
        pltpu.make_async_copy(ybuf.at[0, whole(share)], y_ref.at[whole(share)], scatter_sem).wait()

    @pl.when(s >= 0)
    def _():
        start_row_copies()

    @pl.when(valid_ref[s] == 1)
    def _():
        h = _load_rows_from_tiles(xbuf, (cur,), tm).astype(BF16)
        contrib = _swiglu_chains(h, lambda c: wg_ref[0, :, c], lambda c: wu_ref[0, :, c],
                                 lambda c: wd_ref[0, c, :], wg_ref.shape[-1])

        @pl.when(j == 0)
        def _():
            acc[...] = contrib

        @pl.when(jnp.logical_and(j != 0, j != n_j - 1))
        def _():
            acc[...] += contrib

        @pl.when(j == n_j - 1)
        def _():
            _store_rows_as_tiles(ybuf, (cur,), acc[...] + contrib)

    drain_row_copies()


def _moe_ffn(h_tiles, row_token, row_slot, w_gate, w_up, w_down, step_expert, step_valid, tf):
    tm = MOE_ROW_TILE
    d_ff = w_gate.shape[-1]
    n_steps = step_expert.shape[0]
    n_ff = d_ff // tf
    assert n_ff >= 2
    ff = lambda s, j, f: jnp.where(f[s] == 1, j, n_ff - 1)
    grid_spec = pltpu.PrefetchScalarGridSpec(
        num_scalar_prefetch=4,
        grid=(n_steps, n_ff),
        in_specs=[
            pl.BlockSpec(memory_space=pl.ANY),
            pl.BlockSpec((1, D_MODEL, tf), lambda s, j, t, d, e, f: (e[s], 0, ff(s, j, f))),
            pl.BlockSpec((1, D_MODEL, tf), lambda s, j, t, d, e, f: (e[s], 0, ff(s, j, f))),
            pl.BlockSpec((1, tf, D_MODEL), lambda s, j, t, d, e, f: (e[s], ff(s, j, f), 0)),
        ],
        out_specs=pl.BlockSpec(memory_space=pl.ANY),
        scratch_shapes=[pltpu.VMEM((2, tm * ROW_CHUNKS, LANES), F32), pltpu.VMEM((2, tm * ROW_CHUNKS, LANES), F32),
                        pltpu.VMEM((tm, D_MODEL), F32), pltpu.SemaphoreType.DMA((2,))],
    )
    return pl.pallas_call(
        _moe_ffn_kernel,
        grid_spec=grid_spec,
        out_shape=jax.ShapeDtypeStruct(((TOP_K * ALL_TOKENS + tm) * ROW_CHUNKS, LANES), F32),
        compiler_params=_params(("arbitrary", "arbitrary")),
        name="ffn_experts",
    )(row_token, row_slot, step_expert, step_valid, h_tiles, w_gate, w_up, w_down)


TOP_K = 2
MOE_ROW_TILE = 512
MOE_ROWS = TOP_K * ALL_TOKENS + N_EXPERTS * MOE_ROW_TILE


def _route_plan(route):
    tm = MOE_ROW_TILE
    n_slots = TOP_K * ALL_TOKENS
    n_steps = MOE_ROWS // tm + 1
    expert = route[:, ROUTE_IDX_LANE:ROUTE_IDX_LANE + TOP_K].astype(jnp.int32).reshape(-1)
    experts = jnp.arange(N_EXPERTS, dtype=jnp.int32)[None, :]
    order = jnp.argsort(expert, stable=True).astype(jnp.int32)
    counts = jnp.sum((expert[:, None] == experts).astype(jnp.int32), axis=0)
    tiles = (counts + tm - 1) // tm
    tile_end = jnp.cumsum(tiles)
    starts = (tile_end - tiles) * tm
    before = jnp.cumsum(counts) - counts
    last = tile_end[-1] - 1
    step = jnp.arange(n_steps, dtype=jnp.int32)
    step_expert = jnp.sum((jnp.minimum(step, last)[:, None] >= tile_end[None, :]).astype(jnp.int32), axis=1)
    step_valid = (step <= last).astype(jnp.int32)
    row = jnp.arange((n_steps + 2) * tm, dtype=jnp.int32) - tm
    row_expert = jnp.sum(((row // tm)[:, None] >= tile_end[None, :]).astype(jnp.int32), axis=1)
    pick = (jnp.minimum(row_expert, N_EXPERTS - 1)[:, None] == experts).astype(jnp.int32)
    rank = row - jnp.sum(pick * starts[None, :], axis=1)
    real = (row >= 0) & (row_expert < N_EXPERTS) & (rank < jnp.sum(pick * counts[None, :], axis=1))
    sorted_pos = jnp.clip(jnp.sum(pick * before[None, :], axis=1) + rank, 0, n_slots - 1)
    slot = order[sorted_pos]
    token = slot // TOP_K
    row_token = jnp.where(real, token, 0)
    row_slot = jnp.where(real, (slot % TOP_K) * ALL_TOKENS + token, n_slots + row % tm)
    return row_token.astype(jnp.int32), row_slot.astype(jnp.int32), step_expert.astype(jnp.int32), step_valid


def _combine_kernel(x_ref, route_ref, mod_ref, g_ref, y0_ref, y1_ref, *out_refs, final):
    route = route_ref[...]
    tm = x_ref.shape[0]
    f = (route[:, ROUTE_W_LANE:ROUTE_W_LANE + 1] * _load_rows_from_tiles(y0_ref, (), tm)
         + route[:, ROUTE_W_LANE + 1:ROUTE_W_LANE + 2] * _load_rows_from_tiles(y1_ref, (), tm))
    x = x_ref[...] + mod_ref[0][5:6] * f
    if not final:
        out_refs[0][...] = x
        return
    ms = jnp.mean(x * x, axis=-1, keepdims=True)
    x = x * lax.rsqrt(ms + EPS) * g_ref[...]
    is_ctx = _is_ctx_tile()
    @pl.when(is_ctx)
    def _():
        out_refs[0][...] = x

    @pl.when(jnp.logical_not(is_ctx))
    def _():
        out_refs[1][...] = x


def _combine(x_all, y_slots, route, mod, final_g, final):
    tm = ROW_TILE
    n_tiles = ALL_TOKENS // tm
    row = lambda w: pl.BlockSpec((tm, w), lambda i: (i, 0))
    if final:
        out_specs = list(_stream_specs(D_MODEL))
        out_shape = [jax.ShapeDtypeStruct((CTX_TOKENS, D_MODEL), F32),
                     jax.ShapeDtypeStruct((DEC_TOKENS, D_MODEL), F32)]
    else:
        out_specs = [row(D_MODEL)]
        out_shape = [jax.ShapeDtypeStruct((ALL_TOKENS, D_MODEL), F32)]
    return pl.pallas_call(
        functools.partial(_combine_kernel, final=final),
        grid=(n_tiles,),
        in_specs=[row(D_MODEL), row(LANES),
                  pl.BlockSpec((1, 6, D_MODEL), lambda i: (_mod_row(i, tm), 0, 0)),
                  _resident((1, D_MODEL), lambda i: (0, 0)),
                  pl.BlockSpec((tm * ROW_CHUNKS, LANES), lambda i: (i, 0)),
                  pl.BlockSpec((tm * ROW_CHUNKS, LANES), lambda i: (n_tiles + i, 0))],
        out_specs=out_specs,
        out_shape=out_shape,
        compiler_params=_params(("arbitrary",)),
        name="moe_combine",
    )(x_all, route, mod, final_g, y_slots, y_slots)


def _rope_tables():
    rows = DEC_SEQ // GRID_W
    row = jnp.repeat(jnp.arange(rows, dtype=F32), GRID_W)
    col = jnp.tile(jnp.arange(GRID_W, dtype=F32), rows)
    inv = ROPE_THETA ** (-jnp.arange(0, AXIS_DIM, 2, dtype=F32) / AXIS_DIM)
    ang = jnp.concatenate([row[:, None] * inv, col[:, None] * inv], axis=-1)
    cos, sin = jnp.cos(ang), jnp.sin(ang)
    reps = LANES // HEAD_DIM
    cos_l = jnp.tile(jnp.concatenate([cos, cos], axis=-1), (1, reps))
    sin_l = jnp.tile(jnp.concatenate([-sin, sin], axis=-1), (1, reps))
    return cos_l, sin_l


def kernel(x_prompt, x_sample, cache_k, cache_v, state_ssm, c, c_ctx, w_ada, b_ada, norm1_g, norm2_g, w_in, q_norm_g, k_norm_g, ssm_a_re, ssm_a_im, ssm_log_dt, ssm_b_re, ssm_b_im, ssm_c_re, ssm_c_im, ssm_d, w_glu, w_attn_br, w_out, ffn_w_gate, ffn_w_up, ffn_w_down, moe_router, moe_w_gate, moe_w_up, moe_w_down, final_norm_g):
    P = dict(ssm_a_re=ssm_a_re, ssm_a_im=ssm_a_im, ssm_log_dt=ssm_log_dt, ssm_b_re=ssm_b_re,
             ssm_b_im=ssm_b_im, ssm_c_re=ssm_c_re, ssm_c_im=ssm_c_im)

    c_rows = jnp.zeros((SUBLANES, D_MODEL), F32).at[0].set(c_ctx).at[1:1 + DEC_BATCH].set(c)
    mod_all = _ada_modulation(c_rows, w_ada, b_ada)
    mod_all = mod_all[:, :N_MOD_ROWS].reshape(DEPTH, N_MOD_ROWS, 6, D_MODEL)

    rope_cos, rope_sin = _rope_tables()
    head_id = jnp.arange(MXU_WIDTH, dtype=jnp.int32) // HEAD_DIM
    ones_bd = (head_id[:, None] == head_id[None, :]).astype(BF16)

    cache_k4 = cache_k.reshape(DEC_BATCH, DEPTH, PAST_LEN, KV_W)
    cache_v4 = cache_v.reshape(DEC_BATCH, DEPTH, PAST_LEN, KV_W)
    chain = DEC_SEQ // S5_PIECE
    h0_ctx = jnp.zeros((N_DIRS, S5_N_SETS, BATCH, 2 * S5_SET_ST), F32)
    ks, vs, ss = [], [], []
    x_all = None
    assert DEPTH == 2
    rider_view = lambda w: w.reshape(RIDER_ROWS, -1)
    hosted = {
        (0, "attention"): dict(ffn_gate=ffn_w_gate[0], ffn_up=ffn_w_up[0], ffn_down=ffn_w_down[0], w_in=w_in[1],
                               w_attn_br=w_attn_br[1], w_glu=w_glu[1], w_out=w_out[1]),
        (0, "scan"): dict(moe_gate=moe_w_gate[0]),
        (1, "attention"): dict(moe_down=moe_w_down[0]),
        (1, "scan"): dict(moe_up=moe_w_up[0]),
    }
    narrowed = dict(w_in=w_in[0].astype(BF16), w_attn_br=w_attn_br[0].astype(BF16), w_glu=w_glu[0].astype(BF16),
                    w_out=w_out[0].astype(BF16))

    def collect(host, results):
        for (name, wide), narrow in zip(hosted[host].items(), results):
            narrowed[name] = narrow.reshape(wide.shape)

    for l in range(DEPTH):
        mod = mod_all[l]
        first = x_all is None
        x_streams = ((x_prompt.reshape(CTX_TOKENS, D_MODEL), x_sample.reshape(DEC_TOKENS, D_MODEL), 0) if first
                     else (x_all, x_all, N_CTX_TILES))
        u, q, k_ctx, v_ctx, k_dec, v_dec, sgs, sga, *x_copy = _in_projection(
            *x_streams, mod, norm1_g[l][None, :], narrowed["w_in"], ones_bd,
            jnp.tile(q_norm_g[l], N_HEADS)[None, :], jnp.tile(k_norm_g[l], N_KV_HEADS)[None, :],
            rope_cos, rope_sin, first)
        if first:
            x_all = x_copy[0]
        mixer_weights = [narrowed[name] for name in ("w_attn_br", "w_glu", "w_out")]

        attn_ctx, *riders_done = _attention(q, k_ctx, v_ctx, BATCH, SEQ, 0, "attention_context",
                                            riders=[rider_view(w) for w in hosted[(l, "attention")].values()])
        collect((l, "attention"), riders_done)
        attn_dec, = _attention(q, k_dec, v_dec, DEC_BATCH, DEC_SEQ, CTX_TOKENS, "attention_latent",
                               cache=(cache_k4, cache_v4, l))
        ks.append(k_ctx.reshape(BATCH, SEQ, N_KV_HEADS, HEAD_DIM))
        vs.append(v_ctx.reshape(BATCH, SEQ, N_KV_HEADS, HEAD_DIM))

        bd, a_rows, cd = _s5_operators(l, P)
        d_skip = ssm_d[l][None, :]
        y_ctx, fin, *riders_done = _s5_branch(u, 0, CTX_TOKENS, bd, a_rows, cd, h0_ctx, d_skip, 1, "s5_context",
                                              riders=[rider_view(w) for w in hosted[(l, "scan")].values()])
        collect((l, "scan"), riders_done)
        h0_dec = state_ssm[:, l].reshape(DEC_BATCH, N_DIRS, S5_N_SETS, S5_GROUP_SET * STATE_N, 2)
        h0_dec = h0_dec.transpose(1, 2, 0, 4, 3).reshape(N_DIRS, S5_N_SETS, DEC_BATCH, 2 * S5_SET_ST)
        h0_dec = jnp.repeat(h0_dec, chain, axis=2)
        y_dec, _ = _s5_branch(u, CTX_TOKENS, DEC_TOKENS, bd, a_rows, cd, h0_dec, d_skip, chain, "s5_latent")
        fin = fin.reshape(N_DIRS, S5_N_SETS, BATCH, 2, S5_GROUP_SET, STATE_N)
        ss.append(fin.transpose(2, 0, 1, 4, 5, 3).reshape(BATCH, N_DIRS, N_GROUPS, STATE_N, 2))

        x_all = _mixer_out(attn_ctx, attn_dec, y_ctx, y_dec, sgs, sga, x_all, mod, *mixer_weights)

        i = l // 2
        last_layer = l == DEPTH - 1
        if l % 2 == 0:
            out = _dense_ffn(x_all, mod, norm2_g[l][None, :], narrowed["ffn_gate"], narrowed["ffn_up"],
                             narrowed["ffn_down"], final_norm_g[None, :], last_layer)
            x_all = out
            streams = (out[:CTX_TOKENS], out[CTX_TOKENS:])
        else:
            router_pad = jnp.pad(moe_router[i].astype(F32), ((0, 0), (0, LANES - N_EXPERTS)))
            h2, route = _prenorm_router(x_all, mod, norm2_g[l][None, :], router_pad)
            row_token, row_slot, expert, valid = _route_plan(route)
            y_slots = _moe_ffn(h2, row_token, row_slot, narrowed["moe_gate"], narrowed["moe_up"],
                               narrowed["moe_down"], expert, valid, D_FF_EXPERT // 2)
            out = _combine(x_all, y_slots, route, mod, final_norm_g[None, :], last_layer)
            if last_layer:
                streams = tuple(out)
            else:
                x_all = out[0]
                streams = (x_all[:CTX_TOKENS], x_all[CTX_TOKENS:])

    y_prompt = streams[0].reshape(BATCH, SEQ, D_MODEL)
    y_sample = streams[1].reshape(DEC_BATCH, DEC_SEQ, D_MODEL)
    return (y_prompt, y_sample, jnp.stack(ks, axis=1), jnp.stack(vs, axis=1), jnp.stack(ss, axis=1))
```

```python
import functools
import math

import jax
import jax.numpy as jnp
from jax import lax
from jax.experimental import pallas as pl
from jax.experimental.pallas import tpu as pltpu

F32 = jnp.float32
BF16 = jnp.bfloat16

D_MODEL = 1024
BATCH = 32
SEQ = 256
DEPTH = 2
DEC_BATCH = 2
DEC_SEQ = 1024
PAST_LEN = 256
GRID_W = 64
HEAD_DIM = 64
N_HEADS = 16
N_KV_HEADS = 4
ROPE_THETA = 10000.0
AXIS_DIM = HEAD_DIM // 2
HALF = HEAD_DIM // 2
SSM_GROUP = 16
SSM_WIDTH = 512
N_GROUPS = SSM_WIDTH // SSM_GROUP
STATE_N = 64
N_DIRS = 2
D_FF = 2816
N_EXPERTS = 8
D_FF_EXPERT = 3584
EPS = 1e-6
Q_W = N_HEADS * HEAD_DIM
KV_W = N_KV_HEADS * HEAD_DIM
IN_W = SSM_WIDTH + Q_W + 2 * KV_W + 2 * D_MODEL

CTX_TOKENS = BATCH * SEQ
DEC_TOKENS = DEC_BATCH * DEC_SEQ
ALL_TOKENS = CTX_TOKENS + DEC_TOKENS

LANES = 128
MXU_WIDTH = 256
FF_CHUNK = MXU_WIDTH
SUBLANES = 8
VMEM_LIMIT_BYTES = 56 * 1024 * 1024

ROW_TILE = 512
ATTN_Q_TILE = 256
S5_GROUP_SET = 8
S5_SET_CH = S5_GROUP_SET * SSM_GROUP
S5_SET_ST = S5_GROUP_SET * STATE_N
S5_N_SETS = N_GROUPS // S5_GROUP_SET
S5_SCAN_UNROLL = 4
S5_PIECE = SEQ
N_MOD_ROWS = 1 + DEC_BATCH


def _params(sem):
    return pltpu.CompilerParams(dimension_semantics=sem, vmem_limit_bytes=VMEM_LIMIT_BYTES)


def _resident(shape, index_map):
    return pl.BlockSpec(shape, index_map, pipeline_mode=pl.Buffered(1))


def _sigmoid(x):
    return 1.0 / (1.0 + jnp.exp(-x))


def _mod_row(i, tm):
    nctx = CTX_TOKENS // tm
    per_b = DEC_SEQ // tm
    return jnp.where(i < nctx, 0, 1 + (i - nctx) // per_b)


def _rope_block(i, tm):
    nctx = CTX_TOKENS // tm
    per_b = DEC_SEQ // tm
    return jnp.maximum(i - nctx, 0) % per_b


N_CTX_TILES = CTX_TOKENS // ROW_TILE


def _stream_specs(width, dec_first_block=0):
    return (pl.BlockSpec((ROW_TILE, width), lambda i, *_: (jnp.minimum(i, N_CTX_TILES - 1), 0)),
            pl.BlockSpec((ROW_TILE, width), lambda i, *_: (dec_first_block + jnp.maximum(i - N_CTX_TILES, 0), 0)))


def _is_ctx_tile():
    return pl.program_id(0) < N_CTX_TILES


def _pick_stream(ctx_ref, dec_ref):
    return jnp.where(_is_ctx_tile(), ctx_ref[...], dec_ref[...])


def _rms_modulate(x, g, scale, shift):
    ms = jnp.mean(x * x, axis=-1, keepdims=True)
    return (x * lax.rsqrt(ms + EPS) * g) * (1.0 + scale) + shift


def _ada_kernel(c_ref, w_ref, b_ref, o_ref):
    c = c_ref[...]
    act = c * _sigmoid(c)
    o_ref[0] = jnp.dot(act.astype(BF16), w_ref[0].astype(BF16), preferred_element_type=F32) + b_ref[0]


def _ada_modulation(c_rows, w_ada, b_ada):
    tn = 1536
    return pl.pallas_call(
        _ada_kernel,
        grid=(DEPTH, 6 * D_MODEL // tn),
        in_specs=[
            pl.BlockSpec((SUBLANES, D_MODEL), lambda l, n: (0, 0)),
            pl.BlockSpec((1, D_MODEL, tn), lambda l, n: (l, 0, n)),
            pl.BlockSpec((1, 1, tn), lambda l, n: (l, 0, n)),
        ],
        out_specs=pl.BlockSpec((1, SUBLANES, tn), lambda l, n: (l, 0, n)),
        out_shape=jax.ShapeDtypeStruct((DEPTH, SUBLANES, 6 * D_MODEL), F32),
        compiler_params=_params(("parallel", "parallel")),
        name="ada_modulation",
    )(c_rows, w_ada, b_ada.reshape(DEPTH, 1, 6 * D_MODEL))


def _rotate_pairs(x, cos, sin_signed):
    n = x.shape[-1]
    lane = lax.broadcasted_iota(jnp.int32, x.shape, 1)
    first_half = (lane % HEAD_DIM) < HALF
    partner = jnp.where(first_half, pltpu.roll(x, n - HALF, 1), pltpu.roll(x, HALF, 1))
    reps = n // LANES
    return x * jnp.tile(cos, (1, reps)) + partner * jnp.tile(sin_signed, (1, reps))


def _inproj_kernel(x_ctx_ref, x_dec_ref, mod_ref, g_ref, w_ref, ones_ref, qg_ref, kg_ref, cos_ref, sin_ref,
                   u_ref, q_ref, kc_ref, vc_ref, kd_ref, vd_ref, sgs_ref, sga_ref, *x_copy_ref):
    mod = mod_ref[0]
    x = _pick_stream(x_ctx_ref, x_dec_ref)
    if x_copy_ref:
        x_copy_ref[0][...] = x
    h = _rms_modulate(x, g_ref[...], mod[1:2], mod[0:1]).astype(BF16)

    def proj(lo, width):
        return jnp.dot(h, w_ref[:, lo:lo + width], preferred_element_type=F32)

    def head_norm(t, gain):
        sq = (t * t).astype(BF16)
        ss = jnp.concatenate(
            [jnp.dot(sq[:, c:c + MXU_WIDTH], ones_ref[...], preferred_element_type=F32)
             for c in range(0, t.shape[-1], MXU_WIDTH)], axis=-1)
        return t * lax.rsqrt(ss * (1.0 / HEAD_DIM) + EPS) * gain

    o = 0
    u_ref[...] = proj(o, SSM_WIDTH)
    o += SSM_WIDTH
    q = head_norm(proj(o, Q_W), qg_ref[...]) * (1.0 / math.sqrt(HEAD_DIM))
    o += Q_W
    k = head_norm(proj(o, KV_W), kg_ref[...])
    o += KV_W
    v = proj(o, KV_W)
    o += KV_W
    sgs_ref[...] = _sigmoid(proj(o, D_MODEL))
    o += D_MODEL
    sga_ref[...] = _sigmoid(proj(o, D_MODEL))

    @pl.when(_is_ctx_tile())
    def _():
        q_ref[...] = q.astype(BF16)
        kc_ref[...] = k
        vc_ref[...] = v

    @pl.when(jnp.logical_not(_is_ctx_tile()))
    def _():
        cos = cos_ref[...]
        sin = sin_ref[...]
        q_ref[...] = _rotate_pairs(q, cos, sin).astype(BF16)
        kd_ref[...] = _rotate_pairs(k, cos, sin)
        vd_ref[...] = v


def _in_projection(x_ctx, x_dec, dec_first_block, mod, norm_g, w_in_bf16, ones_bd, q_gain, k_gain,
                   rope_cos, rope_sin, copy_x):
    tm = ROW_TILE
    n_tiles = ALL_TOKENS // tm
    row = lambda w: pl.BlockSpec((tm, w), lambda i: (i, 0))
    kv_ctx, kv_dec = _stream_specs(KV_W)
    out_specs = [row(SSM_WIDTH), row(Q_W), kv_ctx, kv_ctx, kv_dec, kv_dec, row(D_MODEL), row(D_MODEL)]
    out_shape = [jax.ShapeDtypeStruct((ALL_TOKENS, SSM_WIDTH), F32), jax.ShapeDtypeStruct((ALL_TOKENS, Q_W), BF16),
                 jax.ShapeDtypeStruct((CTX_TOKENS, KV_W), F32), jax.ShapeDtypeStruct((CTX_TOKENS, KV_W), F32),
                 jax.ShapeDtypeStruct((DEC_TOKENS, KV_W), F32), jax.ShapeDtypeStruct((DEC_TOKENS, KV_W), F32),
                 jax.ShapeDtypeStruct((ALL_TOKENS, D_MODEL), F32), jax.ShapeDtypeStruct((ALL_TOKENS, D_MODEL), F32)]
    if copy_x:
        out_specs.append(row(D_MODEL))
        out_shape.append(jax.ShapeDtypeStruct((ALL_TOKENS, D_MODEL), F32))
    return pl.pallas_call(
        _inproj_kernel,
        grid=(n_tiles,),
        in_specs=[
            *_stream_specs(D_MODEL, dec_first_block),
            pl.BlockSpec((1, 6, D_MODEL), lambda i: (_mod_row(i, tm), 0, 0)),
            _resident((1, D_MODEL), lambda i: (0, 0)),
            _resident((D_MODEL, IN_W), lambda i: (0, 0)),
            _resident((MXU_WIDTH, MXU_WIDTH), lambda i: (0, 0)),
            _resident((1, Q_W), lambda i: (0, 0)),
            _resident((1, KV_W), lambda i: (0, 0)),
            pl.BlockSpec((tm, LANES), lambda i: (_rope_block(i, tm), 0)),
            pl.BlockSpec((tm, LANES), lambda i: (_rope_block(i, tm), 0)),
        ],
        out_specs=out_specs,
        out_shape=out_shape,
        compiler_params=_params(("arbitrary",)),
        name="in_projection",
    )(x_ctx, x_dec, mod, norm_g, w_in_bf16, ones_bd, q_gain, k_gain, rope_cos, rope_sin)


def _rider_specs(riders, n_steps, linear_step):
    specs = [pl.BlockSpec((w.shape[0] // n_steps, w.shape[1]), lambda *idx: (linear_step(*idx), 0)) for w in riders]
    return specs, [jax.ShapeDtypeStruct(w.shape, BF16) for w in riders]


def _narrow_riders(wide_refs, narrow_refs):
    for wide_ref, narrow_ref in zip(wide_refs, narrow_refs):
        narrow_ref[...] = wide_ref[...].astype(BF16)


def _attn_kernel(*refs, with_cache, n_riders):
    if n_riders:
        n_in = len(refs) - 1 - 2 * n_riders
        _narrow_riders(refs[n_in:n_in + n_riders], refs[n_in + n_riders + 1:])
        refs = refs[:n_in] + refs[n_in + n_riders:n_in + n_riders + 1]
    if with_cache:
        q_ref, k_ref, v_ref, ck_ref, cv_ref, o_ref = refs
        k_parts = (lambda sl: ck_ref[0, 0, :, sl], lambda sl: k_ref[0, :, sl])
        v_parts = (lambda sl: cv_ref[0, 0, :, sl], lambda sl: v_ref[0, :, sl])
    else:
        q_ref, k_ref, v_ref, o_ref = refs
        k_parts = (lambda sl: k_ref[0, :, sl],)
        v_parts = (lambda sl: v_ref[0, :, sl],)
    lane = lax.broadcasted_iota(jnp.int32, (1, LANES), 1)
    low = lane < HEAD_DIM
    groups_per_slab = LANES // HEAD_DIM
    rep = N_HEADS // N_KV_HEADS
    for g in range(N_KV_HEADS):
        slab = g // groups_per_slab
        own_low = (g % groups_per_slab) == 0
        keep = low if own_low else jnp.logical_not(low)

        def halves(parts):
            lanes = slice(slab * LANES, (slab + 1) * LANES)
            rows = jnp.concatenate([part(lanes) for part in parts], axis=0)
            same = jnp.where(keep, rows, 0.0)
            swapped = pltpu.roll(same, HEAD_DIM, 1)
            lo, hi = (same, swapped) if own_low else (swapped, same)
            return lo.astype(BF16), hi.astype(BF16)

        k_lo, k_hi = halves(k_parts)
        v_lo, v_hi = halves(v_parts)
        for j in range(g * rep // groups_per_slab, (g + 1) * rep // groups_per_slab):
            qs = q_ref[0, :, j * LANES:(j + 1) * LANES]
            acc = None
            for kk, vv in ((k_lo, v_lo), (k_hi, v_hi)):
                s = lax.dot_general(qs, kk, (((1,), (1,)), ((), ())), preferred_element_type=F32)
                p = jnp.exp(s - jnp.max(s, axis=-1, keepdims=True))
                denom = jnp.sum(p, axis=-1, keepdims=True)
                part = jnp.dot(p.astype(BF16), vv, preferred_element_type=F32) / denom
                acc = part if acc is None else acc + part
            o_ref[:, j * LANES:(j + 1) * LANES] = acc.astype(BF16)


def _attention(q, k, v, n_batch, seq_len, first_token, name, cache=None, riders=()):
    tq = ATTN_Q_TILE
    per_b = seq_len // tq
    q_tiles = q.reshape(ALL_TOKENS // tq, tq, Q_W)
    first_tile = first_token // tq
    kv_spec = pl.BlockSpec((1, seq_len, KV_W), lambda b, i: (b, 0, 0))
    in_specs = [pl.BlockSpec((1, tq, Q_W), lambda b, i: (first_tile + b * per_b + i, 0, 0)), kv_spec, kv_spec]
    args = [q_tiles, k.reshape(n_batch, seq_len, KV_W), v.reshape(n_batch, seq_len, KV_W)]
    if cache is not None:
        cache_k, cache_v, layer = cache
        cache_spec = pl.BlockSpec((1, 1, PAST_LEN, KV_W), lambda b, i: (b, layer, 0, 0))
        in_specs += [cache_spec, cache_spec]
        args += [cache_k, cache_v]
    rider_specs, rider_shapes = _rider_specs(riders, n_batch * per_b, lambda b, i: b * per_b + i)
    return pl.pallas_call(
        functools.partial(_attn_kernel, with_cache=cache is not None, n_riders=len(riders)),
        grid=(n_batch, per_b),
        in_specs=in_specs + rider_specs,
        out_specs=[pl.BlockSpec((tq, Q_W), lambda b, i: (b * per_b + i, 0))] + rider_specs,
        out_shape=[jax.ShapeDtypeStruct((n_batch * seq_len, Q_W), BF16)] + rider_shapes,
        compiler_params=_params(("parallel", "parallel")),
        name=name,
    )(*args, *riders)


def _s5_kernel(u_ref, bd_ref, a_ref, cd_ref, h0_ref, dskip_ref, *rest, chain, n_riders):
    wide_refs, (y_ref, hfin_ref), rest = rest[:n_riders], rest[n_riders:n_riders + 2], rest[n_riders + 2:]
    narrow_refs, (u_scr, y_scr, bu_scr, hs_scr) = rest[:n_riders], rest[n_riders:]
    _narrow_riders(wide_refs, narrow_refs)
    _s5_body(u_ref, bd_ref, a_ref, cd_ref, h0_ref, dskip_ref, y_ref, hfin_ref, u_scr, y_scr, bu_scr, hs_scr, chain)


def _s5_body(u_ref, bd_ref, a_ref, cd_ref, h0_ref, dskip_ref, y_ref, hfin_ref,
             u_scr, y_scr, bu_scr, hs_scr, chain):
    seq_len = u_ref.shape[0] // SUBLANES
    forward = pl.program_id(2) == 0
    last_dir = pl.program_id(2) == pl.num_programs(2) - 1
    a_re = jnp.broadcast_to(a_ref[0, 0, :, :S5_SET_ST], (SUBLANES, S5_SET_ST))
    a_im = jnp.broadcast_to(a_ref[0, 0, :, S5_SET_ST:], (SUBLANES, S5_SET_ST))

    @pl.when(forward)
    def _():
        for p in range(SUBLANES):
            u_scr[pl.ds(p, seq_len, stride=SUBLANES), :] = u_ref[p * seq_len:(p + 1) * seq_len, :]
        y_scr[...] = u_scr[...] * dskip_ref[...]

    bu_scr[...] = jnp.dot(u_scr[...].astype(BF16), bd_ref[0, 0], preferred_element_type=F32)

    def scan(init, store):
        def step(t, hc):
            h_re, h_im = hc
            tt = jnp.where(forward, t, seq_len - 1 - t)
            r = pl.multiple_of(tt * SUBLANES, SUBLANES)
            n_re = a_re * h_re - a_im * h_im + bu_scr[pl.ds(r, SUBLANES), :S5_SET_ST]
            n_im = a_re * h_im + a_im * h_re + bu_scr[pl.ds(r, SUBLANES), S5_SET_ST:]
            if store:
                hs_scr[pl.ds(r, SUBLANES), :S5_SET_ST] = n_re
                hs_scr[pl.ds(r, SUBLANES), S5_SET_ST:] = n_im
            return n_re, n_im

        return lax.fori_loop(0, seq_len, step, init, unroll=S5_SCAN_UNROLL)

    h0 = h0_ref[0, 0]
    init = (h0[:, :S5_SET_ST], h0[:, S5_SET_ST:])
    if chain > 1:
        zero = jnp.zeros((SUBLANES, S5_SET_ST), F32)
        end_re, end_im = scan((zero, zero), store=False)
        p_re, p_im = a_re, a_im
        for _ in range(seq_len.bit_length() - 1):
            p_re, p_im = p_re * p_re - p_im * p_im, 2.0 * (p_re * p_im)
        piece = lax.broadcasted_iota(jnp.int32, (SUBLANES, S5_SET_ST), 0) % chain
        entry = piece == jnp.where(forward, 0, chain - 1)

        def from_neighbour(x):
            return jnp.where(forward, pltpu.roll(x, 1, 0), pltpu.roll(x, SUBLANES - 1, 0))

        i_re = jnp.where(entry, init[0], 0.0)
        i_im = jnp.where(entry, init[1], 0.0)
        for _ in range(chain - 1):
            o_re = p_re * i_re - p_im * i_im + end_re
            o_im = p_re * i_im + p_im * i_re + end_im
            i_re = jnp.where(entry, init[0], from_neighbour(o_re))
            i_im = jnp.where(entry, init[1], from_neighbour(o_im))
        init = (i_re, i_im)
    h_re, h_im = scan(init, store=True)
    hfin_ref[0, 0, :, :S5_SET_ST] = h_re
    hfin_ref[0, 0, :, S5_SET_ST:] = h_im
    y_scr[...] += jnp.dot(hs_scr[...].astype(BF16), cd_ref[0, 0], preferred_element_type=F32)

    @pl.when(last_dir)
    def _():
        for p in range(SUBLANES):
            y_ref[p * seq_len:(p + 1) * seq_len, :] = y_scr[pl.ds(p, seq_len, stride=SUBLANES), :]


def _s5_branch(u_all, first_token, n_tokens, bd, a_bar, cd, h0, d_skip, chain, name, riders=()):
    seq_len = S5_PIECE
    rows = seq_len * SUBLANES
    nb = n_tokens // seq_len
    first_block = first_token // rows
    assert seq_len & (seq_len - 1) == 0 and SUBLANES % chain == 0 and first_token % rows == 0
    st2 = 2 * S5_SET_ST
    par = lambda shape: pl.BlockSpec(shape, lambda b, s, d: (d, s, 0, 0))
    state = pl.BlockSpec((1, 1, SUBLANES, st2), lambda b, s, d: (d, s, b, 0))
    seq = pl.BlockSpec((rows, S5_SET_CH), lambda b, s, d: (b, s))
    grid = (nb // SUBLANES, S5_N_SETS, N_DIRS)
    in_specs = [
        pl.BlockSpec((rows, S5_SET_CH), lambda b, s, d: (first_block + b, s)),
        par((1, 1, S5_SET_CH, st2)),
        par((1, 1, 1, st2)),
        par((1, 1, st2, S5_SET_CH)),
        state,
        pl.BlockSpec((1, S5_SET_CH), lambda b, s, d: (0, s)),
    ]
    out_specs = [seq, state]
    out_shape = [jax.ShapeDtypeStruct((n_tokens, SSM_WIDTH), F32),
                 jax.ShapeDtypeStruct((N_DIRS, S5_N_SETS, nb, st2), F32)]
    args = [u_all, bd, a_bar, cd, h0, d_skip, *riders]
    rider_specs, rider_shapes = _rider_specs(riders, grid[0] * grid[1] * grid[2],
                                             lambda b, s, d: (b * S5_N_SETS + s) * N_DIRS + d)
    return pl.pallas_call(
        functools.partial(_s5_kernel, chain=chain, n_riders=len(riders)),
        grid=grid,
        in_specs=in_specs + rider_specs,
        out_specs=out_specs + rider_specs,
        out_shape=out_shape + rider_shapes,
        scratch_shapes=[pltpu.VMEM((rows, S5_SET_CH), F32), pltpu.VMEM((rows, S5_SET_CH), F32),
                        pltpu.VMEM((rows, st2), F32), pltpu.VMEM((rows, st2), F32)],
        compiler_params=_params(("parallel", "parallel", "arbitrary")),
        name=name,
    )(*args)


def _s5_operators(l, P):
    a_re, a_im = P['ssm_a_re'][l].astype(F32), P['ssm_a_im'][l].astype(F32)
    dt = jnp.exp(P['ssm_log_dt'][l].astype(F32))[..., None]
    mag = jnp.exp(a_re * dt)
    ab_re, ab_im = mag * jnp.cos(a_im * dt), mag * jnp.sin(a_im * dt)
    den = a_re * a_re + a_im * a_im
    k_re = ((ab_re - 1.0) * a_re + ab_im * a_im) / den
    k_im = (ab_im * a_re - (ab_re - 1.0) * a_im) / den
    b_re, b_im = P['ssm_b_re'][l].astype(F32), P['ssm_b_im'][l].astype(F32)
    bb_re = k_re[..., None] * b_re - k_im[..., None] * b_im
    bb_im = k_re[..., None] * b_im + k_im[..., None] * b_re
    eye = jnp.eye(S5_GROUP_SET, dtype=F32)

    def sets(t):
        return t.reshape(N_DIRS, S5_N_SETS, S5_GROUP_SET, *t.shape[2:])

    def b_operator(part):
        return jnp.einsum('dsgnp,gh->dsgphn', part, eye).reshape(N_DIRS, S5_N_SETS, S5_SET_CH, S5_SET_ST)

    def c_operator(part):
        return jnp.einsum('dsgpn,gh->dshngp', part, eye).reshape(N_DIRS, S5_N_SETS, S5_SET_ST, S5_SET_CH)

    bd = jnp.concatenate([b_operator(sets(bb_re)), b_operator(sets(bb_im))], axis=-1).astype(BF16)
    c_re = sets(P['ssm_c_re'][l].astype(F32))
    c_im = sets(P['ssm_c_im'][l].astype(F32))
    cd = jnp.concatenate([c_operator(c_re), -c_operator(c_im)], axis=-2).astype(BF16)
    row = lambda t: sets(t).reshape(N_DIRS, S5_N_SETS, 1, S5_SET_ST)
    a_rows = jnp.concatenate([row(ab_re), row(ab_im)], axis=-1)
    return bd, a_rows, cd


def _mixer_out_kernel(attn_ctx_ref, attn_dec_ref, y_ctx_ref, y_dec_ref, sgs_ref, sga_ref, x_ref, mod_ref,
                      wab_ref, wglu_ref, wout_ref, o_ref):
    attn_br = jnp.dot(_pick_stream(attn_ctx_ref, attn_dec_ref), wab_ref[...], preferred_element_type=F32)
    y = _pick_stream(y_ctx_ref, y_dec_ref)
    gelu = 0.5 * y * (1.0 + jnp.tanh(math.sqrt(2.0 / math.pi) * (y + 0.044715 * (y * y * y))))
    glu = jnp.dot(gelu.astype(BF16), wglu_ref[...], preferred_element_type=F32)
    ssm_br = glu[:, :D_MODEL] * _sigmoid(glu[:, D_MODEL:])
    merged = sgs_ref[...] * ssm_br + sga_ref[...] * attn_br
    out = jnp.dot(merged.astype(BF16), wout_ref[...], preferred_element_type=F32)
    o_ref[...] = x_ref[...] + mod_ref[0][2:3] * out


def _mixer_out(attn_ctx, attn_dec, y_ctx, y_dec, sgs, sga, x_all, mod, w_attn_br, w_glu, w_out):
    tm = ROW_TILE
    row = lambda w: pl.BlockSpec((tm, w), lambda i: (i, 0))
    return pl.pallas_call(
        _mixer_out_kernel,
        grid=(ALL_TOKENS // tm,),
        in_specs=[
            *_stream_specs(Q_W), *_stream_specs(SSM_WIDTH),
            row(D_MODEL), row(D_MODEL), row(D_MODEL),
            pl.BlockSpec((1, 6, D_MODEL), lambda i: (_mod_row(i, tm), 0, 0)),
            _resident((Q_W, D_MODEL), lambda i: (0, 0)),
            _resident((SSM_WIDTH, 2 * D_MODEL), lambda i: (0, 0)),
            _resident((D_MODEL, D_MODEL), lambda i: (0, 0)),
        ],
        out_specs=row(D_MODEL),
        out_shape=jax.ShapeDtypeStruct((ALL_TOKENS, D_MODEL), F32),
        compiler_params=_params(("parallel",)),
        name="mixer_out",
    )(attn_ctx, attn_dec, y_ctx, y_dec, sgs, sga, x_all, mod, w_attn_br, w_glu, w_out)


ROUTE_IDX_LANE = 0
ROUTE_W_LANE = 2


ROW_CHUNKS = D_MODEL // LANES


def _store_rows_as_tiles(ref, lead, value):
    n = value.shape[0]
    for c in range(ROW_CHUNKS):
        ref[(*lead, pl.ds(c, n, stride=ROW_CHUNKS), slice(None))] = value[:, c * LANES:(c + 1) * LANES]


def _load_rows_from_tiles(ref, lead, n, first=0, every=1):
    stride = every * ROW_CHUNKS
    return jnp.concatenate(
        [ref[(*lead, pl.ds(first * ROW_CHUNKS + c, n, stride=stride), slice(None))] for c in range(ROW_CHUNKS)],
        axis=-1)


def _router_kernel(x_ref, mod_ref, g_ref, r_ref, h_ref, route_ref):
    mod = mod_ref[0]
    h = _rms_modulate(x_ref[...], g_ref[...], mod[4:5], mod[3:4])
    _store_rows_as_tiles(h_ref, (), h)
    r = r_ref[...]
    h_hi, r_hi = h.astype(BF16), r.astype(BF16)
    h_lo = (h - h_hi.astype(F32)).astype(BF16)
    r_lo = (r - r_hi.astype(F32)).astype(BF16)
    logits = (jnp.dot(h_hi, r_hi, preferred_element_type=F32) + jnp.dot(h_lo, r_hi, preferred_element_type=F32)
              + jnp.dot(h_hi, r_lo, preferred_element_type=F32))
    lane = lax.broadcasted_iota(jnp.int32, logits.shape, 1).astype(F32)
    logits = jnp.where(lane < N_EXPERTS, logits, -jnp.inf)
    top1 = jnp.max(logits, axis=-1, keepdims=True)
    idx1 = jnp.min(jnp.where(logits == top1, lane, float(LANES)), axis=-1, keepdims=True)
    rest = jnp.where(lane == idx1, -jnp.inf, logits)
    top2 = jnp.max(rest, axis=-1, keepdims=True)
    idx2 = jnp.min(jnp.where(rest == top2, lane, float(LANES)), axis=-1, keepdims=True)
    e2 = jnp.exp(top2 - top1)
    w1 = 1.0 / (1.0 + e2)
    w2 = e2 / (1.0 + e2)
    route = jnp.where(lane == ROUTE_IDX_LANE, idx1, 0.0)
    route = jnp.where(lane == ROUTE_IDX_LANE + 1, idx2, route)
    route = jnp.where(lane == ROUTE_W_LANE, w1, route)
    route_ref[...] = jnp.where(lane == ROUTE_W_LANE + 1, w2, route)


def _prenorm_router(x_all, mod, norm_g, router_pad):
    tm = ROW_TILE
    row = lambda w: pl.BlockSpec((tm, w), lambda i: (i, 0))
    in_specs = [row(D_MODEL),
                pl.BlockSpec((1, 6, D_MODEL), lambda i: (_mod_row(i, tm), 0, 0)),
                _resident((1, D_MODEL), lambda i: (0, 0))]
    return pl.pallas_call(
        _router_kernel, grid=(ALL_TOKENS // tm,),
        in_specs=in_specs + [_resident((D_MODEL, LANES), lambda i: (0, 0))],
        out_specs=[pl.BlockSpec((tm * ROW_CHUNKS, LANES), lambda i: (i, 0)), row(LANES)],
        out_shape=[jax.ShapeDtypeStruct((ALL_TOKENS * ROW_CHUNKS, LANES), F32),
                   jax.ShapeDtypeStruct((ALL_TOKENS, LANES), F32)],
        compiler_params=_params(("parallel",)), name="prenorm_router",
    )(x_all, mod, norm_g, router_pad)


def _swiglu_chains(h, gate_cols, up_cols, down_rows, d_ff):
    out = None
    for c in range(d_ff // FF_CHUNK):
        cols = slice(c * FF_CHUNK, (c + 1) * FF_CHUNK)
        a = jnp.dot(h, gate_cols(cols), preferred_element_type=F32)
        b = jnp.dot(h, up_cols(cols), preferred_element_type=F32)
        inter = ((a * _sigmoid(a)) * b).astype(BF16)
        part = jnp.dot(inter, down_rows(cols), preferred_element_type=F32)
        out = part if out is None else out + part
    return out


def _dense_ffn_kernel(x_ref, mod_ref, g_ref, wg_ref, wu_ref, wd_ref, fg_ref, o_ref, *, final):
    mod = mod_ref[0]
    x = x_ref[...]
    h = _rms_modulate(x, g_ref[...], mod[4:5], mod[3:4]).astype(BF16)
    f = _swiglu_chains(h, lambda c: wg_ref[:, c], lambda c: wu_ref[:, c], lambda c: wd_ref[c, :],
                       wg_ref.shape[-1])
    x = x + mod[5:6] * f
    if final:
        ms = jnp.mean(x * x, axis=-1, keepdims=True)
        x = x * lax.rsqrt(ms + EPS) * fg_ref[...]
    o_ref[...] = x


def _dense_ffn(x_all, mod, norm_g, w_gate, w_up, w_down, final_g, final):
    tm = ROW_TILE
    d_ff = w_gate.shape[-1]
    row = pl.BlockSpec((tm, D_MODEL), lambda i: (i, 0))
    return pl.pallas_call(
        functools.partial(_dense_ffn_kernel, final=final),
        grid=(ALL_TOKENS // tm,),
        in_specs=[row,
                  pl.BlockSpec((1, 6, D_MODEL), lambda i: (_mod_row(i, tm), 0, 0)),
                  _resident((1, D_MODEL), lambda i: (0, 0)),
                  _resident((D_MODEL, d_ff), lambda i: (0, 0)),
                  _resident((D_MODEL, d_ff), lambda i: (0, 0)),
                  _resident((d_ff, D_MODEL), lambda i: (0, 0)),
                  _resident((1, D_MODEL), lambda i: (0, 0))],
        out_specs=row,
        out_shape=jax.ShapeDtypeStruct((ALL_TOKENS, D_MODEL), F32),
        compiler_params=_params(("parallel",)),
        name="ffn_dense",
    )(x_all, mod, norm_g, w_gate, w_up, w_down, final_g)


def _moe_ffn_kernel(tok_ref, slot_ref, expert_ref, valid_ref, h_ref, wg_ref, wu_ref, wd_ref, y_ref,
                    xbuf, ybuf, acc, sems):
    s = pl.program_id(0)
    j = pl.program_id(1)
    n_j = pl.num_programs(1)
    tm = acc.shape[0]
    share = tm // n_j
    cur = s % 2
    other = 1 - cur
    gather_sem, scatter_sem = sems.at[0], sems.at[1]

    def tile_of(row):
        return pl.ds(pl.multiple_of(row * ROW_CHUNKS, ROW_CHUNKS), ROW_CHUNKS)

    def whole(n_rows):
        return pl.ds(0, n_rows * ROW_CHUNKS)

    @pl.when(jnp.logical_and(s == 0, j == 0))
    def _():
        ybuf[...] = jnp.zeros_like(ybuf)

        def first_tile(r, carry):
            pltpu.make_async_copy(h_ref.at[tile_of(tok_ref[tm + r])], xbuf.at[0, tile_of(r)], gather_sem).start()
            return carry

        lax.fori_loop(0, tm, first_tile, 0)
        pltpu.make_async_copy(h_ref.at[whole(tm)], xbuf.at[0], gather_sem).wait()
        spill = pltpu.make_async_copy(
            ybuf.at[0], y_ref.at[pl.ds(TOP_K * ALL_TOKENS * ROW_CHUNKS, tm * ROW_CHUNKS)], scatter_sem)
        spill.start()
        spill.wait()

    def start_row_copies():
        row0 = j * share
        for r in range(share):
            row = row0 + r
            pltpu.make_async_copy(h_ref.at[tile_of(tok_ref[(s + 2) * tm + row])], xbuf.at[other, tile_of(row)],
                                  gather_sem).start()
            pltpu.make_async_copy(ybuf.at[other, tile_of(row)], y_ref.at[tile_of(slot_ref[s * tm + row])],
                                  scatter_sem).start()

    def drain_row_copies():
        pltpu.make_async_copy(h_ref.at[whole(share)], xbuf.at[0, whole(share)], gather_sem).wait()
        pltpu.make_async_copy(ybuf.at[0, whole(share)], y_ref.at[whole(share)], scatter_sem).wait()

    @pl.when(s >= 0)
    def _():
        start_row_copies()

    @pl.when(valid_ref[s] == 1)
    def _():
        h = _load_rows_from_tiles(xbuf, (cur,), tm).astype(BF16)
        contrib = _swiglu_chains(h, lambda c: wg_ref[0, :, c], lambda c: wu_ref[0, :, c],
                                 lambda c: wd_ref[0, c, :], wg_ref.shape[-1])

        @pl.when(j == 0)
        def _():
            acc[...] = contrib

        @pl.when(jnp.logical_and(j != 0, j != n_j - 1))
        def _():
            acc[...] += contrib

        @pl.when(j == n_j - 1)
        def _():
            _store_rows_as_tiles(ybuf, (cur,), acc[...] + contrib)

    drain_row_copies()


def _moe_ffn(h_tiles, row_token, row_slot, w_gate, w_up, w_down, step_expert, step_valid, tf):
    tm = MOE_ROW_TILE
    d_ff = w_gate.shape[-1]
    n_steps = step_expert.shape[0]
    n_ff = d_ff // tf
    assert n_ff >= 2
    ff = lambda s, j, f: jnp.where(f[s] == 1, j, n_ff - 1)
    grid_spec = pltpu.PrefetchScalarGridSpec(
        num_scalar_prefetch=4,
        grid=(n_steps, n_ff),
        in_specs=[
            pl.BlockSpec(memory_space=pl.ANY),
            pl.BlockSpec((1, D_MODEL, tf), lambda s, j, t, d, e, f: (e[s], 0, ff(s, j, f))),
            pl.BlockSpec((1, D_MODEL, tf), lambda s, j, t, d, e, f: (e[s], 0, ff(s, j, f))),
            pl.BlockSpec((1, tf, D_MODEL), lambda s, j, t, d, e, f: (e[s], ff(s, j, f), 0)),
        ],
        out_specs=pl.BlockSpec(memory_space=pl.ANY),
        scratch_shapes=[pltpu.VMEM((2, tm * ROW_CHUNKS, LANES), F32), pltpu.VMEM((2, tm * ROW_CHUNKS, LANES), F32),
                        pltpu.VMEM((tm, D_MODEL), F32), pltpu.SemaphoreType.DMA((2,))],
    )
    return pl.pallas_call(
        _moe_ffn_kernel,
        grid_spec=grid_spec,
        out_shape=jax.ShapeDtypeStruct(((TOP_K * ALL_TOKENS + tm) * ROW_CHUNKS, LANES), F32),
        compiler_params=_params(("arbitrary", "arbitrary")),
        name="ffn_experts",
    )(row_token, row_slot, step_expert, step_valid, h_tiles, w_gate, w_up, w_down)


TOP_K = 2
MOE_ROW_TILE = 512
MOE_ROWS = TOP_K * ALL_TOKENS + N_EXPERTS * MOE_ROW_TILE


def _route_plan(route):
    tm = MOE_ROW_TILE
    n_slots = TOP_K * ALL_TOKENS
    n_steps = MOE_ROWS // tm + 1
    expert = route[:, ROUTE_IDX_LANE:ROUTE_IDX_LANE + TOP_K].astype(jnp.int32).reshape(-1)
    experts = jnp.arange(N_EXPERTS, dtype=jnp.int32)[None, :]
    order = jnp.argsort(expert, stable=True).astype(jnp.int32)
    counts = jnp.sum((expert[:, None] == experts).astype(jnp.int32), axis=0)
    tiles = (counts + tm - 1) // tm
    tile_end = jnp.cumsum(tiles)
    starts = (tile_end - tiles) * tm
    before = jnp.cumsum(counts) - counts
    last = tile_end[-1] - 1
    step = jnp.arange(n_steps, dtype=jnp.int32)
    step_expert = jnp.sum((jnp.minimum(step, last)[:, None] >= tile_end[None, :]).astype(jnp.int32), axis=1)
    step_valid = (step <= last).astype(jnp.int32)
    row = jnp.arange((n_steps + 2) * tm, dtype=jnp.int32) - tm
    row_expert = jnp.sum(((row // tm)[:, None] >= tile_end[None, :]).astype(jnp.int32), axis=1)
    pick = (jnp.minimum(row_expert, N_EXPERTS - 1)[:, None] == experts).astype(jnp.int32)
    rank = row - jnp.sum(pick * starts[None, :], axis=1)
    real = (row >= 0) & (row_expert < N_EXPERTS) & (rank < jnp.sum(pick * counts[None, :], axis=1))
    sorted_pos = jnp.clip(jnp.sum(pick * before[None, :], axis=1) + rank, 0, n_slots - 1)
    slot = order[sorted_pos]
    token = slot // TOP_K
    row_token = jnp.where(real, token, 0)
    row_slot = jnp.where(real, (slot % TOP_K) * ALL_TOKENS + token, n_slots + row % tm)
    return row_token.astype(jnp.int32), row_slot.astype(jnp.int32), step_expert.astype(jnp.int32), step_valid


def _combine_kernel(x_ref, route_ref, mod_ref, g_ref, y0_ref, y1_ref, *out_refs, final):
    route = route_ref[...]
    tm = x_ref.shape[0]
    f = (route[:, ROUTE_W_LANE:ROUTE_W_LANE + 1] * _load_rows_from_tiles(y0_ref, (), tm)
         + route[:, ROUTE_W_LANE + 1:ROUTE_W_LANE + 2] * _load_rows_from_tiles(y1_ref, (), tm))
    x = x_ref[...] + mod_ref[0][5:6] * f
    if not final:
        out_refs[0][...] = x
        return
    ms = jnp.mean(x * x, axis=-1, keepdims=True)
    x = x * lax.rsqrt(ms + EPS) * g_ref[...]
    is_ctx = _is_ctx_tile()
    @pl.when(is_ctx)
    def _():
        out_refs[0][...] = x

    @pl.when(jnp.logical_not(is_ctx))
    def _():
        out_refs[1][...] = x


def _combine(x_all, y_slots, route, mod, final_g, final):
    tm = ROW_TILE
    n_tiles = ALL_TOKENS // tm
    row = lambda w: pl.BlockSpec((tm, w), lambda i: (i, 0))
    if final:
        out_specs = list(_stream_specs(D_MODEL))
        out_shape = [jax.ShapeDtypeStruct((CTX_TOKENS, D_MODEL), F32),
                     jax.ShapeDtypeStruct((DEC_TOKENS, D_MODEL), F32)]
    else:
        out_specs = [row(D_MODEL)]
        out_shape = [jax.ShapeDtypeStruct((ALL_TOKENS, D_MODEL), F32)]
    return pl.pallas_call(
        functools.partial(_combine_kernel, final=final),
        grid=(n_tiles,),
        in_specs=[row(D_MODEL), row(LANES),
                  pl.BlockSpec((1, 6, D_MODEL), lambda i: (_mod_row(i, tm), 0, 0)),
                  _resident((1, D_MODEL), lambda i: (0, 0)),
                  pl.BlockSpec((tm * ROW_CHUNKS, LANES), lambda i: (i, 0)),
                  pl.BlockSpec((tm * ROW_CHUNKS, LANES), lambda i: (n_tiles + i, 0))],
        out_specs=out_specs,
        out_shape=out_shape,
        compiler_params=_params(("arbitrary",)),
        name="moe_combine",
    )(x_all, route, mod, final_g, y_slots, y_slots)


def _rope_tables():
    rows = DEC_SEQ // GRID_W
    row = jnp.repeat(jnp.arange(rows, dtype=F32), GRID_W)
    col = jnp.tile(jnp.arange(GRID_W, dtype=F32), rows)
    inv = ROPE_THETA ** (-jnp.arange(0, AXIS_DIM, 2, dtype=F32) / AXIS_DIM)
    ang = jnp.concatenate([row[:, None] * inv, col[:, None] * inv], axis=-1)
    cos, sin = jnp.cos(ang), jnp.sin(ang)
    reps = LANES // HEAD_DIM
    cos_l = jnp.tile(jnp.concatenate([cos, cos], axis=-1), (1, reps))
    sin_l = jnp.tile(jnp.concatenate([-sin, sin], axis=-1), (1, reps))
    return cos_l, sin_l


def kernel(x_prompt, x_sample, cache_k, cache_v, state_ssm, c, c_ctx, w_ada, b_ada, norm1_g, norm2_g, w_in, q_norm_g, k_norm_g, ssm_a_re, ssm_a_im, ssm_log_dt, ssm_b_re, ssm_b_im, ssm_c_re, ssm_c_im, ssm_d, w_glu, w_attn_br, w_out, ffn_w_gate, ffn_w_up, ffn_w_down, moe_router, moe_w_gate, moe_w_up, moe_w_down, final_norm_g):
    P = dict(ssm_a_re=ssm_a_re, ssm_a_im=ssm_a_im, ssm_log_dt=ssm_log_dt, ssm_b_re=ssm_b_re,
             ssm_b_im=ssm_b_im, ssm_c_re=ssm_c_re, ssm_c_im=ssm_c_im)

    c_rows = jnp.zeros((SUBLANES, D_MODEL), F32).at[0].set(c_ctx).at[1:1 + DEC_BATCH].set(c)
    mod_all = _ada_modulation(c_rows, w_ada, b_ada)
    mod_all = mod_all[:, :N_MOD_ROWS].reshape(DEPTH, N_MOD_ROWS, 6, D_MODEL)

    rope_cos, rope_sin = _rope_tables()
    head_id = jnp.arange(MXU_WIDTH, dtype=jnp.int32) // HEAD_DIM
    ones_bd = (head_id[:, None] == head_id[None, :]).astype(BF16)

    cache_k4 = cache_k.reshape(DEC_BATCH, DEPTH, PAST_LEN, KV_W)
    cache_v4 = cache_v.reshape(DEC_BATCH, DEPTH, PAST_LEN, KV_W)
    chain = DEC_SEQ // S5_PIECE
    h0_ctx = jnp.zeros((N_DIRS, S5_N_SETS, BATCH, 2 * S5_SET_ST), F32)
    ks, vs, ss = [], [], []
    x_all = None
    assert DEPTH == 2
    rider_view = lambda w: w.reshape(-1, w.shape[-1])
    hosted = {
        (0, "attention"): dict(ffn_gate=ffn_w_gate[0], ffn_up=ffn_w_up[0], w_in=w_in[1],
                               w_attn_br=w_attn_br[1], w_glu=w_glu[1], w_out=w_out[1]),
        (0, "scan"): dict(moe_gate=moe_w_gate[0]),
        (1, "attention"): dict(moe_down=moe_w_down[0]),
        (1, "scan"): dict(moe_up=moe_w_up[0]),
    }
    narrowed = dict(w_in=w_in[0].astype(BF16), w_attn_br=w_attn_br[0].astype(BF16), w_glu=w_glu[0].astype(BF16),
                    w_out=w_out[0].astype(BF16), ffn_down=ffn_w_down[0].astype(BF16))

    def collect(host, results):
        for (name, wide), narrow in zip(hosted[host].items(), results):
            narrowed[name] = narrow.reshape(wide.shape)

    for l in range(DEPTH):
        mod = mod_all[l]
        first = x_all is None
        x_streams = ((x_prompt.reshape(CTX_TOKENS, D_MODEL), x_sample.reshape(DEC_TOKENS, D_MODEL), 0) if first
                     else (x_all, x_all, N_CTX_TILES))
        u, q, k_ctx, v_ctx, k_dec, v_dec, sgs, sga, *x_copy = _in_projection(
            *x_streams, mod, norm1_g[l][None, :], narrowed["w_in"], ones_bd,
            jnp.tile(q_norm_g[l], N_HEADS)[None, :], jnp.tile(k_norm_g[l], N_KV_HEADS)[None, :],
            rope_cos, rope_sin, first)
        if first:
            x_all = x_copy[0]
        mixer_weights = [narrowed[name] for name in ("w_attn_br", "w_glu", "w_out")]

        attn_ctx, *riders_done = _attention(q, k_ctx, v_ctx, BATCH, SEQ, 0, "attention_context",
                                            riders=[rider_view(w) for w in hosted[(l, "attention")].values()])
        collect((l, "attention"), riders_done)
        attn_dec, = _attention(q, k_dec, v_dec, DEC_BATCH, DEC_SEQ, CTX_TOKENS, "attention_latent",
                               cache=(cache_k4, cache_v4, l))
        ks.append(k_ctx.reshape(BATCH, SEQ, N_KV_HEADS, HEAD_DIM))
        vs.append(v_ctx.reshape(BATCH, SEQ, N_KV_HEADS, HEAD_DIM))

        bd, a_rows, cd = _s5_operators(l, P)
        d_skip = ssm_d[l][None, :]
        y_ctx, fin, *riders_done = _s5_branch(u, 0, CTX_TOKENS, bd, a_rows, cd, h0_ctx, d_skip, 1, "s5_context",
                                              riders=[rider_view(w) for w in hosted[(l, "scan")].values()])
        collect((l, "scan"), riders_done)
        h0_dec = state_ssm[:, l].reshape(DEC_BATCH, N_DIRS, S5_N_SETS, S5_GROUP_SET * STATE_N, 2)
        h0_dec = h0_dec.transpose(1, 2, 0, 4, 3).reshape(N_DIRS, S5_N_SETS, DEC_BATCH, 2 * S5_SET_ST)
        h0_dec = jnp.repeat(h0_dec, chain, axis=2)
        y_dec, _ = _s5_branch(u, CTX_TOKENS, DEC_TOKENS, bd, a_rows, cd, h0_dec, d_skip, chain, "s5_latent")
        fin = fin.reshape(N_DIRS, S5_N_SETS, BATCH, 2, S5_GROUP_SET, STATE_N)
        ss.append(fin.transpose(2, 0, 1, 4, 5, 3).reshape(BATCH, N_DIRS, N_GROUPS, STATE_N, 2))

        x_all = _mixer_out(attn_ctx, attn_dec, y_ctx, y_dec, sgs, sga, x_all, mod, *mixer_weights)

        i = l // 2
        last_layer = l == DEPTH - 1
        if l % 2 == 0:
            out = _dense_ffn(x_all, mod, norm2_g[l][None, :], narrowed["ffn_gate"], narrowed["ffn_up"],
                             narrowed["ffn_down"], final_norm_g[None, :], last_layer)
            x_all = out
            streams = (out[:CTX_TOKENS], out[CTX_TOKENS:])
        else:
            router_pad = jnp.pad(moe_router[i].astype(F32), ((0, 0), (0, LANES - N_EXPERTS)))
            h2, route = _prenorm_router(x_all, mod, norm2_g[l][None, :], router_pad)
            row_token, row_slot, expert, valid = _route_plan(route)
            y_slots = _moe_ffn(h2, row_token, row_slot, narrowed["moe_gate"], narrowed["moe_up"],
                               narrowed["moe_down"], expert, valid, D_FF_EXPERT // 2)
            out = _combine(x_all, y_slots, route, mod, final_norm_g[None, :], last_layer)
            if last_layer:
                streams = tuple(out)
            else:
                x_all = out[0]
                streams = (x_all[:CTX_TOKENS], x_all[CTX_TOKENS:])

    y_prompt = streams[0].reshape(BATCH, SEQ, D_MODEL)
    y_sample = streams[1].reshape(DEC_BATCH, DEC_SEQ, D_MODEL)
    return (y_prompt, y_sample, jnp.stack(ks, axis=1), jnp.stack(vs, axis=1), jnp.stack(ss, axis=1))
```

```python
import functools
import math

import jax
import jax.numpy as jnp
from jax import lax
from jax.experimental import pallas as pl
from jax.experimental.pallas import tpu as pltpu

F32 = jnp.float32
BF16 = jnp.bfloat16

D_MODEL = 1024
BATCH = 32
SEQ = 256
DEPTH = 2
DEC_BATCH = 2
DEC_SEQ = 1024
PAST_LEN = 256
GRID_W = 64
HEAD_DIM = 64
N_HEADS = 16
N_KV_HEADS = 4
ROPE_THETA = 10000.0
AXIS_DIM = HEAD_DIM // 2
HALF = HEAD_DIM // 2
SSM_GROUP = 16
SSM_WIDTH = 512
N_GROUPS = SSM_WIDTH // SSM_GROUP
STATE_N = 64
N_DIRS = 2
D_FF = 2816
N_EXPERTS = 8
D_FF_EXPERT = 3584
EPS = 1e-6
Q_W = N_HEADS * HEAD_DIM
KV_W = N_KV_HEADS * HEAD_DIM
IN_W = SSM_WIDTH + Q_W + 2 * KV_W + 2 * D_MODEL

CTX_TOKENS = BATCH * SEQ
DEC_TOKENS = DEC_BATCH * DEC_SEQ
ALL_TOKENS = CTX_TOKENS + DEC_TOKENS

LANES = 128
MXU_WIDTH = 256
FF_CHUNK = MXU_WIDTH
SUBLANES = 8
VMEM_LIMIT_BYTES = 56 * 1024 * 1024

ROW_TILE = 512
ATTN_Q_TILE = 256
S5_GROUP_SET = 8
S5_SET_CH = S5_GROUP_SET * SSM_GROUP
S5_SET_ST = S5_GROUP_SET * STATE_N
S5_N_SETS = N_GROUPS // S5_GROUP_SET
S5_SCAN_UNROLL = 4
S5_PIECE = SEQ
N_MOD_ROWS = 1 + DEC_BATCH


def _params(sem):
    return pltpu.CompilerParams(dimension_semantics=sem, vmem_limit_bytes=VMEM_LIMIT_BYTES)


def _resident(shape, index_map):
    return pl.BlockSpec(shape, index_map, pipeline_mode=pl.Buffered(1))


def _sigmoid(x):
    return 1.0 / (1.0 + jnp.exp(-x))


def _mod_row(i, tm):
    nctx = CTX_TOKENS // tm
    per_b = DEC_SEQ // tm
    return jnp.where(i < nctx, 0, 1 + (i - nctx) // per_b)


def _rope_block(i, tm):
    nctx = CTX_TOKENS // tm
    per_b = DEC_SEQ // tm
    return jnp.maximum(i - nctx, 0) % per_b


N_CTX_TILES = CTX_TOKENS // ROW_TILE


def _stream_specs(width, dec_first_block=0):
    return (pl.BlockSpec((ROW_TILE, width), lambda i, *_: (jnp.minimum(i, N_CTX_TILES - 1), 0)),
            pl.BlockSpec((ROW_TILE, width), lambda i, *_: (dec_first_block + jnp.maximum(i - N_CTX_TILES, 0), 0)))


def _is_ctx_tile():
    return pl.program_id(0) < N_CTX_TILES


def _pick_stream(ctx_ref, dec_ref):
    return jnp.where(_is_ctx_tile(), ctx_ref[...], dec_ref[...])


def _rms_modulate(x, g, scale, shift):
    ms = jnp.mean(x * x, axis=-1, keepdims=True)
    return (x * lax.rsqrt(ms + EPS) * g) * (1.0 + scale) + shift


def _ada_kernel(c_ref, w_ref, b_ref, o_ref):
    c = c_ref[...]
    act = c * _sigmoid(c)
    o_ref[0] = jnp.dot(act.astype(BF16), w_ref[0].astype(BF16), preferred_element_type=F32) + b_ref[0]


def _ada_modulation(c_rows, w_ada, b_ada):
    tn = 1536
    return pl.pallas_call(
        _ada_kernel,
        grid=(DEPTH, 6 * D_MODEL // tn),
        in_specs=[
            pl.BlockSpec((SUBLANES, D_MODEL), lambda l, n: (0, 0)),
            pl.BlockSpec((1, D_MODEL, tn), lambda l, n: (l, 0, n)),
            pl.BlockSpec((1, 1, tn), lambda l, n: (l, 0, n)),
        ],
        out_specs=pl.BlockSpec((1, SUBLANES, tn), lambda l, n: (l, 0, n)),
        out_shape=jax.ShapeDtypeStruct((DEPTH, SUBLANES, 6 * D_MODEL), F32),
        compiler_params=_params(("parallel", "parallel")),
        name="ada_modulation",
    )(c_rows, w_ada, b_ada.reshape(DEPTH, 1, 6 * D_MODEL))


def _rotate_pairs(x, cos, sin_signed):
    n = x.shape[-1]
    lane = lax.broadcasted_iota(jnp.int32, x.shape, 1)
    first_half = (lane % HEAD_DIM) < HALF
    partner = jnp.where(first_half, pltpu.roll(x, n - HALF, 1), pltpu.roll(x, HALF, 1))
    reps = n // LANES
    return x * jnp.tile(cos, (1, reps)) + partner * jnp.tile(sin_signed, (1, reps))


def _inproj_kernel(x_ctx_ref, x_dec_ref, mod_ref, g_ref, w_ref, ones_ref, qg_ref, kg_ref, cos_ref, sin_ref,
                   u_ref, q_ref, kc_ref, vc_ref, kd_ref, vd_ref, sgs_ref, sga_ref, *x_copy_ref):
    mod = mod_ref[0]
    x = _pick_stream(x_ctx_ref, x_dec_ref)
    if x_copy_ref:
        x_copy_ref[0][...] = x
    h = _rms_modulate(x, g_ref[...], mod[1:2], mod[0:1]).astype(BF16)

    def proj(lo, width):
        return jnp.dot(h, w_ref[:, lo:lo + width], preferred_element_type=F32)

    def head_norm(t, gain):
        sq = (t * t).astype(BF16)
        ss = jnp.concatenate(
            [jnp.dot(sq[:, c:c + MXU_WIDTH], ones_ref[...], preferred_element_type=F32)
             for c in range(0, t.shape[-1], MXU_WIDTH)], axis=-1)
        return t * lax.rsqrt(ss * (1.0 / HEAD_DIM) + EPS) * gain

    o = 0
    u_ref[...] = proj(o, SSM_WIDTH)
    o += SSM_WIDTH
    q = head_norm(proj(o, Q_W), qg_ref[...]) * (1.0 / math.sqrt(HEAD_DIM))
    o += Q_W
    k = head_norm(proj(o, KV_W), kg_ref[...])
    o += KV_W
    v = proj(o, KV_W)
    o += KV_W
    sgs_ref[...] = _sigmoid(proj(o, D_MODEL))
    o += D_MODEL
    sga_ref[...] = _sigmoid(proj(o, D_MODEL))

    @pl.when(_is_ctx_tile())
    def _():
        q_ref[...] = q.astype(BF16)
        kc_ref[...] = k
        vc_ref[...] = v

    @pl.when(jnp.logical_not(_is_ctx_tile()))
    def _():
        cos = cos_ref[...]
        sin = sin_ref[...]
        q_ref[...] = _rotate_pairs(q, cos, sin).astype(BF16)
        kd_ref[...] = _rotate_pairs(k, cos, sin)
        vd_ref[...] = v


def _in_projection(x_ctx, x_dec, dec_first_block, mod, norm_g, w_in_bf16, ones_bd, q_gain, k_gain,
                   rope_cos, rope_sin, copy_x):
    tm = ROW_TILE
    n_tiles = ALL_TOKENS // tm
    row = lambda w: pl.BlockSpec((tm, w), lambda i: (i, 0))
    kv_ctx, kv_dec = _stream_specs(KV_W)
    out_specs = [row(SSM_WIDTH), row(Q_W), kv_ctx, kv_ctx, kv_dec, kv_dec, row(D_MODEL), row(D_MODEL)]
    out_shape = [jax.ShapeDtypeStruct((ALL_TOKENS, SSM_WIDTH), F32), jax.ShapeDtypeStruct((ALL_TOKENS, Q_W), BF16),
                 jax.ShapeDtypeStruct((CTX_TOKENS, KV_W), F32), jax.ShapeDtypeStruct((CTX_TOKENS, KV_W), F32),
                 jax.ShapeDtypeStruct((DEC_TOKENS, KV_W), F32), jax.ShapeDtypeStruct((DEC_TOKENS, KV_W), F32),
                 jax.ShapeDtypeStruct((ALL_TOKENS, D_MODEL), F32), jax.ShapeDtypeStruct((ALL_TOKENS, D_MODEL), F32)]
    if copy_x:
        out_specs.append(row(D_MODEL))
        out_shape.append(jax.ShapeDtypeStruct((ALL_TOKENS, D_MODEL), F32))
    return pl.pallas_call(
        _inproj_kernel,
        grid=(n_tiles,),
        in_specs=[
            *_stream_specs(D_MODEL, dec_first_block),
            pl.BlockSpec((1, 6, D_MODEL), lambda i: (_mod_row(i, tm), 0, 0)),
            _resident((1, D_MODEL), lambda i: (0, 0)),
            _resident((D_MODEL, IN_W), lambda i: (0, 0)),
            _resident((MXU_WIDTH, MXU_WIDTH), lambda i: (0, 0)),
            _resident((1, Q_W), lambda i: (0, 0)),
            _resident((1, KV_W), lambda i: (0, 0)),
            pl.BlockSpec((tm, LANES), lambda i: (_rope_block(i, tm), 0)),
            pl.BlockSpec((tm, LANES), lambda i: (_rope_block(i, tm), 0)),
        ],
        out_specs=out_specs,
        out_shape=out_shape,
        compiler_params=_params(("arbitrary",)),
        name="in_projection",
    )(x_ctx, x_dec, mod, norm_g, w_in_bf16, ones_bd, q_gain, k_gain, rope_cos, rope_sin)


def _rider_specs(riders, n_steps, linear_step):
    specs = [pl.BlockSpec((w.shape[0] // n_steps, w.shape[1]), lambda *idx: (linear_step(*idx), 0)) for w in riders]
    return specs, [jax.ShapeDtypeStruct(w.shape, BF16) for w in riders]


def _narrow_riders(wide_refs, narrow_refs):
    for wide_ref, narrow_ref in zip(wide_refs, narrow_refs):
        narrow_ref[...] = wide_ref[...].astype(BF16)


def _attn_kernel(*refs, with_cache, n_riders):
    if n_riders:
        n_in = len(refs) - 1 - 2 * n_riders
        _narrow_riders(refs[n_in:n_in + n_riders], refs[n_in + n_riders + 1:])
        refs = refs[:n_in] + refs[n_in + n_riders:n_in + n_riders + 1]
    if with_cache:
        q_ref, k_ref, v_ref, ck_ref, cv_ref, o_ref = refs
        k_parts = (lambda sl: ck_ref[0, 0, :, sl], lambda sl: k_ref[0, :, sl])
        v_parts = (lambda sl: cv_ref[0, 0, :, sl], lambda sl: v_ref[0, :, sl])
    else:
        q_ref, k_ref, v_ref, o_ref = refs
        k_parts = (lambda sl: k_ref[0, :, sl],)
        v_parts = (lambda sl: v_ref[0, :, sl],)
    lane = lax.broadcasted_iota(jnp.int32, (1, LANES), 1)
    low = lane < HEAD_DIM
    groups_per_slab = LANES // HEAD_DIM
    rep = N_HEADS // N_KV_HEADS
    for g in range(N_KV_HEADS):
        slab = g // groups_per_slab
        own_low = (g % groups_per_slab) == 0
        keep = low if own_low else jnp.logical_not(low)

        def halves(parts):
            lanes = slice(slab * LANES, (slab + 1) * LANES)
            rows = jnp.concatenate([part(lanes) for part in parts], axis=0)
            same = jnp.where(keep, rows, 0.0)
            swapped = pltpu.roll(same, HEAD_DIM, 1)
            lo, hi = (same, swapped) if own_low else (swapped, same)
            return lo.astype(BF16), hi.astype(BF16)

        k_lo, k_hi = halves(k_parts)
        v_lo, v_hi = halves(v_parts)
        for j in range(g * rep // groups_per_slab, (g + 1) * rep // groups_per_slab):
            qs = q_ref[0, :, j * LANES:(j + 1) * LANES]
            acc = None
            for kk, vv in ((k_lo, v_lo), (k_hi, v_hi)):
                s = lax.dot_general(qs, kk, (((1,), (1,)), ((), ())), preferred_element_type=F32)
                p = jnp.exp(s - jnp.max(s, axis=-1, keepdims=True))
                denom = jnp.sum(p, axis=-1, keepdims=True)
                part = jnp.dot(p.astype(BF16), vv, preferred_element_type=F32) / denom
                acc = part if acc is None else acc + part
            o_ref[:, j * LANES:(j + 1) * LANES] = acc.astype(BF16)


def _attention(q, k, v, n_batch, seq_len, first_token, name, cache=None, riders=()):
    tq = ATTN_Q_TILE
    per_b = seq_len // tq
    q_tiles = q.reshape(ALL_TOKENS // tq, tq, Q_W)
    first_tile = first_token // tq
    kv_spec = pl.BlockSpec((1, seq_len, KV_W), lambda b, i: (b, 0, 0))
    in_specs = [pl.BlockSpec((1, tq, Q_W), lambda b, i: (first_tile + b * per_b + i, 0, 0)), kv_spec, kv_spec]
    args = [q_tiles, k.reshape(n_batch, seq_len, KV_W), v.reshape(n_batch, seq_len, KV_W)]
    if cache is not None:
        cache_k, cache_v, layer = cache
        cache_spec = pl.BlockSpec((1, 1, PAST_LEN, KV_W), lambda b, i: (b, layer, 0, 0))
        in_specs += [cache_spec, cache_spec]
        args += [cache_k, cache_v]
    rider_specs, rider_shapes = _rider_specs(riders, n_batch * per_b, lambda b, i: b * per_b + i)
    return pl.pallas_call(
        functools.partial(_attn_kernel, with_cache=cache is not None, n_riders=len(riders)),
        grid=(n_batch, per_b),
        in_specs=in_specs + rider_specs,
        out_specs=[pl.BlockSpec((tq, Q_W), lambda b, i: (b * per_b + i, 0))] + rider_specs,
        out_shape=[jax.ShapeDtypeStruct((n_batch * seq_len, Q_W), BF16)] + rider_shapes,
        compiler_params=_params(("parallel", "parallel")),
        name=name,
    )(*args, *riders)


def _s5_kernel(u_ref, bd_ref, a_ref, cd_ref, h0_ref, dskip_ref, *rest, chain, n_riders):
    wide_refs, (y_ref, hfin_ref), rest = rest[:n_riders], rest[n_riders:n_riders + 2], rest[n_riders + 2:]
    narrow_refs, (u_scr, y_scr, bu_scr, hs_scr) = rest[:n_riders], rest[n_riders:]
    _narrow_riders(wide_refs, narrow_refs)
    _s5_body(u_ref, bd_ref, a_ref, cd_ref, h0_ref, dskip_ref, y_ref, hfin_ref, u_scr, y_scr, bu_scr, hs_scr, chain)


def _s5_body(u_ref, bd_ref, a_ref, cd_ref, h0_ref, dskip_ref, y_ref, hfin_ref,
             u_scr, y_scr, bu_scr, hs_scr, chain):
    seq_len = u_ref.shape[0] // SUBLANES
    forward = pl.program_id(2) == 0
    last_dir = pl.program_id(2) == pl.num_programs(2) - 1
    a_re = jnp.broadcast_to(a_ref[0, 0, :, :S5_SET_ST], (SUBLANES, S5_SET_ST))
    a_im = jnp.broadcast_to(a_ref[0, 0, :, S5_SET_ST:], (SUBLANES, S5_SET_ST))

    @pl.when(forward)
    def _():
        for p in range(SUBLANES):
            u_scr[pl.ds(p, seq_len, stride=SUBLANES), :] = u_ref[p * seq_len:(p + 1) * seq_len, :]
        y_scr[...] = u_scr[...] * dskip_ref[...]

    bu_scr[...] = jnp.dot(u_scr[...].astype(BF16), bd_ref[0, 0], preferred_element_type=F32)

    def scan(init, store):
        def step(t, hc):
            h_re, h_im = hc
            tt = jnp.where(forward, t, seq_len - 1 - t)
            r = pl.multiple_of(tt * SUBLANES, SUBLANES)
            n_re = a_re * h_re - a_im * h_im + bu_scr[pl.ds(r, SUBLANES), :S5_SET_ST]
            n_im = a_re * h_im + a_im * h_re + bu_scr[pl.ds(r, SUBLANES), S5_SET_ST:]
            if store:
                hs_scr[pl.ds(r, SUBLANES), :S5_SET_ST] = n_re
                hs_scr[pl.ds(r, SUBLANES), S5_SET_ST:] = n_im
            return n_re, n_im

        return lax.fori_loop(0, seq_len, step, init, unroll=S5_SCAN_UNROLL)

    h0 = h0_ref[0, 0]
    init = (h0[:, :S5_SET_ST], h0[:, S5_SET_ST:])
    if chain > 1:
        zero = jnp.zeros((SUBLANES, S5_SET_ST), F32)
        end_re, end_im = scan((zero, zero), store=False)
        p_re, p_im = a_re, a_im
        for _ in range(seq_len.bit_length() - 1):
            p_re, p_im = p_re * p_re - p_im * p_im, 2.0 * (p_re * p_im)
        piece = lax.broadcasted_iota(jnp.int32, (SUBLANES, S5_SET_ST), 0) % chain
        entry = piece == jnp.where(forward, 0, chain - 1)

        def from_neighbour(x):
            return jnp.where(forward, pltpu.roll(x, 1, 0), pltpu.roll(x, SUBLANES - 1, 0))

        i_re = jnp.where(entry, init[0], 0.0)
        i_im = jnp.where(entry, init[1], 0.0)
        for _ in range(chain - 1):
            o_re = p_re * i_re - p_im * i_im + end_re
            o_im = p_re * i_im + p_im * i_re + end_im
            i_re = jnp.where(entry, init[0], from_neighbour(o_re))
            i_im = jnp.where(entry, init[1], from_neighbour(o_im))
        init = (i_re, i_im)
    h_re, h_im = scan(init, store=True)
    hfin_ref[0, 0, :, :S5_SET_ST] = h_re
    hfin_ref[0, 0, :, S5_SET_ST:] = h_im
    y_scr[...] += jnp.dot(hs_scr[...].astype(BF16), cd_ref[0, 0], preferred_element_type=F32)

    @pl.when(last_dir)
    def _():
        for p in range(SUBLANES):
            y_ref[p * seq_len:(p + 1) * seq_len, :] = y_scr[pl.ds(p, seq_len, stride=SUBLANES), :]


def _s5_branch(u_all, first_token, n_tokens, bd, a_bar, cd, h0, d_skip, chain, name, riders=()):
    seq_len = S5_PIECE
    rows = seq_len * SUBLANES
    nb = n_tokens // seq_len
    first_block = first_token // rows
    assert seq_len & (seq_len - 1) == 0 and SUBLANES % chain == 0 and first_token % rows == 0
    st2 = 2 * S5_SET_ST
    par = lambda shape: pl.BlockSpec(shape, lambda b, s, d: (d, s, 0, 0))
    state = pl.BlockSpec((1, 1, SUBLANES, st2), lambda b, s, d: (d, s, b, 0))
    seq = pl.BlockSpec((rows, S5_SET_CH), lambda b, s, d: (b, s))
    grid = (nb // SUBLANES, S5_N_SETS, N_DIRS)
    in_specs = [
        pl.BlockSpec((rows, S5_SET_CH), lambda b, s, d: (first_block + b, s)),
        par((1, 1, S5_SET_CH, st2)),
        par((1, 1, 1, st2)),
        par((1, 1, st2, S5_SET_CH)),
        state,
        pl.BlockSpec((1, S5_SET_CH), lambda b, s, d: (0, s)),
    ]
    out_specs = [seq, state]
    out_shape = [jax.ShapeDtypeStruct((n_tokens, SSM_WIDTH), F32),
                 jax.ShapeDtypeStruct((N_DIRS, S5_N_SETS, nb, st2), F32)]
    args = [u_all, bd, a_bar, cd, h0, d_skip, *riders]
    rider_specs, rider_shapes = _rider_specs(riders, grid[0] * grid[1] * grid[2],
                                             lambda b, s, d: (b * S5_N_SETS + s) * N_DIRS + d)
    return pl.pallas_call(
        functools.partial(_s5_kernel, chain=chain, n_riders=len(riders)),
        grid=grid,
        in_specs=in_specs + rider_specs,
        out_specs=out_specs + rider_specs,
        out_shape=out_shape + rider_shapes,
        scratch_shapes=[pltpu.VMEM((rows, S5_SET_CH), F32), pltpu.VMEM((rows, S5_SET_CH), F32),
                        pltpu.VMEM((rows, st2), F32), pltpu.VMEM((rows, st2), F32)],
        compiler_params=_params(("parallel", "parallel", "arbitrary")),
        name=name,
    )(*args)


def _s5_operators(l, P):
    a_re, a_im = P['ssm_a_re'][l].astype(F32), P['ssm_a_im'][l].astype(F32)
    dt = jnp.exp(P['ssm_log_dt'][l].astype(F32))[..., None]
    mag = jnp.exp(a_re * dt)
    ab_re, ab_im = mag * jnp.cos(a_im * dt), mag * jnp.sin(a_im * dt)
    den = a_re * a_re + a_im * a_im
    k_re = ((ab_re - 1.0) * a_re + ab_im * a_im) / den
    k_im = (ab_im * a_re - (ab_re - 1.0) * a_im) / den
    b_re, b_im = P['ssm_b_re'][l].astype(F32), P['ssm_b_im'][l].astype(F32)
    bb_re = k_re[..., None] * b_re - k_im[..., None] * b_im
    bb_im = k_re[..., None] * b_im + k_im[..., None] * b_re
    eye = jnp.eye(S5_GROUP_SET, dtype=F32)

    def sets(t):
        return t.reshape(N_DIRS, S5_N_SETS, S5_GROUP_SET, *t.shape[2:])

    def b_operator(part):
        return jnp.einsum('dsgnp,gh->dsgphn', part, eye).reshape(N_DIRS, S5_N_SETS, S5_SET_CH, S5_SET_ST)

    def c_operator(part):
        return jnp.einsum('dsgpn,gh->dshngp', part, eye).reshape(N_DIRS, S5_N_SETS, S5_SET_ST, S5_SET_CH)

    bd = jnp.concatenate([b_operator(sets(bb_re)), b_operator(sets(bb_im))], axis=-1).astype(BF16)
    c_re = sets(P['ssm_c_re'][l].astype(F32))
    c_im = sets(P['ssm_c_im'][l].astype(F32))
    cd = jnp.concatenate([c_operator(c_re), -c_operator(c_im)], axis=-2).astype(BF16)
    row = lambda t: sets(t).reshape(N_DIRS, S5_N_SETS, 1, S5_SET_ST)
    a_rows = jnp.concatenate([row(ab_re), row(ab_im)], axis=-1)
    return bd, a_rows, cd


def _mixer_out_kernel(attn_ctx_ref, attn_dec_ref, y_ctx_ref, y_dec_ref, sgs_ref, sga_ref, x_ref, mod_ref,
                      wab_ref, wglu_ref, wout_ref, o_ref):
    attn_br = jnp.dot(_pick_stream(attn_ctx_ref, attn_dec_ref), wab_ref[...], preferred_element_type=F32)
    y = _pick_stream(y_ctx_ref, y_dec_ref)
    gelu = 0.5 * y * (1.0 + jnp.tanh(math.sqrt(2.0 / math.pi) * (y + 0.044715 * (y * y * y))))
    glu = jnp.dot(gelu.astype(BF16), wglu_ref[...], preferred_element_type=F32)
    ssm_br = glu[:, :D_MODEL] * _sigmoid(glu[:, D_MODEL:])
    merged = sgs_ref[...] * ssm_br + sga_ref[...] * attn_br
    out = jnp.dot(merged.astype(BF16), wout_ref[...], preferred_element_type=F32)
    o_ref[...] = x_ref[...] + mod_ref[0][2:3] * out


def _mixer_out(attn_ctx, attn_dec, y_ctx, y_dec, sgs, sga, x_all, mod, w_attn_br, w_glu, w_out):
    tm = ROW_TILE
    row = lambda w: pl.BlockSpec((tm, w), lambda i: (i, 0))
    return pl.pallas_call(
        _mixer_out_kernel,
        grid=(ALL_TOKENS // tm,),
        in_specs=[
            *_stream_specs(Q_W), *_stream_specs(SSM_WIDTH),
            row(D_MODEL), row(D_MODEL), row(D_MODEL),
            pl.BlockSpec((1, 6, D_MODEL), lambda i: (_mod_row(i, tm), 0, 0)),
            _resident((Q_W, D_MODEL), lambda i: (0, 0)),
            _resident((SSM_WIDTH, 2 * D_MODEL), lambda i: (0, 0)),
            _resident((D_MODEL, D_MODEL), lambda i: (0, 0)),
        ],
        out_specs=row(D_MODEL),
        out_shape=jax.ShapeDtypeStruct((ALL_TOKENS, D_MODEL), F32),
        compiler_params=_params(("parallel",)),
        name="mixer_out",
    )(attn_ctx, attn_dec, y_ctx, y_dec, sgs, sga, x_all, mod, w_attn_br, w_glu, w_out)


ROUTE_IDX_LANE = 0
ROUTE_W_LANE = 2


ROW_CHUNKS = D_MODEL // LANES


def _store_rows_as_tiles(ref, lead, value):
    n = value.shape[0]
    for c in range(ROW_CHUNKS):
        ref[(*lead, pl.ds(c, n, stride=ROW_CHUNKS), slice(None))] = value[:, c * LANES:(c + 1) * LANES]


def _load_rows_from_tiles(ref, lead, n, first=0, every=1):
    stride = every * ROW_CHUNKS
    return jnp.concatenate(
        [ref[(*lead, pl.ds(first * ROW_CHUNKS + c, n, stride=stride), slice(None))] for c in range(ROW_CHUNKS)],
        axis=-1)


def _router_kernel(x_ref, mod_ref, g_ref, r_ref, h_ref, route_ref):
    mod = mod_ref[0]
    h = _rms_modulate(x_ref[...], g_ref[...], mod[4:5], mod[3:4])
    _store_rows_as_tiles(h_ref, (), h)
    r = r_ref[...]
    h_hi, r_hi = h.astype(BF16), r.astype(BF16)
    h_lo = (h - h_hi.astype(F32)).astype(BF16)
    r_lo = (r - r_hi.astype(F32)).astype(BF16)
    logits = (jnp.dot(h_hi, r_hi, preferred_element_type=F32) + jnp.dot(h_lo, r_hi, preferred_element_type=F32)
              + jnp.dot(h_hi, r_lo, preferred_element_type=F32))
    lane = lax.broadcasted_iota(jnp.int32, logits.shape, 1).astype(F32)
    logits = jnp.where(lane < N_EXPERTS, logits, -jnp.inf)
    top1 = jnp.max(logits, axis=-1, keepdims=True)
    idx1 = jnp.min(jnp.where(logits == top1, lane, float(LANES)), axis=-1, keepdims=True)
    rest = jnp.where(lane == idx1, -jnp.inf, logits)
    top2 = jnp.max(rest, axis=-1, keepdims=True)
    idx2 = jnp.min(jnp.where(rest == top2, lane, float(LANES)), axis=-1, keepdims=True)
    e2 = jnp.exp(top2 - top1)
    w1 = 1.0 / (1.0 + e2)
    w2 = e2 / (1.0 + e2)
    route = jnp.where(lane == ROUTE_IDX_LANE, idx1, 0.0)
    route = jnp.where(lane == ROUTE_IDX_LANE + 1, idx2, route)
    route = jnp.where(lane == ROUTE_W_LANE, w1, route)
    route_ref[...] = jnp.where(lane == ROUTE_W_LANE + 1, w2, route)


def _prenorm_router(x_all, mod, norm_g, router_pad):
    tm = ROW_TILE
    row = lambda w: pl.BlockSpec((tm, w), lambda i: (i, 0))
    in_specs = [row(D_MODEL),
                pl.BlockSpec((1, 6, D_MODEL), lambda i: (_mod_row(i, tm), 0, 0)),
                _resident((1, D_MODEL), lambda i: (0, 0))]
    return pl.pallas_call(
        _router_kernel, grid=(ALL_TOKENS // tm,),
        in_specs=in_specs + [_resident((D_MODEL, LANES), lambda i: (0, 0))],
        out_specs=[pl.BlockSpec((tm * ROW_CHUNKS, LANES), lambda i: (i, 0)), row(LANES)],
        out_shape=[jax.ShapeDtypeStruct((ALL_TOKENS * ROW_CHUNKS, LANES), F32),
                   jax.ShapeDtypeStruct((ALL_TOKENS, LANES), F32)],
        compiler_params=_params(("parallel",)), name="prenorm_router",
    )(x_all, mod, norm_g, router_pad)


def _swiglu_chains(h, gate_cols, up_cols, down_rows, d_ff):
    out = None
    for c in range(d_ff // FF_CHUNK):
        cols = slice(c * FF_CHUNK, (c + 1) * FF_CHUNK)
        a = jnp.dot(h, gate_cols(cols), preferred_element_type=F32)
        b = jnp.dot(h, up_cols(cols), preferred_element_type=F32)
        inter = ((a * _sigmoid(a)) * b).astype(BF16)
        part = jnp.dot(inter, down_rows(cols), preferred_element_type=F32)
        out = part if out is None else out + part
    return out


def _dense_ffn_kernel(x_ref, mod_ref, g_ref, wg_ref, wu_ref, wd_ref, fg_ref, o_ref, *, final):
    mod = mod_ref[0]
    x = x_ref[...]
    h = _rms_modulate(x, g_ref[...], mod[4:5], mod[3:4]).astype(BF16)
    f = _swiglu_chains(h, lambda c: wg_ref[:, c], lambda c: wu_ref[:, c], lambda c: wd_ref[c, :],
                       wg_ref.shape[-1])
    x = x + mod[5:6] * f
    if final:
        ms = jnp.mean(x * x, axis=-1, keepdims=True)
        x = x * lax.rsqrt(ms + EPS) * fg_ref[...]
    o_ref[...] = x


def _dense_ffn(x_all, mod, norm_g, w_gate, w_up, w_down, final_g, final):
    tm = ROW_TILE
    d_ff = w_gate.shape[-1]
    row = pl.BlockSpec((tm, D_MODEL), lambda i: (i, 0))
    return pl.pallas_call(
        functools.partial(_dense_ffn_kernel, final=final),
        grid=(ALL_TOKENS // tm,),
        in_specs=[row,
                  pl.BlockSpec((1, 6, D_MODEL), lambda i: (_mod_row(i, tm), 0, 0)),
                  _resident((1, D_MODEL), lambda i: (0, 0)),
                  _resident((D_MODEL, d_ff), lambda i: (0, 0)),
                  _resident((D_MODEL, d_ff), lambda i: (0, 0)),
                  _resident((d_ff, D_MODEL), lambda i: (0, 0)),
                  _resident((1, D_MODEL), lambda i: (0, 0))],
        out_specs=row,
        out_shape=jax.ShapeDtypeStruct((ALL_TOKENS, D_MODEL), F32),
        compiler_params=_params(("parallel",)),
        name="ffn_dense",
    )(x_all, mod, norm_g, w_gate, w_up, w_down, final_g)


def _moe_ffn_kernel(tok_ref, slot_ref, expert_ref, valid_ref, h_ref, wg_ref, wu_ref, wd_ref, y_ref,
                    xbuf, ybuf, acc, sems):
    s = pl.program_id(0)
    j = pl.program_id(1)
    n_j = pl.num_programs(1)
    tm = acc.shape[0]
    share = tm // n_j
    cur = s % 2
    other = 1 - cur
    gather_sem, scatter_sem = sems.at[0], sems.at[1]

    def tile_of(row):
        return pl.ds(pl.multiple_of(row * ROW_CHUNKS, ROW_CHUNKS), ROW_CHUNKS)

    def whole(n_rows):
        return pl.ds(0, n_rows * ROW_CHUNKS)

    @pl.when(jnp.logical_and(s == 0, j == 0))
    def _():
        ybuf[...] = jnp.zeros_like(ybuf)

        def first_tile(r, carry):
            pltpu.make_async_copy(h_ref.at[tile_of(tok_ref[tm + r])], xbuf.at[0, tile_of(r)], gather_sem).start()
            return carry

        lax.fori_loop(0, tm, first_tile, 0)
        pltpu.make_async_copy(h_ref.at[whole(tm)], xbuf.at[0], gather_sem).wait()
        spill = pltpu.make_async_copy(
            ybuf.at[0], y_ref.at[pl.ds(TOP_K * ALL_TOKENS * ROW_CHUNKS, tm * ROW_CHUNKS)], scatter_sem)
        spill.start()
        spill.wait()

    def start_row_copies():
        row0 = j * share
        for r in range(share):
            row = row0 + r
            pltpu.make_async_copy(h_ref.at[tile_of(tok_ref[(s + 2) * tm + row])], xbuf.at[other, tile_of(row)],
                                  gather_sem).start(priority=r % 2)
            pltpu.make_async_copy(ybuf.at[other, tile_of(row)], y_ref.at[tile_of(slot_ref[s * tm + row])],
                                  scatter_sem).start(priority=(r + 1) % 2)

    def drain_row_copies():
        pltpu.make_async_copy(h_ref.at[whole(share)], xbuf.at[0, whole(share)], gather_sem).wait()
        pltpu.make_async_copy(ybuf.at[0, whole(share)], y_ref.at[whole(share)], scatter_sem).wait()

    @pl.when(s >= 0)
    def _():
        start_row_copies()

    @pl.when(valid_ref[s] == 1)
    def _():
        h = _load_rows_from_tiles(xbuf, (cur,), tm).astype(BF16)
        contrib = _swiglu_chains(h, lambda c: wg_ref[0, :, c], lambda c: wu_ref[0, :, c],
                                 lambda c: wd_ref[0, c, :], wg_ref.shape[-1])

        @pl.when(j == 0)
        def _():
            acc[...] = contrib

        @pl.when(jnp.logical_and(j != 0, j != n_j - 1))
        def _():
            acc[...] += contrib

        @pl.when(j == n_j - 1)
        def _():
            _store_rows_as_tiles(ybuf, (cur,), acc[...] + contrib)

    drain_row_copies()


def _moe_ffn(h_tiles, row_token, row_slot, w_gate, w_up, w_down, step_expert, step_valid, tf):
    tm = MOE_ROW_TILE
    d_ff = w_gate.shape[-1]
    n_steps = step_expert.shape[0]
    n_ff = d_ff // tf
    assert n_ff >= 2
    ff = lambda s, j, f: jnp.where(f[s] == 1, j, n_ff - 1)
    grid_spec = pltpu.PrefetchScalarGridSpec(
        num_scalar_prefetch=4,
        grid=(n_steps, n_ff),
        in_specs=[
            pl.BlockSpec(memory_space=pl.ANY),
            pl.BlockSpec((1, D_MODEL, tf), lambda s, j, t, d, e, f: (e[s], 0, ff(s, j, f))),
            pl.BlockSpec((1, D_MODEL, tf), lambda s, j, t, d, e, f: (e[s], 0, ff(s, j, f))),
            pl.BlockSpec((1, tf, D_MODEL), lambda s, j, t, d, e, f: (e[s], ff(s, j, f), 0)),
        ],
        out_specs=pl.BlockSpec(memory_space=pl.ANY),
        scratch_shapes=[pltpu.VMEM((2, tm * ROW_CHUNKS, LANES), F32), pltpu.VMEM((2, tm * ROW_CHUNKS, LANES), F32),
                        pltpu.VMEM((tm, D_MODEL), F32), pltpu.SemaphoreType.DMA((2,))],
    )
    return pl.pallas_call(
        _moe_ffn_kernel,
        grid_spec=grid_spec,
        out_shape=jax.ShapeDtypeStruct(((TOP_K * ALL_TOKENS + tm) * ROW_CHUNKS, LANES), F32),
        compiler_params=_params(("arbitrary", "arbitrary")),
        name="ffn_experts",
    )(row_token, row_slot, step_expert, step_valid, h_tiles, w_gate, w_up, w_down)


TOP_K = 2
MOE_ROW_TILE = 512
MOE_ROWS = TOP_K * ALL_TOKENS + N_EXPERTS * MOE_ROW_TILE


def _route_plan(route):
    tm = MOE_ROW_TILE
    n_slots = TOP_K * ALL_TOKENS
    n_steps = MOE_ROWS // tm + 1
    expert = route[:, ROUTE_IDX_LANE:ROUTE_IDX_LANE + TOP_K].astype(jnp.int32).reshape(-1)
    experts = jnp.arange(N_EXPERTS, dtype=jnp.int32)[None, :]
    order = jnp.argsort(expert, stable=True).astype(jnp.int32)
    counts = jnp.sum((expert[:, None] == experts).astype(jnp.int32), axis=0)
    tiles = (counts + tm - 1) // tm
    tile_end = jnp.cumsum(tiles)
    starts = (tile_end - tiles) * tm
    before = jnp.cumsum(counts) - counts
    last = tile_end[-1] - 1
    step = jnp.arange(n_steps, dtype=jnp.int32)
    step_expert = jnp.sum((jnp.minimum(step, last)[:, None] >= tile_end[None, :]).astype(jnp.int32), axis=1)
    step_valid = (step <= last).astype(jnp.int32)
    row = jnp.arange((n_steps + 2) * tm, dtype=jnp.int32) - tm
    row_expert = jnp.sum(((row // tm)[:, None] >= tile_end[None, :]).astype(jnp.int32), axis=1)
    pick = (jnp.minimum(row_expert, N_EXPERTS - 1)[:, None] == experts).astype(jnp.int32)
    rank = row - jnp.sum(pick * starts[None, :], axis=1)
    real = (row >= 0) & (row_expert < N_EXPERTS) & (rank < jnp.sum(pick * counts[None, :], axis=1))
    sorted_pos = jnp.clip(jnp.sum(pick * before[None, :], axis=1) + rank, 0, n_slots - 1)
    slot = order[sorted_pos]
    token = slot // TOP_K
    row_token = jnp.where(real, token, 0)
    row_slot = jnp.where(real, (slot % TOP_K) * ALL_TOKENS + token, n_slots + row % tm)
    return row_token.astype(jnp.int32), row_slot.astype(jnp.int32), step_expert.astype(jnp.int32), step_valid


def _combine_kernel(x_ref, route_ref, mod_ref, g_ref, y0_ref, y1_ref, *out_refs, final):
    route = route_ref[...]
    tm = x_ref.shape[0]
    f = (route[:, ROUTE_W_LANE:ROUTE_W_LANE + 1] * _load_rows_from_tiles(y0_ref, (), tm)
         + route[:, ROUTE_W_LANE + 1:ROUTE_W_LANE + 2] * _load_rows_from_tiles(y1_ref, (), tm))
    x = x_ref[...] + mod_ref[0][5:6] * f
    if not final:
        out_refs[0][...] = x
        return
    ms = jnp.mean(x * x, axis=-1, keepdims=True)
    x = x * lax.rsqrt(ms + EPS) * g_ref[...]
    is_ctx = _is_ctx_tile()
    @pl.when(is_ctx)
    def _():
        out_refs[0][...] = x

    @pl.when(jnp.logical_not(is_ctx))
    def _():
        out_refs[1][...] = x


def _combine(x_all, y_slots, route, mod, final_g, final):
    tm = ROW_TILE
    n_tiles = ALL_TOKENS // tm
    row = lambda w: pl.BlockSpec((tm, w), lambda i: (i, 0))
    if final:
        out_specs = list(_stream_specs(D_MODEL))
        out_shape = [jax.ShapeDtypeStruct((CTX_TOKENS, D_MODEL), F32),
                     jax.ShapeDtypeStruct((DEC_TOKENS, D_MODEL), F32)]
    else:
        out_specs = [row(D_MODEL)]
        out_shape = [jax.ShapeDtypeStruct((ALL_TOKENS, D_MODEL), F32)]
    return pl.pallas_call(
        functools.partial(_combine_kernel, final=final),
        grid=(n_tiles,),
        in_specs=[row(D_MODEL), row(LANES),
                  pl.BlockSpec((1, 6, D_MODEL), lambda i: (_mod_row(i, tm), 0, 0)),
                  _resident((1, D_MODEL), lambda i: (0, 0)),
                  pl.BlockSpec((tm * ROW_CHUNKS, LANES), lambda i: (i, 0)),
                  pl.BlockSpec((tm * ROW_CHUNKS, LANES), lambda i: (n_tiles + i, 0))],
        out_specs=out_specs,
        out_shape=out_shape,
        compiler_params=_params(("arbitrary",)),
        name="moe_combine",
    )(x_all, route, mod, final_g, y_slots, y_slots)


def _rope_tables():
    rows = DEC_SEQ // GRID_W
    row = jnp.repeat(jnp.arange(rows, dtype=F32), GRID_W)
    col = jnp.tile(jnp.arange(GRID_W, dtype=F32), rows)
    inv = ROPE_THETA ** (-jnp.arange(0, AXIS_DIM, 2, dtype=F32) / AXIS_DIM)
    ang = jnp.concatenate([row[:, None] * inv, col[:, None] * inv], axis=-1)
    cos, sin = jnp.cos(ang), jnp.sin(ang)
    reps = LANES // HEAD_DIM
    cos_l = jnp.tile(jnp.concatenate([cos, cos], axis=-1), (1, reps))
    sin_l = jnp.tile(jnp.concatenate([-sin, sin], axis=-1), (1, reps))
    return cos_l, sin_l


def kernel(x_prompt, x_sample, cache_k, cache_v, state_ssm, c, c_ctx, w_ada, b_ada, norm1_g, norm2_g, w_in, q_norm_g, k_norm_g, ssm_a_re, ssm_a_im, ssm_log_dt, ssm_b_re, ssm_b_im, ssm_c_re, ssm_c_im, ssm_d, w_glu, w_attn_br, w_out, ffn_w_gate, ffn_w_up, ffn_w_down, moe_router, moe_w_gate, moe_w_up, moe_w_down, final_norm_g):
    P = dict(ssm_a_re=ssm_a_re, ssm_a_im=ssm_a_im, ssm_log_dt=ssm_log_dt, ssm_b_re=ssm_b_re,
             ssm_b_im=ssm_b_im, ssm_c_re=ssm_c_re, ssm_c_im=ssm_c_im)

    c_rows = jnp.zeros((SUBLANES, D_MODEL), F32).at[0].set(c_ctx).at[1:1 + DEC_BATCH].set(c)
    mod_all = _ada_modulation(c_rows, w_ada, b_ada)
    mod_all = mod_all[:, :N_MOD_ROWS].reshape(DEPTH, N_MOD_ROWS, 6, D_MODEL)

    rope_cos, rope_sin = _rope_tables()
    head_id = jnp.arange(MXU_WIDTH, dtype=jnp.int32) // HEAD_DIM
    ones_bd = (head_id[:, None] == head_id[None, :]).astype(BF16)

    cache_k4 = cache_k.reshape(DEC_BATCH, DEPTH, PAST_LEN, KV_W)
    cache_v4 = cache_v.reshape(DEC_BATCH, DEPTH, PAST_LEN, KV_W)
    chain = DEC_SEQ // S5_PIECE
    h0_ctx = jnp.zeros((N_DIRS, S5_N_SETS, BATCH, 2 * S5_SET_ST), F32)
    ks, vs, ss = [], [], []
    x_all = None
    assert DEPTH == 2
    rider_view = lambda w: w.reshape(-1, w.shape[-1])
    hosted = {
        (0, "attention"): dict(ffn_gate=ffn_w_gate[0], ffn_up=ffn_w_up[0], w_in=w_in[1],
                               w_attn_br=w_attn_br[1], w_glu=w_glu[1], w_out=w_out[1]),
        (0, "scan"): dict(moe_gate=moe_w_gate[0]),
        (1, "attention"): dict(moe_down=moe_w_down[0]),
        (1, "scan"): dict(moe_up=moe_w_up[0]),
    }
    narrowed = dict(w_in=w_in[0].astype(BF16), w_attn_br=w_attn_br[0].astype(BF16), w_glu=w_glu[0].astype(BF16),
                    w_out=w_out[0].astype(BF16), ffn_down=ffn_w_down[0].astype(BF16))

    def collect(host, results):
        for (name, wide), narrow in zip(hosted[host].items(), results):
            narrowed[name] = narrow.reshape(wide.shape)

    for l in range(DEPTH):
        mod = mod_all[l]
        first = x_all is None
        x_streams = ((x_prompt.reshape(CTX_TOKENS, D_MODEL), x_sample.reshape(DEC_TOKENS, D_MODEL), 0) if first
                     else (x_all, x_all, N_CTX_TILES))
        u, q, k_ctx, v_ctx, k_dec, v_dec, sgs, sga, *x_copy = _in_projection(
            *x_streams, mod, norm1_g[l][None, :], narrowed["w_in"], ones_bd,
            jnp.tile(q_norm_g[l], N_HEADS)[None, :], jnp.tile(k_norm_g[l], N_KV_HEADS)[None, :],
            rope_cos, rope_sin, first)
        if first:
            x_all = x_copy[0]
        mixer_weights = [narrowed[name] for name in ("w_attn_br", "w_glu", "w_out")]

        attn_ctx, *riders_done = _attention(q, k_ctx, v_ctx, BATCH, SEQ, 0, "attention_context",
                                            riders=[rider_view(w) for w in hosted[(l, "attention")].values()])
        collect((l, "attention"), riders_done)
        attn_dec, = _attention(q, k_dec, v_dec, DEC_BATCH, DEC_SEQ, CTX_TOKENS, "attention_latent",
                               cache=(cache_k4, cache_v4, l))
        ks.append(k_ctx.reshape(BATCH, SEQ, N_KV_HEADS, HEAD_DIM))
        vs.append(v_ctx.reshape(BATCH, SEQ, N_KV_HEADS, HEAD_DIM))

        bd, a_rows, cd = _s5_operators(l, P)
        d_skip = ssm_d[l][None, :]
        y_ctx, fin, *riders_done = _s5_branch(u, 0, CTX_TOKENS, bd, a_rows, cd, h0_ctx, d_skip, 1, "s5_context",
                                              riders=[rider_view(w) for w in hosted[(l, "scan")].values()])
        collect((l, "scan"), riders_done)
        h0_dec = state_ssm[:, l].reshape(DEC_BATCH, N_DIRS, S5_N_SETS, S5_GROUP_SET * STATE_N, 2)
        h0_dec = h0_dec.transpose(1, 2, 0, 4, 3).reshape(N_DIRS, S5_N_SETS, DEC_BATCH, 2 * S5_SET_ST)
        h0_dec = jnp.repeat(h0_dec, chain, axis=2)
        y_dec, _ = _s5_branch(u, CTX_TOKENS, DEC_TOKENS, bd, a_rows, cd, h0_dec, d_skip, chain, "s5_latent")
        fin = fin.reshape(N_DIRS, S5_N_SETS, BATCH, 2, S5_GROUP_SET, STATE_N)
        ss.append(fin.transpose(2, 0, 1, 4, 5, 3).reshape(BATCH, N_DIRS, N_GROUPS, STATE_N, 2))

        x_all = _mixer_out(attn_ctx, attn_dec, y_ctx, y_dec, sgs, sga, x_all, mod, *mixer_weights)

        i = l // 2
        last_layer = l == DEPTH - 1
        if l % 2 == 0:
            out = _dense_ffn(x_all, mod, norm2_g[l][None, :], narrowed["ffn_gate"], narrowed["ffn_up"],
                             narrowed["ffn_down"], final_norm_g[None, :], last_layer)
            x_all = out
            streams = (out[:CTX_TOKENS], out[CTX_TOKENS:])
        else:
            router_pad = jnp.pad(moe_router[i].astype(F32), ((0, 0), (0, LANES - N_EXPERTS)))
            h2, route = _prenorm_router(x_all, mod, norm2_g[l][None, :], router_pad)
            row_token, row_slot, expert, valid = _route_plan(route)
            y_slots = _moe_ffn(h2, row_token, row_slot, narrowed["moe_gate"], narrowed["moe_up"],
                               narrowed["moe_down"], expert, valid, D_FF_EXPERT // 2)
            out = _combine(x_all, y_slots, route, mod, final_norm_g[None, :], last_layer)
            if last_layer:
                streams = tuple(out)
            else:
                x_all = out[0]
                streams = (x_all[:CTX_TOKENS], x_all[CTX_TOKENS:])

    y_prompt = streams[0].reshape(BATCH, SEQ, D_MODEL)
    y_sample = streams[1].reshape(DEC_BATCH, DEC_SEQ, D_MODEL)
    return (y_prompt, y_sample, jnp.stack(ks, axis=1), jnp.stack(vs, axis=1), jnp.stack(ss, axis=1))
```

```python
import functools
import math

import jax
import jax.numpy as jnp
from jax import lax
from jax.experimental import pallas as pl
from jax.experimental.pallas import tpu as pltpu

F32 = jnp.float32
BF16 = jnp.bfloat16

D_MODEL = 1024
BATCH = 32
SEQ = 256
DEPTH = 2
DEC_BATCH = 2
DEC_SEQ = 1024
PAST_LEN = 256
GRID_W = 64
HEAD_DIM = 64
N_HEADS = 16
N_KV_HEADS = 4
ROPE_THETA = 10000.0
AXIS_DIM = HEAD_DIM // 2
HALF = HEAD_DIM // 2
SSM_GROUP = 16
SSM_WIDTH = 512
N_GROUPS = SSM_WIDTH // SSM_GROUP
STATE_N = 64
N_DIRS = 2
D_FF = 2816
N_EXPERTS = 8
D_FF_EXPERT = 3584
EPS = 1e-6
Q_W = N_HEADS * HEAD_DIM
KV_W = N_KV_HEADS * HEAD_DIM
IN_W = SSM_WIDTH + Q_W + 2 * KV_W + 2 * D_MODEL

CTX_TOKENS = BATCH * SEQ
DEC_TOKENS = DEC_BATCH * DEC_SEQ
ALL_TOKENS = CTX_TOKENS + DEC_TOKENS

LANES = 128
MXU_WIDTH = 256
FF_CHUNK = MXU_WIDTH
SUBLANES = 8
VMEM_LIMIT_BYTES = 56 * 1024 * 1024

ROW_TILE = 512
ATTN_Q_TILE = 256
S5_GROUP_SET = 8
S5_SET_CH = S5_GROUP_SET * SSM_GROUP
S5_SET_ST = S5_GROUP_SET * STATE_N
S5_N_SETS = N_GROUPS // S5_GROUP_SET
S5_SCAN_UNROLL = 4
S5_PIECE = SEQ
N_MOD_ROWS = 1 + DEC_BATCH


def _params(sem):
    return pltpu.CompilerParams(dimension_semantics=sem, vmem_limit_bytes=VMEM_LIMIT_BYTES)


def _resident(shape, index_map):
    return pl.BlockSpec(shape, index_map, pipeline_mode=pl.Buffered(1))


def _sigmoid(x):
    return 1.0 / (1.0 + jnp.exp(-x))


def _mod_row(i, tm):
    nctx = CTX_TOKENS // tm
    per_b = DEC_SEQ // tm
    return jnp.where(i < nctx, 0, 1 + (i - nctx) // per_b)


def _rope_block(i, tm):
    nctx = CTX_TOKENS // tm
    per_b = DEC_SEQ // tm
    return jnp.maximum(i - nctx, 0) % per_b


N_CTX_TILES = CTX_TOKENS // ROW_TILE


def _stream_specs(width, dec_first_block=0):
    return (pl.BlockSpec((ROW_TILE, width), lambda i, *_: (jnp.minimum(i, N_CTX_TILES - 1), 0)),
            pl.BlockSpec((ROW_TILE, width), lambda i, *_: (dec_first_block + jnp.maximum(i - N_CTX_TILES, 0), 0)))


def _is_ctx_tile():
    return pl.program_id(0) < N_CTX_TILES


def _pick_stream(ctx_ref, dec_ref):
    return jnp.where(_is_ctx_tile(), ctx_ref[...], dec_ref[...])


def _rms_modulate(x, g, scale, shift):
    ms = jnp.mean(x * x, axis=-1, keepdims=True)
    return (x * lax.rsqrt(ms + EPS) * g) * (1.0 + scale) + shift


def _ada_kernel(c_ref, w_ref, b_ref, o_ref):
    c = c_ref[...]
    act = c * _sigmoid(c)
    o_ref[0] = jnp.dot(act.astype(BF16), w_ref[0].astype(BF16), preferred_element_type=F32) + b_ref[0]


def _ada_modulation(c_rows, w_ada, b_ada):
    tn = 1536
    return pl.pallas_call(
        _ada_kernel,
        grid=(DEPTH, 6 * D_MODEL // tn),
        in_specs=[
            pl.BlockSpec((SUBLANES, D_MODEL), lambda l, n: (0, 0)),
            pl.BlockSpec((1, D_MODEL, tn), lambda l, n: (l, 0, n)),
            pl.BlockSpec((1, 1, tn), lambda l, n: (l, 0, n)),
        ],
        out_specs=pl.BlockSpec((1, SUBLANES, tn), lambda l, n: (l, 0, n)),
        out_shape=jax.ShapeDtypeStruct((DEPTH, SUBLANES, 6 * D_MODEL), F32),
        compiler_params=_params(("parallel", "parallel")),
        name="ada_modulation",
    )(c_rows, w_ada, b_ada.reshape(DEPTH, 1, 6 * D_MODEL))


def _rotate_pairs(x, cos, sin_signed):
    n = x.shape[-1]
    lane = lax.broadcasted_iota(jnp.int32, x.shape, 1)
    first_half = (lane % HEAD_DIM) < HALF
    partner = jnp.where(first_half, pltpu.roll(x, n - HALF, 1), pltpu.roll(x, HALF, 1))
    reps = n // LANES
    return x * jnp.tile(cos, (1, reps)) + partner * jnp.tile(sin_signed, (1, reps))


def _inproj_kernel(x_ctx_ref, x_dec_ref, mod_ref, g_ref, w_ref, ones_ref, qg_ref, kg_ref, cos_ref, sin_ref,
                   u_ref, q_ref, kc_ref, vc_ref, kd_ref, vd_ref, sgs_ref, sga_ref, *x_copy_ref):
    mod = mod_ref[0]
    x = _pick_stream(x_ctx_ref, x_dec_ref)
    if x_copy_ref:
        x_copy_ref[0][...] = x
    h = _rms_modulate(x, g_ref[...], mod[1:2], mod[0:1]).astype(BF16)

    def proj(lo, width):
        return jnp.dot(h, w_ref[:, lo:lo + width], preferred_element_type=F32)

    def head_norm(t, gain):
        sq = (t * t).astype(BF16)
        ss = jnp.concatenate(
            [jnp.dot(sq[:, c:c + MXU_WIDTH], ones_ref[...], preferred_element_type=F32)
             for c in range(0, t.shape[-1], MXU_WIDTH)], axis=-1)
        return t * lax.rsqrt(ss * (1.0 / HEAD_DIM) + EPS) * gain

    o = 0
    u_ref[...] = proj(o, SSM_WIDTH)
    o += SSM_WIDTH
    q = head_norm(proj(o, Q_W), qg_ref[...]) * (1.0 / math.sqrt(HEAD_DIM))
    o += Q_W
    k = head_norm(proj(o, KV_W), kg_ref[...])
    o += KV_W
    v = proj(o, KV_W)
    o += KV_W
    sgs_ref[...] = _sigmoid(proj(o, D_MODEL))
    o += D_MODEL
    sga_ref[...] = _sigmoid(proj(o, D_MODEL))

    @pl.when(_is_ctx_tile())
    def _():
        q_ref[...] = q.astype(BF16)
        kc_ref[...] = k
        vc_ref[...] = v

    @pl.when(jnp.logical_not(_is_ctx_tile()))
    def _():
        cos = cos_ref[...]
        sin = sin_ref[...]
        q_ref[...] = _rotate_pairs(q, cos, sin).astype(BF16)
        kd_ref[...] = _rotate_pairs(k, cos, sin)
        vd_ref[...] = v


def _in_projection(x_ctx, x_dec, dec_first_block, mod, norm_g, w_in_bf16, ones_bd, q_gain, k_gain,
                   rope_cos, rope_sin, copy_x):
    tm = ROW_TILE
    n_tiles = ALL_TOKENS // tm
    row = lambda w: pl.BlockSpec((tm, w), lambda i: (i, 0))
    kv_ctx, kv_dec = _stream_specs(KV_W)
    out_specs = [row(SSM_WIDTH), row(Q_W), kv_ctx, kv_ctx, kv_dec, kv_dec, row(D_MODEL), row(D_MODEL)]
    out_shape = [jax.ShapeDtypeStruct((ALL_TOKENS, SSM_WIDTH), F32), jax.ShapeDtypeStruct((ALL_TOKENS, Q_W), BF16),
                 jax.ShapeDtypeStruct((CTX_TOKENS, KV_W), F32), jax.ShapeDtypeStruct((CTX_TOKENS, KV_W), F32),
                 jax.ShapeDtypeStruct((DEC_TOKENS, KV_W), F32), jax.ShapeDtypeStruct((DEC_TOKENS, KV_W), F32),
                 jax.ShapeDtypeStruct((ALL_TOKENS, D_MODEL), F32), jax.ShapeDtypeStruct((ALL_TOKENS, D_MODEL), F32)]
    if copy_x:
        out_specs.append(row(D_MODEL))
        out_shape.append(jax.ShapeDtypeStruct((ALL_TOKENS, D_MODEL), F32))
    return pl.pallas_call(
        _inproj_kernel,
        grid=(n_tiles,),
        in_specs=[
            *_stream_specs(D_MODEL, dec_first_block),
            pl.BlockSpec((1, 6, D_MODEL), lambda i: (_mod_row(i, tm), 0, 0)),
            _resident((1, D_MODEL), lambda i: (0, 0)),
            _resident((D_MODEL, IN_W), lambda i: (0, 0)),
            _resident((MXU_WIDTH, MXU_WIDTH), lambda i: (0, 0)),
            _resident((1, Q_W), lambda i: (0, 0)),
            _resident((1, KV_W), lambda i: (0, 0)),
            pl.BlockSpec((tm, LANES), lambda i: (_rope_block(i, tm), 0)),
            pl.BlockSpec((tm, LANES), lambda i: (_rope_block(i, tm), 0)),
        ],
        out_specs=out_specs,
        out_shape=out_shape,
        compiler_params=_params(("arbitrary",)),
        name="in_projection",
    )(x_ctx, x_dec, mod, norm_g, w_in_bf16, ones_bd, q_gain, k_gain, rope_cos, rope_sin)


def _rider_specs(riders, n_steps, linear_step):
    specs = [pl.BlockSpec((w.shape[0] // n_steps, w.shape[1]), lambda *idx: (linear_step(*idx), 0)) for w in riders]
    return specs, [jax.ShapeDtypeStruct(w.shape, BF16) for w in riders]


def _narrow_riders(wide_refs, narrow_refs):
    for wide_ref, narrow_ref in zip(wide_refs, narrow_refs):
        narrow_ref[...] = wide_ref[...].astype(BF16)


def _attn_kernel(*refs, with_cache, n_riders):
    if n_riders:
        n_in = len(refs) - 1 - 2 * n_riders
        _narrow_riders(refs[n_in:n_in + n_riders], refs[n_in + n_riders + 1:])
        refs = refs[:n_in] + refs[n_in + n_riders:n_in + n_riders + 1]
    if with_cache:
        q_ref, k_ref, v_ref, ck_ref, cv_ref, o_ref = refs
        k_parts = (lambda sl: ck_ref[0, 0, :, sl], lambda sl: k_ref[0, :, sl])
        v_parts = (lambda sl: cv_ref[0, 0, :, sl], lambda sl: v_ref[0, :, sl])
    else:
        q_ref, k_ref, v_ref, o_ref = refs
        k_parts = (lambda sl: k_ref[0, :, sl],)
        v_parts = (lambda sl: v_ref[0, :, sl],)
    lane = lax.broadcasted_iota(jnp.int32, (1, LANES), 1)
    low = lane < HEAD_DIM
    groups_per_slab = LANES // HEAD_DIM
    rep = N_HEADS // N_KV_HEADS
    for g in range(N_KV_HEADS):
        slab = g // groups_per_slab
        own_low = (g % groups_per_slab) == 0
        keep = low if own_low else jnp.logical_not(low)

        def halves(parts):
            lanes = slice(slab * LANES, (slab + 1) * LANES)
            rows = jnp.concatenate([part(lanes) for part in parts], axis=0)
            same = jnp.where(keep, rows, 0.0)
            swapped = pltpu.roll(same, HEAD_DIM, 1)
            lo, hi = (same, swapped) if own_low else (swapped, same)
            return lo.astype(BF16), hi.astype(BF16)

        k_lo, k_hi = halves(k_parts)
        v_lo, v_hi = halves(v_parts)
        for j in range(g * rep // groups_per_slab, (g + 1) * rep // groups_per_slab):
            qs = q_ref[0, :, j * LANES:(j + 1) * LANES]
            acc = None
            for kk, vv in ((k_lo, v_lo), (k_hi, v_hi)):
                s = lax.dot_general(qs, kk, (((1,), (1,)), ((), ())), preferred_element_type=F32)
                p = jnp.exp(s - jnp.max(s, axis=-1, keepdims=True))
                denom = jnp.sum(p, axis=-1, keepdims=True)
                part = jnp.dot(p.astype(BF16), vv, preferred_element_type=F32) / denom
                acc = part if acc is None else acc + part
            o_ref[:, j * LANES:(j + 1) * LANES] = acc.astype(BF16)


def _attention(q, k, v, n_batch, seq_len, first_token, name, cache=None, riders=()):
    tq = ATTN_Q_TILE
    per_b = seq_len // tq
    q_tiles = q.reshape(ALL_TOKENS // tq, tq, Q_W)
    first_tile = first_token // tq
    kv_spec = pl.BlockSpec((1, seq_len, KV_W), lambda b, i: (b, 0, 0))
    in_specs = [pl.BlockSpec((1, tq, Q_W), lambda b, i: (first_tile + b * per_b + i, 0, 0)), kv_spec, kv_spec]
    args = [q_tiles, k.reshape(n_batch, seq_len, KV_W), v.reshape(n_batch, seq_len, KV_W)]
    if cache is not None:
        cache_k, cache_v, layer = cache
        cache_spec = pl.BlockSpec((1, 1, PAST_LEN, KV_W), lambda b, i: (b, layer, 0, 0))
        in_specs += [cache_spec, cache_spec]
        args += [cache_k, cache_v]
    rider_specs, rider_shapes = _rider_specs(riders, n_batch * per_b, lambda b, i: b * per_b + i)
    return pl.pallas_call(
        functools.partial(_attn_kernel, with_cache=cache is not None, n_riders=len(riders)),
        grid=(n_batch, per_b),
        in_specs=in_specs + rider_specs,
        out_specs=[pl.BlockSpec((tq, Q_W), lambda b, i: (b * per_b + i, 0))] + rider_specs,
        out_shape=[jax.ShapeDtypeStruct((n_batch * seq_len, Q_W), BF16)] + rider_shapes,
        compiler_params=_params(("parallel", "parallel")),
        name=name,
    )(*args, *riders)


def _s5_kernel(u_ref, bd_ref, a_ref, cd_ref, h0_ref, dskip_ref, *rest, chain, n_riders):
    wide_refs, (y_ref, hfin_ref), rest = rest[:n_riders], rest[n_riders:n_riders + 2], rest[n_riders + 2:]
    narrow_refs, (u_scr, y_scr, bu_scr, hs_scr) = rest[:n_riders], rest[n_riders:]
    _narrow_riders(wide_refs, narrow_refs)
    _s5_body(u_ref, bd_ref, a_ref, cd_ref, h0_ref, dskip_ref, y_ref, hfin_ref, u_scr, y_scr, bu_scr, hs_scr, chain)


def _s5_body(u_ref, bd_ref, a_ref, cd_ref, h0_ref, dskip_ref, y_ref, hfin_ref,
             u_scr, y_scr, bu_scr, hs_scr, chain):
    seq_len = u_ref.shape[0] // SUBLANES
    forward = pl.program_id(2) == 0
    last_dir = pl.program_id(2) == pl.num_programs(2) - 1
    a_re = jnp.broadcast_to(a_ref[0, 0, :, :S5_SET_ST], (SUBLANES, S5_SET_ST))
    a_im = jnp.broadcast_to(a_ref[0, 0, :, S5_SET_ST:], (SUBLANES, S5_SET_ST))

    @pl.when(forward)
    def _():
        for p in range(SUBLANES):
            u_scr[pl.ds(p, seq_len, stride=SUBLANES), :] = u_ref[p * seq_len:(p + 1) * seq_len, :]
        y_scr[...] = u_scr[...] * dskip_ref[...]

    bu_scr[...] = jnp.dot(u_scr[...].astype(BF16), bd_ref[0, 0], preferred_element_type=F32)

    def scan(init, store):
        def step(t, hc):
            h_re, h_im = hc
            tt = jnp.where(forward, t, seq_len - 1 - t)
            r = pl.multiple_of(tt * SUBLANES, SUBLANES)
            n_re = a_re * h_re - a_im * h_im + bu_scr[pl.ds(r, SUBLANES), :S5_SET_ST]
            n_im = a_re * h_im + a_im * h_re + bu_scr[pl.ds(r, SUBLANES), S5_SET_ST:]
            if store:
                hs_scr[pl.ds(r, SUBLANES), :S5_SET_ST] = n_re
                hs_scr[pl.ds(r, SUBLANES), S5_SET_ST:] = n_im
            return n_re, n_im

        return lax.fori_loop(0, seq_len, step, init, unroll=S5_SCAN_UNROLL)

    h0 = h0_ref[0, 0]
    init = (h0[:, :S5_SET_ST], h0[:, S5_SET_ST:])
    if chain > 1:
        zero = jnp.zeros((SUBLANES, S5_SET_ST), F32)
        end_re, end_im = scan((zero, zero), store=False)
        p_re, p_im = a_re, a_im
        for _ in range(seq_len.bit_length() - 1):
            p_re, p_im = p_re * p_re - p_im * p_im, 2.0 * (p_re * p_im)
        piece = lax.broadcasted_iota(jnp.int32, (SUBLANES, S5_SET_ST), 0) % chain
        entry = piece == jnp.where(forward, 0, chain - 1)

        def from_neighbour(x):
            return jnp.where(forward, pltpu.roll(x, 1, 0), pltpu.roll(x, SUBLANES - 1, 0))

        i_re = jnp.where(entry, init[0], 0.0)
        i_im = jnp.where(entry, init[1], 0.0)
        for _ in range(chain - 1):
            o_re = p_re * i_re - p_im * i_im + end_re
            o_im = p_re * i_im + p_im * i_re + end_im
            i_re = jnp.where(entry, init[0], from_neighbour(o_re))
            i_im = jnp.where(entry, init[1], from_neighbour(o_im))
        init = (i_re, i_im)
    h_re, h_im = scan(init, store=True)
    hfin_ref[0, 0, :, :S5_SET_ST] = h_re
    hfin_ref[0, 0, :, S5_SET_ST:] = h_im
    y_scr[...] += jnp.dot(hs_scr[...].astype(BF16), cd_ref[0, 0], preferred_element_type=F32)

    @pl.when(last_dir)
    def _():
        for p in range(SUBLANES):
            y_ref[p * seq_len:(p + 1) * seq_len, :] = y_scr[pl.ds(p, seq_len, stride=SUBLANES), :]


def _s5_branch(u_all, first_token, n_tokens, bd, a_bar, cd, h0, d_skip, chain, name, riders=()):
    seq_len = S5_PIECE
    rows = seq_len * SUBLANES
    nb = n_tokens // seq_len
    first_block = first_token // rows
    assert seq_len & (seq_len - 1) == 0 and SUBLANES % chain == 0 and first_token % rows == 0
    st2 = 2 * S5_SET_ST
    par = lambda shape: pl.BlockSpec(shape, lambda b, s, d: (d, s, 0, 0))
    state = pl.BlockSpec((1, 1, SUBLANES, st2), lambda b, s, d: (d, s, b, 0))
    seq = pl.BlockSpec((rows, S5_SET_CH), lambda b, s, d: (b, s))
    grid = (nb // SUBLANES, S5_N_SETS, N_DIRS)
    in_specs = [
        pl.BlockSpec((rows, S5_SET_CH), lambda b, s, d: (first_block + b, s)),
        par((1, 1, S5_SET_CH, st2)),
        par((1, 1, 1, st2)),
        par((1, 1, st2, S5_SET_CH)),
        state,
        pl.BlockSpec((1, S5_SET_CH), lambda b, s, d: (0, s)),
    ]
    out_specs = [seq, state]
    out_shape = [jax.ShapeDtypeStruct((n_tokens, SSM_WIDTH), F32),
                 jax.ShapeDtypeStruct((N_DIRS, S5_N_SETS, nb, st2), F32)]
    args = [u_all, bd, a_bar, cd, h0, d_skip, *riders]
    rider_specs, rider_shapes = _rider_specs(riders, grid[0] * grid[1] * grid[2],
                                             lambda b, s, d: (b * S5_N_SETS + s) * N_DIRS + d)
    return pl.pallas_call(
        functools.partial(_s5_kernel, chain=chain, n_riders=len(riders)),
        grid=grid,
        in_specs=in_specs + rider_specs,
        out_specs=out_specs + rider_specs,
        out_shape=out_shape + rider_shapes,
        scratch_shapes=[pltpu.VMEM((rows, S5_SET_CH), F32), pltpu.VMEM((rows, S5_SET_CH), F32),
                        pltpu.VMEM((rows, st2), F32), pltpu.VMEM((rows, st2), F32)],
        compiler_params=_params(("parallel", "parallel", "arbitrary")),
        name=name,
    )(*args)


def _s5_operators(l, P):
    a_re, a_im = P['ssm_a_re'][l].astype(F32), P['ssm_a_im'][l].astype(F32)
    dt = jnp.exp(P['ssm_log_dt'][l].astype(F32))[..., None]
    mag = jnp.exp(a_re * dt)
    ab_re, ab_im = mag * jnp.cos(a_im * dt), mag * jnp.sin(a_im * dt)
    den = a_re * a_re + a_im * a_im
    k_re = ((ab_re - 1.0) * a_re + ab_im * a_im) / den
    k_im = (ab_im * a_re - (ab_re - 1.0) * a_im) / den
    b_re, b_im = P['ssm_b_re'][l].astype(F32), P['ssm_b_im'][l].astype(F32)
    bb_re = k_re[..., None] * b_re - k_im[..., None] * b_im
    bb_im = k_re[..., None] * b_im + k_im[..., None] * b_re
    eye = jnp.eye(S5_GROUP_SET, dtype=F32)

    def sets(t):
        return t.reshape(N_DIRS, S5_N_SETS, S5_GROUP_SET, *t.shape[2:])

    def b_operator(part):
        return jnp.einsum('dsgnp,gh->dsgphn', part, eye).reshape(N_DIRS, S5_N_SETS, S5_SET_CH, S5_SET_ST)

    def c_operator(part):
        return jnp.einsum('dsgpn,gh->dshngp', part, eye).reshape(N_DIRS, S5_N_SETS, S5_SET_ST, S5_SET_CH)

    bd = jnp.concatenate([b_operator(sets(bb_re)), b_operator(sets(bb_im))], axis=-1).astype(BF16)
    c_re = sets(P['ssm_c_re'][l].astype(F32))
    c_im = sets(P['ssm_c_im'][l].astype(F32))
    cd = jnp.concatenate([c_operator(c_re), -c_operator(c_im)], axis=-2).astype(BF16)
    row = lambda t: sets(t).reshape(N_DIRS, S5_N_SETS, 1, S5_SET_ST)
    a_rows = jnp.concatenate([row(ab_re), row(ab_im)], axis=-1)
    return bd, a_rows, cd


def _mixer_out_kernel(attn_ctx_ref, attn_dec_ref, y_ctx_ref, y_dec_ref, sgs_ref, sga_ref, x_ref, mod_ref,
                      wab_ref, wglu_ref, wout_ref, o_ref):
    attn_br = jnp.dot(_pick_stream(attn_ctx_ref, attn_dec_ref), wab_ref[...], preferred_element_type=F32)
    y = _pick_stream(y_ctx_ref, y_dec_ref)
    gelu = 0.5 * y * (1.0 + jnp.tanh(math.sqrt(2.0 / math.pi) * (y + 0.044715 * (y * y * y))))
    glu = jnp.dot(gelu.astype(BF16), wglu_ref[...], preferred_element_type=F32)
    ssm_br = glu[:, :D_MODEL] * _sigmoid(glu[:, D_MODEL:])
    merged = sgs_ref[...] * ssm_br + sga_ref[...] * attn_br
    out = jnp.dot(merged.astype(BF16), wout_ref[...], preferred_element_type=F32)
    o_ref[...] = x_ref[...] + mod_ref[0][2:3] * out


def _mixer_out(attn_ctx, attn_dec, y_ctx, y_dec, sgs, sga, x_all, mod, w_attn_br, w_glu, w_out):
    tm = ROW_TILE
    row = lambda w: pl.BlockSpec((tm, w), lambda i: (i, 0))
    return pl.pallas_call(
        _mixer_out_kernel,
        grid=(ALL_TOKENS // tm,),
        in_specs=[
            *_stream_specs(Q_W), *_stream_specs(SSM_WIDTH),
            row(D_MODEL), row(D_MODEL), row(D_MODEL),
            pl.BlockSpec((1, 6, D_MODEL), lambda i: (_mod_row(i, tm), 0, 0)),
            _resident((Q_W, D_MODEL), lambda i: (0, 0)),
            _resident((SSM_WIDTH, 2 * D_MODEL), lambda i: (0, 0)),
            _resident((D_MODEL, D_MODEL), lambda i: (0, 0)),
        ],
        out_specs=row(D_MODEL),
        out_shape=jax.ShapeDtypeStruct((ALL_TOKENS, D_MODEL), F32),
        compiler_params=_params(("parallel",)),
        name="mixer_out",
    )(attn_ctx, attn_dec, y_ctx, y_dec, sgs, sga, x_all, mod, w_attn_br, w_glu, w_out)


ROUTE_IDX_LANE = 0
ROUTE_W_LANE = 2


ROW_CHUNKS = D_MODEL // LANES


def _store_rows_as_tiles(ref, lead, value):
    n = value.shape[0]
    for c in range(ROW_CHUNKS):
        ref[(*lead, pl.ds(c, n, stride=ROW_CHUNKS), slice(None))] = value[:, c * LANES:(c + 1) * LANES]


def _load_rows_from_tiles(ref, lead, n, first=0, every=1):
    stride = every * ROW_CHUNKS
    return jnp.concatenate(
        [ref[(*lead, pl.ds(first * ROW_CHUNKS + c, n, stride=stride), slice(None))] for c in range(ROW_CHUNKS)],
        axis=-1)


def _router_kernel(x_ref, mod_ref, g_ref, r_ref, h_ref, route_ref):
    mod = mod_ref[0]
    h = _rms_modulate(x_ref[...], g_ref[...], mod[4:5], mod[3:4])
    _store_rows_as_tiles(h_ref, (), h)
    r = r_ref[...]
    h_hi, r_hi = h.astype(BF16), r.astype(BF16)
    h_lo = (h - h_hi.astype(F32)).astype(BF16)
    r_lo = (r - r_hi.astype(F32)).astype(BF16)
    logits = (jnp.dot(h_hi, r_hi, preferred_element_type=F32) + jnp.dot(h_lo, r_hi, preferred_element_type=F32)
              + jnp.dot(h_hi, r_lo, preferred_element_type=F32))
    lane = lax.broadcasted_iota(jnp.int32, logits.shape, 1).astype(F32)
    logits = jnp.where(lane < N_EXPERTS, logits, -jnp.inf)
    top1 = jnp.max(logits, axis=-1, keepdims=True)
    idx1 = jnp.min(jnp.where(logits == top1, lane, float(LANES)), axis=-1, keepdims=True)
    rest = jnp.where(lane == idx1, -jnp.inf, logits)
    top2 = jnp.max(rest, axis=-1, keepdims=True)
    idx2 = jnp.min(jnp.where(rest == top2, lane, float(LANES)), axis=-1, keepdims=True)
    e2 = jnp.exp(top2 - top1)
    w1 = 1.0 / (1.0 + e2)
    w2 = e2 / (1.0 + e2)
    route = jnp.where(lane == ROUTE_IDX_LANE, idx1, 0.0)
    route = jnp.where(lane == ROUTE_IDX_LANE + 1, idx2, route)
    route = jnp.where(lane == ROUTE_W_LANE, w1, route)
    route_ref[...] = jnp.where(lane == ROUTE_W_LANE + 1, w2, route)


def _prenorm_router(x_all, mod, norm_g, router_pad):
    tm = ROW_TILE
    row = lambda w: pl.BlockSpec((tm, w), lambda i: (i, 0))
    in_specs = [row(D_MODEL),
                pl.BlockSpec((1, 6, D_MODEL), lambda i: (_mod_row(i, tm), 0, 0)),
                _resident((1, D_MODEL), lambda i: (0, 0))]
    return pl.pallas_call(
        _router_kernel, grid=(ALL_TOKENS // tm,),
        in_specs=in_specs + [_resident((D_MODEL, LANES), lambda i: (0, 0))],
        out_specs=[pl.BlockSpec((tm * ROW_CHUNKS, LANES), lambda i: (i, 0)), row(LANES)],
        out_shape=[jax.ShapeDtypeStruct((ALL_TOKENS * ROW_CHUNKS, LANES), F32),
                   jax.ShapeDtypeStruct((ALL_TOKENS, LANES), F32)],
        compiler_params=_params(("parallel",)), name="prenorm_router",
    )(x_all, mod, norm_g, router_pad)


def _swiglu_chains(h, gate_cols, up_cols, down_rows, d_ff):
    out = None
    for c in range(d_ff // FF_CHUNK):
        cols = slice(c * FF_CHUNK, (c + 1) * FF_CHUNK)
        a = jnp.dot(h, gate_cols(cols), preferred_element_type=F32)
        b = jnp.dot(h, up_cols(cols), preferred_element_type=F32)
        inter = ((a * _sigmoid(a)) * b).astype(BF16)
        part = jnp.dot(inter, down_rows(cols), preferred_element_type=F32)
        out = part if out is None else out + part
    return out


def _dense_ffn_kernel(x_ref, mod_ref, g_ref, wg_ref, wu_ref, wd_ref, fg_ref, o_ref, *, final):
    mod = mod_ref[0]
    x = x_ref[...]
    h = _rms_modulate(x, g_ref[...], mod[4:5], mod[3:4]).astype(BF16)
    f = _swiglu_chains(h, lambda c: wg_ref[:, c], lambda c: wu_ref[:, c], lambda c: wd_ref[c, :],
                       wg_ref.shape[-1])
    x = x + mod[5:6] * f
    if final:
        ms = jnp.mean(x * x, axis=-1, keepdims=True)
        x = x * lax.rsqrt(ms + EPS) * fg_ref[...]
    o_ref[...] = x


def _dense_ffn(x_all, mod, norm_g, w_gate, w_up, w_down, final_g, final):
    tm = ROW_TILE
    d_ff = w_gate.shape[-1]
    row = pl.BlockSpec((tm, D_MODEL), lambda i: (i, 0))
    return pl.pallas_call(
        functools.partial(_dense_ffn_kernel, final=final),
        grid=(ALL_TOKENS // tm,),
        in_specs=[row,
                  pl.BlockSpec((1, 6, D_MODEL), lambda i: (_mod_row(i, tm), 0, 0)),
                  _resident((1, D_MODEL), lambda i: (0, 0)),
                  _resident((D_MODEL, d_ff), lambda i: (0, 0)),
                  _resident((D_MODEL, d_ff), lambda i: (0, 0)),
                  _resident((d_ff, D_MODEL), lambda i: (0, 0)),
                  _resident((1, D_MODEL), lambda i: (0, 0))],
        out_specs=row,
        out_shape=jax.ShapeDtypeStruct((ALL_TOKENS, D_MODEL), F32),
        compiler_params=_params(("parallel",)),
        name="ffn_dense",
    )(x_all, mod, norm_g, w_gate, w_up, w_down, final_g)


def _tile_of(row):
    return pl.ds(pl.multiple_of(row * ROW_CHUNKS, ROW_CHUNKS), ROW_CHUNKS)


def _moe_ffn_kernel(slot_ref, tile_ref, expert_ref, valid_ref, x_ref, wg_ref, wu_ref, wd_ref, y_ref,
                    ybuf, acc, sem):
    s = pl.program_id(0)
    j = pl.program_id(1)
    n_j = pl.num_programs(1)
    tm = acc.shape[0]
    share = tm // n_j
    cur = s % 2
    other = 1 - cur

    @pl.when(jnp.logical_and(s == 0, j == 0))
    def _():
        ybuf[...] = jnp.zeros_like(ybuf)
        spill = pltpu.make_async_copy(
            ybuf.at[0], y_ref.at[pl.ds(TOP_K * ALL_TOKENS * ROW_CHUNKS, tm * ROW_CHUNKS)], sem)
        spill.start()
        spill.wait()

    @pl.when(s >= 0)
    def _():
        row0 = j * share
        for r in range(share):
            row = row0 + r
            pltpu.make_async_copy(ybuf.at[other, _tile_of(row)], y_ref.at[_tile_of(slot_ref[s * tm + row])],
                                  sem).start()

    @pl.when(valid_ref[s] == 1)
    def _():
        h = _load_rows_from_tiles(x_ref, (), tm).astype(BF16)
        contrib = _swiglu_chains(h, lambda c: wg_ref[0, :, c], lambda c: wu_ref[0, :, c],
                                 lambda c: wd_ref[0, c, :], wg_ref.shape[-1])

        @pl.when(j == 0)
        def _():
            acc[...] = contrib

        @pl.when(jnp.logical_and(j != 0, j != n_j - 1))
        def _():
            acc[...] += contrib

        @pl.when(j == n_j - 1)
        def _():
            _store_rows_as_tiles(ybuf, (cur,), acc[...] + contrib)

    pltpu.make_async_copy(ybuf.at[0, pl.ds(0, share * ROW_CHUNKS)], y_ref.at[pl.ds(0, share * ROW_CHUNKS)], sem).wait()


def _moe_ffn(x_sorted, row_slot, w_gate, w_up, w_down, step_tile, step_expert, step_valid, tf):
    tm = MOE_ROW_TILE
    d_ff = w_gate.shape[-1]
    n_steps = step_expert.shape[0]
    n_ff = d_ff // tf
    assert n_ff >= 2
    ff = lambda s, j, f: jnp.where(f[s] == 1, j, n_ff - 1)
    grid_spec = pltpu.PrefetchScalarGridSpec(
        num_scalar_prefetch=4,
        grid=(n_steps, n_ff),
        in_specs=[
            pl.BlockSpec((tm * ROW_CHUNKS, LANES), lambda s, j, d, t, e, f: (t[s], 0)),
            pl.BlockSpec((1, D_MODEL, tf), lambda s, j, d, t, e, f: (e[s], 0, ff(s, j, f))),
            pl.BlockSpec((1, D_MODEL, tf), lambda s, j, d, t, e, f: (e[s], 0, ff(s, j, f))),
            pl.BlockSpec((1, tf, D_MODEL), lambda s, j, d, t, e, f: (e[s], ff(s, j, f), 0)),
        ],
        out_specs=pl.BlockSpec(memory_space=pl.ANY),
        scratch_shapes=[pltpu.VMEM((2, tm * ROW_CHUNKS, LANES), F32), pltpu.VMEM((tm, D_MODEL), F32),
                        pltpu.SemaphoreType.DMA(())],
    )
    return pl.pallas_call(
        _moe_ffn_kernel,
        grid_spec=grid_spec,
        out_shape=jax.ShapeDtypeStruct(((TOP_K * ALL_TOKENS + tm) * ROW_CHUNKS, LANES), F32),
        compiler_params=_params(("arbitrary", "arbitrary")),
        name="ffn_experts",
    )(row_slot, step_tile, step_expert, step_valid, x_sorted, w_gate, w_up, w_down)


def _dispatch_kernel(pos_ref, clear_ref, h_ref, xs_ref, zeros, sem):
    tm = h_ref.shape[0] // ROW_CHUNKS
    i = pl.program_id(0)

    @pl.when(i == 0)
    def _():
        zeros[...] = jnp.zeros_like(zeros)
        clears = [pltpu.make_async_copy(zeros, xs_ref.at[pl.ds(pl.multiple_of(clear_ref[e] * ROW_CHUNKS, ROW_CHUNKS),
                                                               zeros.shape[0])], sem)
                  for e in range(clear_ref.shape[0])]
        for c in clears:
            c.start()
            c.wait()

    def send(r, carry):
        for k in range(TOP_K):
            dst = pos_ref[TOP_K * (i * tm + r) + k]
            pltpu.make_async_copy(h_ref.at[_tile_of(r)], xs_ref.at[_tile_of(dst)], sem).start(priority=k)
        return carry

    lax.fori_loop(0, tm, send, 0, unroll=8)
    for k in range(TOP_K):
        pltpu.make_async_copy(h_ref, xs_ref.at[pl.ds(0, tm * ROW_CHUNKS)], sem).wait()


def _dispatch(h_tiles, pos, tail_rows):
    tm = ROW_TILE
    grid_spec = pltpu.PrefetchScalarGridSpec(
        num_scalar_prefetch=2,
        grid=(ALL_TOKENS // tm,),
        in_specs=[pl.BlockSpec((tm * ROW_CHUNKS, LANES), lambda i, p, t: (i, 0))],
        out_specs=pl.BlockSpec(memory_space=pl.ANY),
        scratch_shapes=[pltpu.VMEM((MOE_ROW_TILE * ROW_CHUNKS, LANES), F32), pltpu.SemaphoreType.DMA(())],
    )
    return pl.pallas_call(
        _dispatch_kernel,
        grid_spec=grid_spec,
        out_shape=jax.ShapeDtypeStruct((MOE_ROWS * ROW_CHUNKS, LANES), F32),
        compiler_params=_params(("arbitrary",)),
        name="moe_dispatch",
    )(pos, tail_rows, h_tiles)


TOP_K = 2
MOE_ROW_TILE = 512
MOE_ROWS = TOP_K * ALL_TOKENS + N_EXPERTS * MOE_ROW_TILE


def _route_plan(route):
    tm = MOE_ROW_TILE
    n_slots = TOP_K * ALL_TOKENS
    n_steps = MOE_ROWS // tm + 1
    expert = route[:, ROUTE_IDX_LANE:ROUTE_IDX_LANE + TOP_K].astype(jnp.int32).reshape(-1)
    experts = jnp.arange(N_EXPERTS, dtype=jnp.int32)[None, :]
    order = jnp.argsort(expert, stable=True).astype(jnp.int32)
    counts = jnp.sum((expert[:, None] == experts).astype(jnp.int32), axis=0)
    tiles = (counts + tm - 1) // tm
    tile_end = jnp.cumsum(tiles)
    starts = (tile_end - tiles) * tm
    before = jnp.cumsum(counts) - counts
    last = tile_end[-1] - 1
    step = jnp.arange(n_steps, dtype=jnp.int32)
    step_expert = jnp.sum((jnp.minimum(step, last)[:, None] >= tile_end[None, :]).astype(jnp.int32), axis=1)
    step_valid = (step <= last).astype(jnp.int32)
    row = jnp.arange((n_steps + 2) * tm, dtype=jnp.int32) - tm
    row_expert = jnp.sum(((row // tm)[:, None] >= tile_end[None, :]).astype(jnp.int32), axis=1)
    pick = (jnp.minimum(row_expert, N_EXPERTS - 1)[:, None] == experts).astype(jnp.int32)
    rank = row - jnp.sum(pick * starts[None, :], axis=1)
    real = (row >= 0) & (row_expert < N_EXPERTS) & (rank < jnp.sum(pick * counts[None, :], axis=1))
    sorted_pos = jnp.clip(jnp.sum(pick * before[None, :], axis=1) + rank, 0, n_slots - 1)
    slot = order[sorted_pos]
    row_slot = jnp.where(real, (slot % TOP_K) * ALL_TOKENS + slot // TOP_K, n_slots + row % tm)
    place = jnp.argsort(order).astype(jnp.int32)
    own = (expert[:, None] == experts).astype(jnp.int32)
    pos = place + jnp.sum(own * (starts - before)[None, :], axis=1)
    group_tails = jnp.maximum(starts + tiles * tm - tm, 0)
    never_full = (n_slots // tm + jnp.arange(N_EXPERTS, dtype=jnp.int32)) * tm
    clear_rows = jnp.concatenate([group_tails, never_full])
    return (pos.astype(jnp.int32), clear_rows.astype(jnp.int32), row_slot.astype(jnp.int32),
            jnp.minimum(step, last), step_expert.astype(jnp.int32), step_valid)


def _combine_kernel(x_ref, route_ref, mod_ref, g_ref, y0_ref, y1_ref, *out_refs, final):
    route = route_ref[...]
    tm = x_ref.shape[0]
    f = (route[:, ROUTE_W_LANE:ROUTE_W_LANE + 1] * _load_rows_from_tiles(y0_ref, (), tm)
         + route[:, ROUTE_W_LANE + 1:ROUTE_W_LANE + 2] * _load_rows_from_tiles(y1_ref, (), tm))
    x = x_ref[...] + mod_ref[0][5:6] * f
    if not final:
        out_refs[0][...] = x
        return
    ms = jnp.mean(x * x, axis=-1, keepdims=True)
    x = x * lax.rsqrt(ms + EPS) * g_ref[...]
    is_ctx = _is_ctx_tile()
    @pl.when(is_ctx)
    def _():
        out_refs[0][...] = x

    @pl.when(jnp.logical_not(is_ctx))
    def _():
        out_refs[1][...] = x


def _combine(x_all, y_slots, route, mod, final_g, final):
    tm = ROW_TILE
    n_tiles = ALL_TOKENS // tm
    row = lambda w: pl.BlockSpec((tm, w), lambda i: (i, 0))
    if final:
        out_specs = list(_stream_specs(D_MODEL))
        out_shape = [jax.ShapeDtypeStruct((CTX_TOKENS, D_MODEL), F32),
                     jax.ShapeDtypeStruct((DEC_TOKENS, D_MODEL), F32)]
    else:
        out_specs = [row(D_MODEL)]
        out_shape = [jax.ShapeDtypeStruct((ALL_TOKENS, D_MODEL), F32)]
    return pl.pallas_call(
        functools.partial(_combine_kernel, final=final),
        grid=(n_tiles,),
        in_specs=[row(D_MODEL), row(LANES),
                  pl.BlockSpec((1, 6, D_MODEL), lambda i: (_mod_row(i, tm), 0, 0)),
                  _resident((1, D_MODEL), lambda i: (0, 0)),
                  pl.BlockSpec((tm * ROW_CHUNKS, LANES), lambda i: (i, 0)),
                  pl.BlockSpec((tm * ROW_CHUNKS, LANES), lambda i: (n_tiles + i, 0))],
        out_specs=out_specs,
        out_shape=out_shape,
        compiler_params=_params(("arbitrary",)),
        name="moe_combine",
    )(x_all, route, mod, final_g, y_slots, y_slots)


def _rope_tables():
    rows = DEC_SEQ // GRID_W
    row = jnp.repeat(jnp.arange(rows, dtype=F32), GRID_W)
    col = jnp.tile(jnp.arange(GRID_W, dtype=F32), rows)
    inv = ROPE_THETA ** (-jnp.arange(0, AXIS_DIM, 2, dtype=F32) / AXIS_DIM)
    ang = jnp.concatenate([row[:, None] * inv, col[:, None] * inv], axis=-1)
    cos, sin = jnp.cos(ang), jnp.sin(ang)
    reps = LANES // HEAD_DIM
    cos_l = jnp.tile(jnp.concatenate([cos, cos], axis=-1), (1, reps))
    sin_l = jnp.tile(jnp.concatenate([-sin, sin], axis=-1), (1, reps))
    return cos_l, sin_l


def kernel(x_prompt, x_sample, cache_k, cache_v, state_ssm, c, c_ctx, w_ada, b_ada, norm1_g, norm2_g, w_in, q_norm_g, k_norm_g, ssm_a_re, ssm_a_im, ssm_log_dt, ssm_b_re, ssm_b_im, ssm_c_re, ssm_c_im, ssm_d, w_glu, w_attn_br, w_out, ffn_w_gate, ffn_w_up, ffn_w_down, moe_router, moe_w_gate, moe_w_up, moe_w_down, final_norm_g):
    P = dict(ssm_a_re=ssm_a_re, ssm_a_im=ssm_a_im, ssm_log_dt=ssm_log_dt, ssm_b_re=ssm_b_re,
             ssm_b_im=ssm_b_im, ssm_c_re=ssm_c_re, ssm_c_im=ssm_c_im)

    c_rows = jnp.zeros((SUBLANES, D_MODEL), F32).at[0].set(c_ctx).at[1:1 + DEC_BATCH].set(c)
    mod_all = _ada_modulation(c_rows, w_ada, b_ada)
    mod_all = mod_all[:, :N_MOD_ROWS].reshape(DEPTH, N_MOD_ROWS, 6, D_MODEL)

    rope_cos, rope_sin = _rope_tables()
    head_id = jnp.arange(MXU_WIDTH, dtype=jnp.int32) // HEAD_DIM
    ones_bd = (head_id[:, None] == head_id[None, :]).astype(BF16)

    cache_k4 = cache_k.reshape(DEC_BATCH, DEPTH, PAST_LEN, KV_W)
    cache_v4 = cache_v.reshape(DEC_BATCH, DEPTH, PAST_LEN, KV_W)
    chain = DEC_SEQ // S5_PIECE
    h0_ctx = jnp.zeros((N_DIRS, S5_N_SETS, BATCH, 2 * S5_SET_ST), F32)
    ks, vs, ss = [], [], []
    x_all = None
    assert DEPTH == 2
    rider_view = lambda w: w.reshape(-1, w.shape[-1])
    hosted = {
        (0, "attention"): dict(ffn_gate=ffn_w_gate[0], ffn_up=ffn_w_up[0], w_in=w_in[1],
                               w_attn_br=w_attn_br[1], w_glu=w_glu[1], w_out=w_out[1]),
        (0, "scan"): dict(moe_gate=moe_w_gate[0]),
        (1, "attention"): dict(moe_down=moe_w_down[0]),
        (1, "scan"): dict(moe_up=moe_w_up[0]),
    }
    narrowed = dict(w_in=w_in[0].astype(BF16), w_attn_br=w_attn_br[0].astype(BF16), w_glu=w_glu[0].astype(BF16),
                    w_out=w_out[0].astype(BF16), ffn_down=ffn_w_down[0].astype(BF16))

    def collect(host, results):
        for (name, wide), narrow in zip(hosted[host].items(), results):
            narrowed[name] = narrow.reshape(wide.shape)

    for l in range(DEPTH):
        mod = mod_all[l]
        first = x_all is None
        x_streams = ((x_prompt.reshape(CTX_TOKENS, D_MODEL), x_sample.reshape(DEC_TOKENS, D_MODEL), 0) if first
                     else (x_all, x_all, N_CTX_TILES))
        u, q, k_ctx, v_ctx, k_dec, v_dec, sgs, sga, *x_copy = _in_projection(
            *x_streams, mod, norm1_g[l][None, :], narrowed["w_in"], ones_bd,
            jnp.tile(q_norm_g[l], N_HEADS)[None, :], jnp.tile(k_norm_g[l], N_KV_HEADS)[None, :],
            rope_cos, rope_sin, first)
        if first:
            x_all = x_copy[0]
        mixer_weights = [narrowed[name] for name in ("w_attn_br", "w_glu", "w_out")]

        attn_ctx, *riders_done = _attention(q, k_ctx, v_ctx, BATCH, SEQ, 0, "attention_context",
                                            riders=[rider_view(w) for w in hosted[(l, "attention")].values()])
        collect((l, "attention"), riders_done)
        attn_dec, = _attention(q, k_dec, v_dec, DEC_BATCH, DEC_SEQ, CTX_TOKENS, "attention_latent",
                               cache=(cache_k4, cache_v4, l))
        ks.append(k_ctx.reshape(BATCH, SEQ, N_KV_HEADS, HEAD_DIM))
        vs.append(v_ctx.reshape(BATCH, SEQ, N_KV_HEADS, HEAD_DIM))

        bd, a_rows, cd = _s5_operators(l, P)
        d_skip = ssm_d[l][None, :]
        y_ctx, fin, *riders_done = _s5_branch(u, 0, CTX_TOKENS, bd, a_rows, cd, h0_ctx, d_skip, 1, "s5_context",
                                              riders=[rider_view(w) for w in hosted[(l, "scan")].values()])
        collect((l, "scan"), riders_done)
        h0_dec = state_ssm[:, l].reshape(DEC_BATCH, N_DIRS, S5_N_SETS, S5_GROUP_SET * STATE_N, 2)
        h0_dec = h0_dec.transpose(1, 2, 0, 4, 3).reshape(N_DIRS, S5_N_SETS, DEC_BATCH, 2 * S5_SET_ST)
        h0_dec = jnp.repeat(h0_dec, chain, axis=2)
        y_dec, _ = _s5_branch(u, CTX_TOKENS, DEC_TOKENS, bd, a_rows, cd, h0_dec, d_skip, chain, "s5_latent")
        fin = fin.reshape(N_DIRS, S5_N_SETS, BATCH, 2, S5_GROUP_SET, STATE_N)
        ss.append(fin.transpose(2, 0, 1, 4, 5, 3).reshape(BATCH, N_DIRS, N_GROUPS, STATE_N, 2))

        x_all = _mixer_out(attn_ctx, attn_dec, y_ctx, y_dec, sgs, sga, x_all, mod, *mixer_weights)

        i = l // 2
        last_layer = l == DEPTH - 1
        if l % 2 == 0:
            out = _dense_ffn(x_all, mod, norm2_g[l][None, :], narrowed["ffn_gate"], narrowed["ffn_up"],
                             narrowed["ffn_down"], final_norm_g[None, :], last_layer)
            x_all = out
            streams = (out[:CTX_TOKENS], out[CTX_TOKENS:])
        else:
            router_pad = jnp.pad(moe_router[i].astype(F32), ((0, 0), (0, LANES - N_EXPERTS)))
            h2, route = _prenorm_router(x_all, mod, norm2_g[l][None, :], router_pad)
            pos, clear_rows, row_slot, tile, expert, valid = _route_plan(route)
            x_sorted = _dispatch(h2, pos, clear_rows)
            y_slots = _moe_ffn(x_sorted, row_slot, narrowed["moe_gate"], narrowed["moe_up"],
                               narrowed["moe_down"], tile, expert, valid, D_FF_EXPERT // 2)
            out = _combine(x_all, y_slots, route, mod, final_norm_g[None, :], last_layer)
            if last_layer:
                streams = tuple(out)
            else:
                x_all = out[0]
                streams = (x_all[:CTX_TOKENS], x_all[CTX_TOKENS:])

    y_prompt = streams[0].reshape(BATCH, SEQ, D_MODEL)
    y_sample = streams[1].reshape(DEC_BATCH, DEC_SEQ, D_MODEL)
    return (y_prompt, y_sample, jnp.stack(ks, axis=1), jnp.stack(vs, axis=1), jnp.stack(ss, axis=1))
```

```python
import functools
import math

import jax
import jax.numpy as jnp
from jax import lax
from jax.experimental import pallas as pl
from jax.experimental.pallas import tpu as pltpu

F32 = jnp.float32
BF16 = jnp.bfloat16

D_MODEL = 1024
BATCH = 32
SEQ = 256
DEPTH = 2
DEC_BATCH = 2
DEC_SEQ = 1024
PAST_LEN = 256
GRID_W = 64
HEAD_DIM = 64
N_HEADS = 16
N_KV_HEADS = 4
ROPE_THETA = 10000.0
AXIS_DIM = HEAD_DIM // 2
HALF = HEAD_DIM // 2
SSM_GROUP = 16
SSM_WIDTH = 512
N_GROUPS = SSM_WIDTH // SSM_GROUP
STATE_N = 64
N_DIRS = 2
D_FF = 2816
N_EXPERTS = 8
D_FF_EXPERT = 3584
EPS = 1e-6
Q_W = N_HEADS * HEAD_DIM
KV_W = N_KV_HEADS * HEAD_DIM
IN_W = SSM_WIDTH + Q_W + 2 * KV_W + 2 * D_MODEL

CTX_TOKENS = BATCH * SEQ
DEC_TOKENS = DEC_BATCH * DEC_SEQ
ALL_TOKENS = CTX_TOKENS + DEC_TOKENS

LANES = 128
MXU_WIDTH = 256
FF_CHUNK = MXU_WIDTH
SUBLANES = 8
VMEM_LIMIT_BYTES = 56 * 1024 * 1024

ROW_TILE = 512
ATTN_Q_TILE = 256
S5_GROUP_SET = 8
S5_SET_CH = S5_GROUP_SET * SSM_GROUP
S5_SET_ST = S5_GROUP_SET * STATE_N
S5_N_SETS = N_GROUPS // S5_GROUP_SET
S5_SCAN_UNROLL = 4
S5_PIECE = SEQ
N_MOD_ROWS = 1 + DEC_BATCH


def _params(sem):
    return pltpu.CompilerParams(dimension_semantics=sem, vmem_limit_bytes=VMEM_LIMIT_BYTES)


def _resident(shape, index_map):
    return pl.BlockSpec(shape, index_map, pipeline_mode=pl.Buffered(1))


def _sigmoid(x):
    return 1.0 / (1.0 + jnp.exp(-x))


def _mod_row(i, tm):
    nctx = CTX_TOKENS // tm
    per_b = DEC_SEQ // tm
    return jnp.where(i < nctx, 0, 1 + (i - nctx) // per_b)


def _rope_block(i, tm):
    nctx = CTX_TOKENS // tm
    per_b = DEC_SEQ // tm
    return jnp.maximum(i - nctx, 0) % per_b


N_CTX_TILES = CTX_TOKENS // ROW_TILE


def _stream_specs(width, dec_first_block=0):
    return (pl.BlockSpec((ROW_TILE, width), lambda i, *_: (jnp.minimum(i, N_CTX_TILES - 1), 0)),
            pl.BlockSpec((ROW_TILE, width), lambda i, *_: (dec_first_block + jnp.maximum(i - N_CTX_TILES, 0), 0)))


def _is_ctx_tile():
    return pl.program_id(0) < N_CTX_TILES


def _pick_stream(ctx_ref, dec_ref):
    return jnp.where(_is_ctx_tile(), ctx_ref[...], dec_ref[...])


def _rms_modulate(x, g, scale, shift):
    ms = jnp.mean(x * x, axis=-1, keepdims=True)
    return (x * lax.rsqrt(ms + EPS) * g) * (1.0 + scale) + shift


def _ada_kernel(c_ref, w_ref, b_ref, o_ref):
    c = c_ref[...]
    act = c * _sigmoid(c)
    o_ref[0] = jnp.dot(act.astype(BF16), w_ref[0].astype(BF16), preferred_element_type=F32) + b_ref[0]


def _ada_modulation(c_rows, w_ada, b_ada):
    tn = 1536
    return pl.pallas_call(
        _ada_kernel,
        grid=(DEPTH, 6 * D_MODEL // tn),
        in_specs=[
            pl.BlockSpec((SUBLANES, D_MODEL), lambda l, n: (0, 0)),
            pl.BlockSpec((1, D_MODEL, tn), lambda l, n: (l, 0, n)),
            pl.BlockSpec((1, 1, tn), lambda l, n: (l, 0, n)),
        ],
        out_specs=pl.BlockSpec((1, SUBLANES, tn), lambda l, n: (l, 0, n)),
        out_shape=jax.ShapeDtypeStruct((DEPTH, SUBLANES, 6 * D_MODEL), F32),
        compiler_params=_params(("parallel", "parallel")),
        name="ada_modulation",
    )(c_rows, w_ada, b_ada.reshape(DEPTH, 1, 6 * D_MODEL))


def _rotate_pairs(x, cos, sin_signed):
    n = x.shape[-1]
    lane = lax.broadcasted_iota(jnp.int32, x.shape, 1)
    first_half = (lane % HEAD_DIM) < HALF
    partner = jnp.where(first_half, pltpu.roll(x, n - HALF, 1), pltpu.roll(x, HALF, 1))
    reps = n // LANES
    return x * jnp.tile(cos, (1, reps)) + partner * jnp.tile(sin_signed, (1, reps))


def _inproj_kernel(x_ctx_ref, x_dec_ref, mod_ref, g_ref, w_ref, ones_ref, qg_ref, kg_ref, cos_ref, sin_ref,
                   u_ref, q_ref, kc_ref, vc_ref, kd_ref, vd_ref, sgs_ref, sga_ref, *x_copy_ref):
    mod = mod_ref[0]
    x = _pick_stream(x_ctx_ref, x_dec_ref)
    if x_copy_ref:
        x_copy_ref[0][...] = x
    h = _rms_modulate(x, g_ref[...], mod[1:2], mod[0:1]).astype(BF16)

    def proj(lo, width):
        return jnp.dot(h, w_ref[:, lo:lo + width], preferred_element_type=F32)

    def head_norm(t, gain):
        sq = (t * t).astype(BF16)
        ss = jnp.concatenate(
            [jnp.dot(sq[:, c:c + MXU_WIDTH], ones_ref[...], preferred_element_type=F32)
             for c in range(0, t.shape[-1], MXU_WIDTH)], axis=-1)
        return t * lax.rsqrt(ss * (1.0 / HEAD_DIM) + EPS) * gain

    is_ctx = _is_ctx_tile()
    cos = cos_ref[...]
    sin = sin_ref[...]

    def chunks(lo, width):
        for c in range(0, width, MXU_WIDTH):
            yield slice(c, c + MXU_WIDTH), proj(lo + c, MXU_WIDTH)

    def positioned(t):
        return jnp.where(is_ctx, t, _rotate_pairs(t, cos, sin))

    o = 0
    u_ref[...] = proj(o, SSM_WIDTH)
    o += SSM_WIDTH
    q = head_norm(proj(o, Q_W), qg_ref[...]) * (1.0 / math.sqrt(HEAD_DIM))
    q_ref[...] = positioned(q).astype(BF16)
    o += Q_W
    k = positioned(head_norm(proj(o, KV_W), kg_ref[...]))
    o += KV_W
    v = proj(o, KV_W)
    o += KV_W
    for cols, t in chunks(o, D_MODEL):
        sgs_ref[:, cols] = _sigmoid(t)
    o += D_MODEL
    for cols, t in chunks(o, D_MODEL):
        sga_ref[:, cols] = _sigmoid(t)

    @pl.when(is_ctx)
    def _():
        kc_ref[...] = k
        vc_ref[...] = v

    @pl.when(jnp.logical_not(is_ctx))
    def _():
        kd_ref[...] = k
        vd_ref[...] = v


def _in_projection(x_ctx, x_dec, dec_first_block, mod, norm_g, w_in_bf16, ones_bd, q_gain, k_gain,
                   rope_cos, rope_sin, copy_x):
    tm = ROW_TILE
    n_tiles = ALL_TOKENS // tm
    row = lambda w: pl.BlockSpec((tm, w), lambda i: (i, 0))
    kv_ctx, kv_dec = _stream_specs(KV_W)
    out_specs = [row(SSM_WIDTH), row(Q_W), kv_ctx, kv_ctx, kv_dec, kv_dec, row(D_MODEL), row(D_MODEL)]
    out_shape = [jax.ShapeDtypeStruct((ALL_TOKENS, SSM_WIDTH), F32), jax.ShapeDtypeStruct((ALL_TOKENS, Q_W), BF16),
                 jax.ShapeDtypeStruct((CTX_TOKENS, KV_W), F32), jax.ShapeDtypeStruct((CTX_TOKENS, KV_W), F32),
                 jax.ShapeDtypeStruct((DEC_TOKENS, KV_W), F32), jax.ShapeDtypeStruct((DEC_TOKENS, KV_W), F32),
                 jax.ShapeDtypeStruct((ALL_TOKENS, D_MODEL), F32), jax.ShapeDtypeStruct((ALL_TOKENS, D_MODEL), F32)]
    if copy_x:
        out_specs.append(row(D_MODEL))
        out_shape.append(jax.ShapeDtypeStruct((ALL_TOKENS, D_MODEL), F32))
    return pl.pallas_call(
        _inproj_kernel,
        grid=(n_tiles,),
        in_specs=[
            *_stream_specs(D_MODEL, dec_first_block),
            pl.BlockSpec((1, 6, D_MODEL), lambda i: (_mod_row(i, tm), 0, 0)),
            _resident((1, D_MODEL), lambda i: (0, 0)),
            _resident((D_MODEL, IN_W), lambda i: (0, 0)),
            _resident((MXU_WIDTH, MXU_WIDTH), lambda i: (0, 0)),
            _resident((1, Q_W), lambda i: (0, 0)),
            _resident((1, KV_W), lambda i: (0, 0)),
            pl.BlockSpec((tm, LANES), lambda i: (_rope_block(i, tm), 0)),
            pl.BlockSpec((tm, LANES), lambda i: (_rope_block(i, tm), 0)),
        ],
        out_specs=out_specs,
        out_shape=out_shape,
        compiler_params=_params(("arbitrary",)),
        name="in_projection",
    )(x_ctx, x_dec, mod, norm_g, w_in_bf16, ones_bd, q_gain, k_gain, rope_cos, rope_sin)


def _rider_specs(riders, n_steps, linear_step):
    specs = [pl.BlockSpec((w.shape[0] // n_steps, w.shape[1]), lambda *idx: (linear_step(*idx), 0)) for w in riders]
    return specs, [jax.ShapeDtypeStruct(w.shape, BF16) for w in riders]


def _narrow_riders(wide_refs, narrow_refs):
    for wide_ref, narrow_ref in zip(wide_refs, narrow_refs):
        narrow_ref[...] = wide_ref[...].astype(BF16)


def _attn_kernel(*refs, with_cache, n_riders):
    if n_riders:
        n_in = len(refs) - 1 - 2 * n_riders
        _narrow_riders(refs[n_in:n_in + n_riders], refs[n_in + n_riders + 1:])
        refs = refs[:n_in] + refs[n_in + n_riders:n_in + n_riders + 1]
    if with_cache:
        q_ref, k_ref, v_ref, ck_ref, cv_ref, o_ref = refs
        k_parts = (lambda sl: ck_ref[0, 0, :, sl], lambda sl: k_ref[0, :, sl])
        v_parts = (lambda sl: cv_ref[0, 0, :, sl], lambda sl: v_ref[0, :, sl])
    else:
        q_ref, k_ref, v_ref, o_ref = refs
        k_parts = (lambda sl: k_ref[0, :, sl],)
        v_parts = (lambda sl: v_ref[0, :, sl],)
    lane = lax.broadcasted_iota(jnp.int32, (1, LANES), 1)
    low = lane < HEAD_DIM
    groups_per_slab = LANES // HEAD_DIM
    rep = N_HEADS // N_KV_HEADS
    for g in range(N_KV_HEADS):
        slab = g // groups_per_slab
        own_low = (g % groups_per_slab) == 0
        keep = low if own_low else jnp.logical_not(low)

        def halves(parts):
            lanes = slice(slab * LANES, (slab + 1) * LANES)
            rows = jnp.concatenate([part(lanes) for part in parts], axis=0)
            same = jnp.where(keep, rows, 0.0)
            swapped = pltpu.roll(same, HEAD_DIM, 1)
            lo, hi = (same, swapped) if own_low else (swapped, same)
            return lo.astype(BF16), hi.astype(BF16)

        k_lo, k_hi = halves(k_parts)
        v_lo, v_hi = halves(v_parts)
        for j in range(g * rep // groups_per_slab, (g + 1) * rep // groups_per_slab):
            qs = q_ref[0, :, j * LANES:(j + 1) * LANES]
            acc = None
            for kk, vv in ((k_lo, v_lo), (k_hi, v_hi)):
                s = lax.dot_general(qs, kk, (((1,), (1,)), ((), ())), preferred_element_type=F32)
                p = jnp.exp(s - jnp.max(s, axis=-1, keepdims=True))
                denom = jnp.sum(p, axis=-1, keepdims=True)
                part = jnp.dot(p.astype(BF16), vv, preferred_element_type=F32) / denom
                acc = part if acc is None else acc + part
            o_ref[:, j * LANES:(j + 1) * LANES] = acc.astype(BF16)


def _attention(q, k, v, n_batch, seq_len, first_token, name, cache=None, riders=()):
    tq = ATTN_Q_TILE
    per_b = seq_len // tq
    q_tiles = q.reshape(ALL_TOKENS // tq, tq, Q_W)
    first_tile = first_token // tq
    kv_spec = pl.BlockSpec((1, seq_len, KV_W), lambda b, i: (b, 0, 0))
    in_specs = [pl.BlockSpec((1, tq, Q_W), lambda b, i: (first_tile + b * per_b + i, 0, 0)), kv_spec, kv_spec]
    args = [q_tiles, k.reshape(n_batch, seq_len, KV_W), v.reshape(n_batch, seq_len, KV_W)]
    if cache is not None:
        cache_k, cache_v, layer = cache
        cache_spec = pl.BlockSpec((1, 1, PAST_LEN, KV_W), lambda b, i: (b, layer, 0, 0))
        in_specs += [cache_spec, cache_spec]
        args += [cache_k, cache_v]
    rider_specs, rider_shapes = _rider_specs(riders, n_batch * per_b, lambda b, i: b * per_b + i)
    return pl.pallas_call(
        functools.partial(_attn_kernel, with_cache=cache is not None, n_riders=len(riders)),
        grid=(n_batch, per_b),
        in_specs=in_specs + rider_specs,
        out_specs=[pl.BlockSpec((tq, Q_W), lambda b, i: (b * per_b + i, 0))] + rider_specs,
        out_shape=[jax.ShapeDtypeStruct((n_batch * seq_len, Q_W), BF16)] + rider_shapes,
        compiler_params=_params(("parallel", "parallel")),
        name=name,
    )(*args, *riders)


def _s5_kernel(u_ref, bd_ref, a_ref, cd_ref, h0_ref, dskip_ref, *rest, chain, n_riders):
    wide_refs, (y_ref, hfin_ref), rest = rest[:n_riders], rest[n_riders:n_riders + 2], rest[n_riders + 2:]
    narrow_refs, (u_scr, y_scr, bu_scr, hs_scr) = rest[:n_riders], rest[n_riders:]
    _narrow_riders(wide_refs, narrow_refs)
    _s5_body(u_ref, bd_ref, a_ref, cd_ref, h0_ref, dskip_ref, y_ref, hfin_ref, u_scr, y_scr, bu_scr, hs_scr, chain)


def _s5_body(u_ref, bd_ref, a_ref, cd_ref, h0_ref, dskip_ref, y_ref, hfin_ref,
             u_scr, y_scr, bu_scr, hs_scr, chain):
    seq_len = u_ref.shape[0] // SUBLANES
    forward = pl.program_id(2) == 0
    last_dir = pl.program_id(2) == pl.num_programs(2) - 1
    a_re = jnp.broadcast_to(a_ref[0, 0, :, :S5_SET_ST], (SUBLANES, S5_SET_ST))
    a_im = jnp.broadcast_to(a_ref[0, 0, :, S5_SET_ST:], (SUBLANES, S5_SET_ST))

    @pl.when(forward)
    def _():
        for p in range(SUBLANES):
            u_scr[pl.ds(p, seq_len, stride=SUBLANES), :] = u_ref[p * seq_len:(p + 1) * seq_len, :]
        y_scr[...] = u_scr[...] * dskip_ref[...]

    bu_scr[...] = jnp.dot(u_scr[...].astype(BF16), bd_ref[0, 0], preferred_element_type=F32)

    def scan(init, store):
        def step(t, hc):
            h_re, h_im = hc
            tt = jnp.where(forward, t, seq_len - 1 - t)
            r = pl.multiple_of(tt * SUBLANES, SUBLANES)
            n_re = a_re * h_re - a_im * h_im + bu_scr[pl.ds(r, SUBLANES), :S5_SET_ST]
            n_im = a_re * h_im + a_im * h_re + bu_scr[pl.ds(r, SUBLANES), S5_SET_ST:]
            if store:
                hs_scr[pl.ds(r, SUBLANES), :S5_SET_ST] = n_re
                hs_scr[pl.ds(r, SUBLANES), S5_SET_ST:] = n_im
            return n_re, n_im

        return lax.fori_loop(0, seq_len, step, init, unroll=S5_SCAN_UNROLL)

    h0 = h0_ref[0, 0]
    init = (h0[:, :S5_SET_ST], h0[:, S5_SET_ST:])
    if chain > 1:
        zero = jnp.zeros((SUBLANES, S5_SET_ST), F32)
        end_re, end_im = scan((zero, zero), store=False)
        p_re, p_im = a_re, a_im
        for _ in range(seq_len.bit_length() - 1):
            p_re, p_im = p_re * p_re - p_im * p_im, 2.0 * (p_re * p_im)
        piece = lax.broadcasted_iota(jnp.int32, (SUBLANES, S5_SET_ST), 0) % chain
        entry = piece == jnp.where(forward, 0, chain - 1)

        def from_neighbour(x):
            return jnp.where(forward, pltpu.roll(x, 1, 0), pltpu.roll(x, SUBLANES - 1, 0))

        i_re = jnp.where(entry, init[0], 0.0)
        i_im = jnp.where(entry, init[1], 0.0)
        for _ in range(chain - 1):
            o_re = p_re * i_re - p_im * i_im + end_re
            o_im = p_re * i_im + p_im * i_re + end_im
            i_re = jnp.where(entry, init[0], from_neighbour(o_re))
            i_im = jnp.where(entry, init[1], from_neighbour(o_im))
        init = (i_re, i_im)
    h_re, h_im = scan(init, store=True)
    hfin_ref[0, 0, :, :S5_SET_ST] = h_re
    hfin_ref[0, 0, :, S5_SET_ST:] = h_im
    y_scr[...] += jnp.dot(hs_scr[...].astype(BF16), cd_ref[0, 0], preferred_element_type=F32)

    @pl.when(last_dir)
    def _():
        for p in range(SUBLANES):
            y_ref[p * seq_len:(p + 1) * seq_len, :] = y_scr[pl.ds(p, seq_len, stride=SUBLANES), :]


def _s5_branch(u_all, first_token, n_tokens, bd, a_bar, cd, h0, d_skip, chain, name, riders=()):
    seq_len = S5_PIECE
    rows = seq_len * SUBLANES
    nb = n_tokens // seq_len
    first_block = first_token // rows
    assert seq_len & (seq_len - 1) == 0 and SUBLANES % chain == 0 and first_token % rows == 0
    st2 = 2 * S5_SET_ST
    par = lambda shape: pl.BlockSpec(shape, lambda b, s, d: (d, s, 0, 0))
    state = pl.BlockSpec((1, 1, SUBLANES, st2), lambda b, s, d: (d, s, b, 0))
    seq = pl.BlockSpec((rows, S5_SET_CH), lambda b, s, d: (b, s))
    grid = (nb // SUBLANES, S5_N_SETS, N_DIRS)
    in_specs = [
        pl.BlockSpec((rows, S5_SET_CH), lambda b, s, d: (first_block + b, s)),
        par((1, 1, S5_SET_CH, st2)),
        par((1, 1, 1, st2)),
        par((1, 1, st2, S5_SET_CH)),
        state,
        pl.BlockSpec((1, S5_SET_CH), lambda b, s, d: (0, s)),
    ]
    out_specs = [seq, state]
    out_shape = [jax.ShapeDtypeStruct((n_tokens, SSM_WIDTH), F32),
                 jax.ShapeDtypeStruct((N_DIRS, S5_N_SETS, nb, st2), F32)]
    args = [u_all, bd, a_bar, cd, h0, d_skip, *riders]
    rider_specs, rider_shapes = _rider_specs(riders, grid[0] * grid[1] * grid[2],
                                             lambda b, s, d: (b * S5_N_SETS + s) * N_DIRS + d)
    return pl.pallas_call(
        functools.partial(_s5_kernel, chain=chain, n_riders=len(riders)),
        grid=grid,
        in_specs=in_specs + rider_specs,
        out_specs=out_specs + rider_specs,
        out_shape=out_shape + rider_shapes,
        scratch_shapes=[pltpu.VMEM((rows, S5_SET_CH), F32), pltpu.VMEM((rows, S5_SET_CH), F32),
                        pltpu.VMEM((rows, st2), F32), pltpu.VMEM((rows, st2), F32)],
        compiler_params=_params(("parallel", "parallel", "arbitrary")),
        name=name,
    )(*args)


def _s5_operators(l, P):
    a_re, a_im = P['ssm_a_re'][l].astype(F32), P['ssm_a_im'][l].astype(F32)
    dt = jnp.exp(P['ssm_log_dt'][l].astype(F32))[..., None]
    mag = jnp.exp(a_re * dt)
    ab_re, ab_im = mag * jnp.cos(a_im * dt), mag * jnp.sin(a_im * dt)
    den = a_re * a_re + a_im * a_im
    k_re = ((ab_re - 1.0) * a_re + ab_im * a_im) / den
    k_im = (ab_im * a_re - (ab_re - 1.0) * a_im) / den
    b_re, b_im = P['ssm_b_re'][l].astype(F32), P['ssm_b_im'][l].astype(F32)
    bb_re = k_re[..., None] * b_re - k_im[..., None] * b_im
    bb_im = k_re[..., None] * b_im + k_im[..., None] * b_re
    eye = jnp.eye(S5_GROUP_SET, dtype=F32)

    def sets(t):
        return t.reshape(N_DIRS, S5_N_SETS, S5_GROUP_SET, *t.shape[2:])

    def b_operator(part):
        return jnp.einsum('dsgnp,gh->dsgphn', part, eye).reshape(N_DIRS, S5_N_SETS, S5_SET_CH, S5_SET_ST)

    def c_operator(part):
        return jnp.einsum('dsgpn,gh->dshngp', part, eye).reshape(N_DIRS, S5_N_SETS, S5_SET_ST, S5_SET_CH)

    bd = jnp.concatenate([b_operator(sets(bb_re)), b_operator(sets(bb_im))], axis=-1).astype(BF16)
    c_re = sets(P['ssm_c_re'][l].astype(F32))
    c_im = sets(P['ssm_c_im'][l].astype(F32))
    cd = jnp.concatenate([c_operator(c_re), -c_operator(c_im)], axis=-2).astype(BF16)
    row = lambda t: sets(t).reshape(N_DIRS, S5_N_SETS, 1, S5_SET_ST)
    a_rows = jnp.concatenate([row(ab_re), row(ab_im)], axis=-1)
    return bd, a_rows, cd


def _mixer_out_kernel(attn_ctx_ref, attn_dec_ref, y_ctx_ref, y_dec_ref, sgs_ref, sga_ref, x_ref, mod_ref,
                      wab_ref, wglu_ref, wout_ref, o_ref):
    attn_br = jnp.dot(_pick_stream(attn_ctx_ref, attn_dec_ref), wab_ref[...], preferred_element_type=F32)
    y = _pick_stream(y_ctx_ref, y_dec_ref)
    gelu = 0.5 * y * (1.0 + jnp.tanh(math.sqrt(2.0 / math.pi) * (y + 0.044715 * (y * y * y))))
    glu = jnp.dot(gelu.astype(BF16), wglu_ref[...], preferred_element_type=F32)
    ssm_br = glu[:, :D_MODEL] * _sigmoid(glu[:, D_MODEL:])
    merged = sgs_ref[...] * ssm_br + sga_ref[...] * attn_br
    out = jnp.dot(merged.astype(BF16), wout_ref[...], preferred_element_type=F32)
    o_ref[...] = x_ref[...] + mod_ref[0][2:3] * out


def _mixer_out(attn_ctx, attn_dec, y_ctx, y_dec, sgs, sga, x_all, mod, w_attn_br, w_glu, w_out):
    tm = ROW_TILE
    row = lambda w: pl.BlockSpec((tm, w), lambda i: (i, 0))
    return pl.pallas_call(
        _mixer_out_kernel,
        grid=(ALL_TOKENS // tm,),
        in_specs=[
            *_stream_specs(Q_W), *_stream_specs(SSM_WIDTH),
            row(D_MODEL), row(D_MODEL), row(D_MODEL),
            pl.BlockSpec((1, 6, D_MODEL), lambda i: (_mod_row(i, tm), 0, 0)),
            _resident((Q_W, D_MODEL), lambda i: (0, 0)),
            _resident((SSM_WIDTH, 2 * D_MODEL), lambda i: (0, 0)),
            _resident((D_MODEL, D_MODEL), lambda i: (0, 0)),
        ],
        out_specs=row(D_MODEL),
        out_shape=jax.ShapeDtypeStruct((ALL_TOKENS, D_MODEL), F32),
        compiler_params=_params(("parallel",)),
        name="mixer_out",
    )(attn_ctx, attn_dec, y_ctx, y_dec, sgs, sga, x_all, mod, w_attn_br, w_glu, w_out)


ROUTE_IDX_LANE = 0
ROUTE_W_LANE = 2


ROW_CHUNKS = D_MODEL // LANES


def _store_rows_as_tiles(ref, lead, value):
    n = value.shape[0]
    for c in range(ROW_CHUNKS):
        ref[(*lead, pl.ds(c, n, stride=ROW_CHUNKS), slice(None))] = value[:, c * LANES:(c + 1) * LANES]


def _load_rows_from_tiles(ref, lead, n, first=0, every=1):
    stride = every * ROW_CHUNKS
    return jnp.concatenate(
        [ref[(*lead, pl.ds(first * ROW_CHUNKS + c, n, stride=stride), slice(None))] for c in range(ROW_CHUNKS)],
        axis=-1)


def _router_kernel(x_ref, mod_ref, g_ref, r_ref, h_ref, route_ref):
    mod = mod_ref[0]
    h = _rms_modulate(x_ref[...], g_ref[...], mod[4:5], mod[3:4])
    _store_rows_as_tiles(h_ref, (), h)
    r = r_ref[...]
    h_hi, r_hi = h.astype(BF16), r.astype(BF16)
    h_lo = (h - h_hi.astype(F32)).astype(BF16)
    r_lo = (r - r_hi.astype(F32)).astype(BF16)
    logits = (jnp.dot(h_hi, r_hi, preferred_element_type=F32) + jnp.dot(h_lo, r_hi, preferred_element_type=F32)
              + jnp.dot(h_hi, r_lo, preferred_element_type=F32))
    lane = lax.broadcasted_iota(jnp.int32, logits.shape, 1).astype(F32)
    logits = jnp.where(lane < N_EXPERTS, logits, -jnp.inf)
    top1 = jnp.max(logits, axis=-1, keepdims=True)
    idx1 = jnp.min(jnp.where(logits == top1, lane, float(LANES)), axis=-1, keepdims=True)
    rest = jnp.where(lane == idx1, -jnp.inf, logits)
    top2 = jnp.max(rest, axis=-1, keepdims=True)
    idx2 = jnp.min(jnp.where(rest == top2, lane, float(LANES)), axis=-1, keepdims=True)
    e2 = jnp.exp(top2 - top1)
    w1 = 1.0 / (1.0 + e2)
    w2 = e2 / (1.0 + e2)
    route = jnp.where(lane == ROUTE_IDX_LANE, idx1, 0.0)
    route = jnp.where(lane == ROUTE_IDX_LANE + 1, idx2, route)
    route = jnp.where(lane == ROUTE_W_LANE, w1, route)
    route_ref[...] = jnp.where(lane == ROUTE_W_LANE + 1, w2, route)


def _prenorm_router(x_all, mod, norm_g, router_pad):
    tm = ROW_TILE
    row = lambda w: pl.BlockSpec((tm, w), lambda i: (i, 0))
    in_specs = [row(D_MODEL),
                pl.BlockSpec((1, 6, D_MODEL), lambda i: (_mod_row(i, tm), 0, 0)),
                _resident((1, D_MODEL), lambda i: (0, 0))]
    return pl.pallas_call(
        _router_kernel, grid=(ALL_TOKENS // tm,),
        in_specs=in_specs + [_resident((D_MODEL, LANES), lambda i: (0, 0))],
        out_specs=[pl.BlockSpec((tm * ROW_CHUNKS, LANES), lambda i: (i, 0)), row(LANES)],
        out_shape=[jax.ShapeDtypeStruct((ALL_TOKENS * ROW_CHUNKS, LANES), F32),
                   jax.ShapeDtypeStruct((ALL_TOKENS, LANES), F32)],
        compiler_params=_params(("parallel",)), name="prenorm_router",
    )(x_all, mod, norm_g, router_pad)


def _swiglu_chains(h, gate_cols, up_cols, down_rows, d_ff):
    out = None
    for c in range(d_ff // FF_CHUNK):
        cols = slice(c * FF_CHUNK, (c + 1) * FF_CHUNK)
        a = jnp.dot(h, gate_cols(cols), preferred_element_type=F32)
        b = jnp.dot(h, up_cols(cols), preferred_element_type=F32)
        inter = ((a * _sigmoid(a)) * b).astype(BF16)
        part = jnp.dot(inter, down_rows(cols), preferred_element_type=F32)
        out = part if out is None else out + part
    return out


def _dense_ffn_kernel(x_ref, mod_ref, g_ref, wg_ref, wu_ref, wd_ref, fg_ref, o_ref, *, final):
    mod = mod_ref[0]
    x = x_ref[...]
    h = _rms_modulate(x, g_ref[...], mod[4:5], mod[3:4]).astype(BF16)
    f = _swiglu_chains(h, lambda c: wg_ref[:, c], lambda c: wu_ref[:, c], lambda c: wd_ref[c, :],
                       wg_ref.shape[-1])
    x = x + mod[5:6] * f
    if final:
        ms = jnp.mean(x * x, axis=-1, keepdims=True)
        x = x * lax.rsqrt(ms + EPS) * fg_ref[...]
    o_ref[...] = x


def _dense_ffn(x_all, mod, norm_g, w_gate, w_up, w_down, final_g, final):
    tm = ROW_TILE
    d_ff = w_gate.shape[-1]
    row = pl.BlockSpec((tm, D_MODEL), lambda i: (i, 0))
    return pl.pallas_call(
        functools.partial(_dense_ffn_kernel, final=final),
        grid=(ALL_TOKENS // tm,),
        in_specs=[row,
                  pl.BlockSpec((1, 6, D_MODEL), lambda i: (_mod_row(i, tm), 0, 0)),
                  _resident((1, D_MODEL), lambda i: (0, 0)),
                  _resident((D_MODEL, d_ff), lambda i: (0, 0)),
                  _resident((D_MODEL, d_ff), lambda i: (0, 0)),
                  _resident((d_ff, D_MODEL), lambda i: (0, 0)),
                  _resident((1, D_MODEL), lambda i: (0, 0))],
        out_specs=row,
        out_shape=jax.ShapeDtypeStruct((ALL_TOKENS, D_MODEL), F32),
        compiler_params=_params(("parallel",)),
        name="ffn_dense",
    )(x_all, mod, norm_g, w_gate, w_up, w_down, final_g)


def _tile_of(row):
    return pl.ds(pl.multiple_of(row * ROW_CHUNKS, ROW_CHUNKS), ROW_CHUNKS)


def _moe_ffn_kernel(slot_ref, tile_ref, expert_ref, valid_ref, x_ref, wg_ref, wu_ref, wd_ref, y_ref,
                    ybuf, acc, sem):
    s = pl.program_id(0)
    j = pl.program_id(1)
    n_j = pl.num_programs(1)
    tm = acc.shape[0]
    share = tm // n_j
    cur = s % 2
    other = 1 - cur

    @pl.when(jnp.logical_and(s == 0, j == 0))
    def _():
        ybuf[...] = jnp.zeros_like(ybuf)
        spill = pltpu.make_async_copy(
            ybuf.at[0], y_ref.at[pl.ds(TOP_K * ALL_TOKENS * ROW_CHUNKS, tm * ROW_CHUNKS)], sem)
        spill.start()
        spill.wait()

    @pl.when(s >= 0)
    def _():
        row0 = j * share
        for r in range(share):
            row = row0 + r
            pltpu.make_async_copy(ybuf.at[other, _tile_of(row)], y_ref.at[_tile_of(slot_ref[s * tm + row])],
                                  sem).start()

    @pl.when(valid_ref[s] == 1)
    def _():
        h = _load_rows_from_tiles(x_ref, (), tm).astype(BF16)
        contrib = _swiglu_chains(h, lambda c: wg_ref[0, :, c], lambda c: wu_ref[0, :, c],
                                 lambda c: wd_ref[0, c, :], wg_ref.shape[-1])

        @pl.when(j == 0)
        def _():
            acc[...] = contrib

        @pl.when(jnp.logical_and(j != 0, j != n_j - 1))
        def _():
            acc[...] += contrib

        @pl.when(j == n_j - 1)
        def _():
            _store_rows_as_tiles(ybuf, (cur,), acc[...] + contrib)

    pltpu.make_async_copy(ybuf.at[0, pl.ds(0, share * ROW_CHUNKS)], y_ref.at[pl.ds(0, share * ROW_CHUNKS)], sem).wait()


def _moe_ffn(x_sorted, row_slot, w_gate, w_up, w_down, step_tile, step_expert, step_valid, tf):
    tm = MOE_ROW_TILE
    d_ff = w_gate.shape[-1]
    n_steps = step_expert.shape[0]
    n_ff = d_ff // tf
    assert n_ff >= 2
    ff = lambda s, j, f: jnp.where(f[s] == 1, j, n_ff - 1)
    grid_spec = pltpu.PrefetchScalarGridSpec(
        num_scalar_prefetch=4,
        grid=(n_steps, n_ff),
        in_specs=[
            pl.BlockSpec((tm * ROW_CHUNKS, LANES), lambda s, j, d, t, e, f: (t[s], 0)),
            pl.BlockSpec((1, D_MODEL, tf), lambda s, j, d, t, e, f: (e[s], 0, ff(s, j, f))),
            pl.BlockSpec((1, D_MODEL, tf), lambda s, j, d, t, e, f: (e[s], 0, ff(s, j, f))),
            pl.BlockSpec((1, tf, D_MODEL), lambda s, j, d, t, e, f: (e[s], ff(s, j, f), 0)),
        ],
        out_specs=pl.BlockSpec(memory_space=pl.ANY),
        scratch_shapes=[pltpu.VMEM((2, tm * ROW_CHUNKS, LANES), F32), pltpu.VMEM((tm, D_MODEL), F32),
                        pltpu.SemaphoreType.DMA(())],
    )
    return pl.pallas_call(
        _moe_ffn_kernel,
        grid_spec=grid_spec,
        out_shape=jax.ShapeDtypeStruct(((TOP_K * ALL_TOKENS + tm) * ROW_CHUNKS, LANES), F32),
        compiler_params=_params(("arbitrary", "arbitrary")),
        name="ffn_experts",
    )(row_slot, step_tile, step_expert, step_valid, x_sorted, w_gate, w_up, w_down)


def _dispatch_kernel(pos_ref, clear_ref, h_ref, xs_ref, zeros, sem):
    tm = h_ref.shape[0] // ROW_CHUNKS
    i = pl.program_id(0)

    @pl.when(i == 0)
    def _():
        zeros[...] = jnp.zeros_like(zeros)
        clears = [pltpu.make_async_copy(zeros, xs_ref.at[pl.ds(pl.multiple_of(clear_ref[e] * ROW_CHUNKS, ROW_CHUNKS),
                                                               zeros.shape[0])], sem)
                  for e in range(clear_ref.shape[0])]
        for c in clears:
            c.start()
            c.wait()

    def send(r, carry):
        for k in range(TOP_K):
            dst = pos_ref[TOP_K * (i * tm + r) + k]
            pltpu.make_async_copy(h_ref.at[_tile_of(r)], xs_ref.at[_tile_of(dst)], sem).start(priority=k)
        return carry

    lax.fori_loop(0, tm, send, 0, unroll=8)
    for k in range(TOP_K):
        pltpu.make_async_copy(h_ref, xs_ref.at[pl.ds(0, tm * ROW_CHUNKS)], sem).wait()


def _dispatch(h_tiles, pos, tail_rows):
    tm = ROW_TILE
    grid_spec = pltpu.PrefetchScalarGridSpec(
        num_scalar_prefetch=2,
        grid=(ALL_TOKENS // tm,),
        in_specs=[pl.BlockSpec((tm * ROW_CHUNKS, LANES), lambda i, p, t: (i, 0))],
        out_specs=pl.BlockSpec(memory_space=pl.ANY),
        scratch_shapes=[pltpu.VMEM((MOE_ROW_TILE * ROW_CHUNKS, LANES), F32), pltpu.SemaphoreType.DMA(())],
    )
    return pl.pallas_call(
        _dispatch_kernel,
        grid_spec=grid_spec,
        out_shape=jax.ShapeDtypeStruct((MOE_ROWS * ROW_CHUNKS, LANES), F32),
        compiler_params=_params(("arbitrary",)),
        name="moe_dispatch",
    )(pos, tail_rows, h_tiles)


TOP_K = 2
MOE_ROW_TILE = 512
MOE_ROWS = TOP_K * ALL_TOKENS + N_EXPERTS * MOE_ROW_TILE


def _route_plan(route):
    tm = MOE_ROW_TILE
    n_slots = TOP_K * ALL_TOKENS
    n_steps = MOE_ROWS // tm + 1
    expert = route[:, ROUTE_IDX_LANE:ROUTE_IDX_LANE + TOP_K].astype(jnp.int32).reshape(-1)
    experts = jnp.arange(N_EXPERTS, dtype=jnp.int32)[None, :]
    order = jnp.argsort(expert, stable=True).astype(jnp.int32)
    counts = jnp.sum((expert[:, None] == experts).astype(jnp.int32), axis=0)
    tiles = (counts + tm - 1) // tm
    tile_end = jnp.cumsum(tiles)
    starts = (tile_end - tiles) * tm
    before = jnp.cumsum(counts) - counts
    last = tile_end[-1] - 1
    step = jnp.arange(n_steps, dtype=jnp.int32)
    step_expert = jnp.sum((jnp.minimum(step, last)[:, None] >= tile_end[None, :]).astype(jnp.int32), axis=1)
    step_valid = (step <= last).astype(jnp.int32)
    row = jnp.arange((n_steps + 2) * tm, dtype=jnp.int32) - tm
    row_expert = jnp.sum(((row // tm)[:, None] >= tile_end[None, :]).astype(jnp.int32), axis=1)
    pick = (jnp.minimum(row_expert, N_EXPERTS - 1)[:, None] == experts).astype(jnp.int32)
    rank = row - jnp.sum(pick * starts[None, :], axis=1)
    real = (row >= 0) & (row_expert < N_EXPERTS) & (rank < jnp.sum(pick * counts[None, :], axis=1))
    sorted_pos = jnp.clip(jnp.sum(pick * before[None, :], axis=1) + rank, 0, n_slots - 1)
    slot = order[sorted_pos]
    row_slot = jnp.where(real, (slot % TOP_K) * ALL_TOKENS + slot // TOP_K, n_slots + row % tm)
    place = jnp.argsort(order).astype(jnp.int32)
    own = (expert[:, None] == experts).astype(jnp.int32)
    pos = place + jnp.sum(own * (starts - before)[None, :], axis=1)
    group_tails = jnp.maximum(starts + tiles * tm - tm, 0)
    never_full = (n_slots // tm + jnp.arange(N_EXPERTS, dtype=jnp.int32)) * tm
    clear_rows = jnp.concatenate([group_tails, never_full])
    return (pos.astype(jnp.int32), clear_rows.astype(jnp.int32), row_slot.astype(jnp.int32),
            jnp.minimum(step, last), step_expert.astype(jnp.int32), step_valid)


def _combine_kernel(x_ref, route_ref, mod_ref, g_ref, y0_ref, y1_ref, *out_refs, final):
    route = route_ref[...]
    tm = x_ref.shape[0]
    f = (route[:, ROUTE_W_LANE:ROUTE_W_LANE + 1] * _load_rows_from_tiles(y0_ref, (), tm)
         + route[:, ROUTE_W_LANE + 1:ROUTE_W_LANE + 2] * _load_rows_from_tiles(y1_ref, (), tm))
    x = x_ref[...] + mod_ref[0][5:6] * f
    if not final:
        out_refs[0][...] = x
        return
    ms = jnp.mean(x * x, axis=-1, keepdims=True)
    x = x * lax.rsqrt(ms + EPS) * g_ref[...]
    is_ctx = _is_ctx_tile()
    @pl.when(is_ctx)
    def _():
        out_refs[0][...] = x

    @pl.when(jnp.logical_not(is_ctx))
    def _():
        out_refs[1][...] = x


def _combine(x_all, y_slots, route, mod, final_g, final):
    tm = ROW_TILE
    n_tiles = ALL_TOKENS // tm
    row = lambda w: pl.BlockSpec((tm, w), lambda i: (i, 0))
    if final:
        out_specs = list(_stream_specs(D_MODEL))
        out_shape = [jax.ShapeDtypeStruct((CTX_TOKENS, D_MODEL), F32),
                     jax.ShapeDtypeStruct((DEC_TOKENS, D_MODEL), F32)]
    else:
        out_specs = [row(D_MODEL)]
        out_shape = [jax.ShapeDtypeStruct((ALL_TOKENS, D_MODEL), F32)]
    return pl.pallas_call(
        functools.partial(_combine_kernel, final=final),
        grid=(n_tiles,),
        in_specs=[row(D_MODEL), row(LANES),
                  pl.BlockSpec((1, 6, D_MODEL), lambda i: (_mod_row(i, tm), 0, 0)),
                  _resident((1, D_MODEL), lambda i: (0, 0)),
                  pl.BlockSpec((tm * ROW_CHUNKS, LANES), lambda i: (i, 0)),
                  pl.BlockSpec((tm * ROW_CHUNKS, LANES), lambda i: (n_tiles + i, 0))],
        out_specs=out_specs,
        out_shape=out_shape,
        compiler_params=_params(("arbitrary",)),
        name="moe_combine",
    )(x_all, route, mod, final_g, y_slots, y_slots)


def _rope_tables():
    rows = DEC_SEQ // GRID_W
    row = jnp.repeat(jnp.arange(rows, dtype=F32), GRID_W)
    col = jnp.tile(jnp.arange(GRID_W, dtype=F32), rows)
    inv = ROPE_THETA ** (-jnp.arange(0, AXIS_DIM, 2, dtype=F32) / AXIS_DIM)
    ang = jnp.concatenate([row[:, None] * inv, col[:, None] * inv], axis=-1)
    cos, sin = jnp.cos(ang), jnp.sin(ang)
    reps = LANES // HEAD_DIM
    cos_l = jnp.tile(jnp.concatenate([cos, cos], axis=-1), (1, reps))
    sin_l = jnp.tile(jnp.concatenate([-sin, sin], axis=-1), (1, reps))
    return cos_l, sin_l


def kernel(x_prompt, x_sample, cache_k, cache_v, state_ssm, c, c_ctx, w_ada, b_ada, norm1_g, norm2_g, w_in, q_norm_g, k_norm_g, ssm_a_re, ssm_a_im, ssm_log_dt, ssm_b_re, ssm_b_im, ssm_c_re, ssm_c_im, ssm_d, w_glu, w_attn_br, w_out, ffn_w_gate, ffn_w_up, ffn_w_down, moe_router, moe_w_gate, moe_w_up, moe_w_down, final_norm_g):
    P = dict(ssm_a_re=ssm_a_re, ssm_a_im=ssm_a_im, ssm_log_dt=ssm_log_dt, ssm_b_re=ssm_b_re,
             ssm_b_im=ssm_b_im, ssm_c_re=ssm_c_re, ssm_c_im=ssm_c_im)

    c_rows = jnp.zeros((SUBLANES, D_MODEL), F32).at[0].set(c_ctx).at[1:1 + DEC_BATCH].set(c)
    mod_all = _ada_modulation(c_rows, w_ada, b_ada)
    mod_all = mod_all[:, :N_MOD_ROWS].reshape(DEPTH, N_MOD_ROWS, 6, D_MODEL)

    rope_cos, rope_sin = _rope_tables()
    head_id = jnp.arange(MXU_WIDTH, dtype=jnp.int32) // HEAD_DIM
    ones_bd = (head_id[:, None] == head_id[None, :]).astype(BF16)

    cache_k4 = cache_k.reshape(DEC_BATCH, DEPTH, PAST_LEN, KV_W)
    cache_v4 = cache_v.reshape(DEC_BATCH, DEPTH, PAST_LEN, KV_W)
    chain = DEC_SEQ // S5_PIECE
    h0_ctx = jnp.zeros((N_DIRS, S5_N_SETS, BATCH, 2 * S5_SET_ST), F32)
    ks, vs, ss = [], [], []
    x_all = None
    assert DEPTH == 2
    rider_view = lambda w: w.reshape(-1, w.shape[-1])
    hosted = {
        (0, "attention"): dict(ffn_gate=ffn_w_gate[0], ffn_up=ffn_w_up[0], w_in=w_in[1],
                               w_attn_br=w_attn_br[1], w_glu=w_glu[1], w_out=w_out[1]),
        (0, "scan"): dict(moe_gate=moe_w_gate[0]),
        (1, "attention"): dict(moe_down=moe_w_down[0]),
        (1, "scan"): dict(moe_up=moe_w_up[0]),
    }
    narrowed = dict(w_in=w_in[0].astype(BF16), w_attn_br=w_attn_br[0].astype(BF16), w_glu=w_glu[0].astype(BF16),
                    w_out=w_out[0].astype(BF16), ffn_down=ffn_w_down[0].astype(BF16))

    def collect(host, results):
        for (name, wide), narrow in zip(hosted[host].items(), results):
            narrowed[name] = narrow.reshape(wide.shape)

    for l in range(DEPTH):
        mod = mod_all[l]
        first = x_all is None
        x_streams = ((x_prompt.reshape(CTX_TOKENS, D_MODEL), x_sample.reshape(DEC_TOKENS, D_MODEL), 0) if first
                     else (x_all, x_all, N_CTX_TILES))
        u, q, k_ctx, v_ctx, k_dec, v_dec, sgs, sga, *x_copy = _in_projection(
            *x_streams, mod, norm1_g[l][None, :], narrowed["w_in"], ones_bd,
            jnp.tile(q_norm_g[l], N_HEADS)[None, :], jnp.tile(k_norm_g[l], N_KV_HEADS)[None, :],
            rope_cos, rope_sin, first)
        if first:
            x_all = x_copy[0]
        mixer_weights = [narrowed[name] for name in ("w_attn_br", "w_glu", "w_out")]

        attn_ctx, *riders_done = _attention(q, k_ctx, v_ctx, BATCH, SEQ, 0, "attention_context",
                                            riders=[rider_view(w) for w in hosted[(l, "attention")].values()])
        collect((l, "attention"), riders_done)
        attn_dec, = _attention(q, k_dec, v_dec, DEC_BATCH, DEC_SEQ, CTX_TOKENS, "attention_latent",
                               cache=(cache_k4, cache_v4, l))
        ks.append(k_ctx.reshape(BATCH, SEQ, N_KV_HEADS, HEAD_DIM))
        vs.append(v_ctx.reshape(BATCH, SEQ, N_KV_HEADS, HEAD_DIM))

        bd, a_rows, cd = _s5_operators(l, P)
        d_skip = ssm_d[l][None, :]
        y_ctx, fin, *riders_done = _s5_branch(u, 0, CTX_TOKENS, bd, a_rows, cd, h0_ctx, d_skip, 1, "s5_context",
                                              riders=[rider_view(w) for w in hosted[(l, "scan")].values()])
        collect((l, "scan"), riders_done)
        h0_dec = state_ssm[:, l].reshape(DEC_BATCH, N_DIRS, S5_N_SETS, S5_GROUP_SET * STATE_N, 2)
        h0_dec = h0_dec.transpose(1, 2, 0, 4, 3).reshape(N_DIRS, S5_N_SETS, DEC_BATCH, 2 * S5_SET_ST)
        h0_dec = jnp.repeat(h0_dec, chain, axis=2)
        y_dec, _ = _s5_branch(u, CTX_TOKENS, DEC_TOKENS, bd, a_rows, cd, h0_dec, d_skip, chain, "s5_latent")
        fin = fin.reshape(N_DIRS, S5_N_SETS, BATCH, 2, S5_GROUP_SET, STATE_N)
        ss.append(fin.transpose(2, 0, 1, 4, 5, 3).reshape(BATCH, N_DIRS, N_GROUPS, STATE_N, 2))

        x_all = _mixer_out(attn_ctx, attn_dec, y_ctx, y_dec, sgs, sga, x_all, mod, *mixer_weights)

        i = l // 2
        last_layer = l == DEPTH - 1
        if l % 2 == 0:
            out = _dense_ffn(x_all, mod, norm2_g[l][None, :], narrowed["ffn_gate"], narrowed["ffn_up"],
                             narrowed["ffn_down"], final_norm_g[None, :], last_layer)
            x_all = out
            streams = (out[:CTX_TOKENS], out[CTX_TOKENS:])
        else:
            router_pad = jnp.pad(moe_router[i].astype(F32), ((0, 0), (0, LANES - N_EXPERTS)))
            h2, route = _prenorm_router(x_all, mod, norm2_g[l][None, :], router_pad)
            pos, clear_rows, row_slot, tile, expert, valid = _route_plan(route)
            x_sorted = _dispatch(h2, pos, clear_rows)
            y_slots = _moe_ffn(x_sorted, row_slot, narrowed["moe_gate"], narrowed["moe_up"],
                               narrowed["moe_down"], tile, expert, valid, D_FF_EXPERT // 2)
            out = _combine(x_all, y_slots, route, mod, final_norm_g[None, :], last_layer)
            if last_layer:
                streams = tuple(out)
            else:
                x_all = out[0]
                streams = (x_all[:CTX_TOKENS], x_all[CTX_TOKENS:])

    y_prompt = streams[0].reshape(BATCH, SEQ, D_MODEL)
    y_sample = streams[1].reshape(DEC_BATCH, DEC_SEQ, D_MODEL)
    return (y_prompt, y_sample, jnp.stack(ks, axis=1), jnp.stack(vs, axis=1), jnp.stack(ss, axis=1))
```
